```python
import math
import jax, jax.numpy as jnp
from jax import lax
import numpy as np

D_MODEL = 2048
BATCH = 4
SEQ = 4096
DEPTH = 1

CTX_LEN = 256
GRID_W = 64
N_HEADS = 16
HEAD_DIM = 128
N_KV_HEADS = 4
AXIS_DIM = HEAD_DIM // 2
Q_BLOCK = 128
ROPE_THETA = 10000.0
HYENA_WIDTH = D_MODEL // 2
SHORT_CONV = 3
FILTER_HIDDEN = 64
FILTER_EMB = 17
FILTER_BANDS = (FILTER_EMB - 1) // 2
DECAY_TARGET = 1e-2
FAST_DECAY_PCT = 0.3
SLOW_DECAY_PCT = 1.5
MIN_DECAY = math.log(DECAY_TARGET) / SLOW_DECAY_PCT
MAX_DECAY = math.log(DECAY_TARGET) / FAST_DECAY_PCT
D_FF = 4 * D_MODEL
EPS = 1e-6
Q_W = N_HEADS * HEAD_DIM
KV_W = N_KV_HEADS * HEAD_DIM
Q_OFF = 3 * HYENA_WIDTH
K_OFF = Q_OFF + Q_W
V_OFF = K_OFF + KV_W
GA_OFF = V_OFF + KV_W
GB_OFF = GA_OFF + D_MODEL
IN_W = GB_OFF + D_MODEL

kernel_name = "hybrid_hyena_gqa_dit_block"


def rmsnorm(x, g):
    xf = x.astype(jnp.float32)
    y = xf * lax.rsqrt(jnp.mean(xf * xf, axis=-1, keepdims=True) + EPS)
    return (y * g.astype(jnp.float32)).astype(x.dtype)


def modulate(h, shift, scale):
    return h * (1.0 + scale) + shift


def adaln(cond, w, b):
    return jax.nn.silu(cond) @ w + b


def axial_rope_tables(rows_count):
    rows = jnp.repeat(jnp.arange(rows_count), GRID_W)
    cols = jnp.tile(jnp.arange(GRID_W), rows_count)
    inv = ROPE_THETA ** (-jnp.arange(0, AXIS_DIM, 2, dtype=jnp.float32) / AXIS_DIM)
    ang = jnp.stack([rows[:, None] * inv, cols[:, None] * inv], axis=1)
    return jnp.cos(ang), jnp.sin(ang)


def apply_rope(x, cos, sin):
    B, L, H, Dh = x.shape
    xr = x.astype(jnp.float32).reshape(B, L, H, 2, AXIS_DIM)
    x1, x2 = xr[..., :AXIS_DIM // 2], xr[..., AXIS_DIM // 2:]
    c = cos[None, :, None]
    s = sin[None, :, None]
    out = jnp.concatenate([x1 * c - x2 * s, x2 * c + x1 * s], axis=-1)
    return out.reshape(B, L, H, Dh).astype(x.dtype)


def short_conv(u, w, b):
    up = jnp.pad(u, ((0, 0), (1, 1), (0, 0)))
    return up[:, :-2] * w[0] + up[:, 1:-1] * w[1] + up[:, 2:] * w[2] + b


def hyena_filter_spectrum(L, w1, b1, w2, b2, w3, b3, w4, freq):
    f32 = jnp.float32
    C = w4.shape[-1] // 2
    t = jnp.linspace(0.0, 1.0, L, dtype=f32)[:, None]
    wpos = 2.0 * math.pi * jnp.arange(L, dtype=f32)[:, None] / L
    bands = jnp.linspace(1e-4, FILTER_BANDS - 1, FILTER_BANDS, dtype=f32)
    emb = jnp.concatenate([t, jnp.cos(bands * wpos), -jnp.sin(bands * wpos)], axis=-1)
    fr = freq.astype(f32)
    h = jnp.sin(fr * (emb @ w1.astype(f32) + b1.astype(f32)))
    h = jnp.sin(fr * (h @ w2.astype(f32) + b2.astype(f32)))
    h = jnp.sin(fr * (h @ w3.astype(f32) + b3.astype(f32)))
    h = h @ w4.astype(f32)
    deltas = jnp.abs(jnp.linspace(MIN_DECAY, MAX_DECAY, C, dtype=f32))
    h = h.reshape(L, 2, C) * jnp.exp(-t * deltas)[:, None, :]
    k = jnp.concatenate([h[:, 0], jnp.zeros((1, C), f32), h[:0:-1, 1]], axis=0)
    k = k / jnp.sum(jnp.abs(k), axis=0, keepdims=True)
    return jnp.fft.rfft(k, axis=0)


def hyena_branch(u, conv_w, conv_b, fw1, fb1, fw2, fb2, fw3, fb3, fw4, ffreq, fbias):
    B, L, _ = u.shape
    u = short_conv(u, conv_w, conv_b)
    x0, x1, v = jnp.split(u, 3, axis=-1)
    kf = hyena_filter_spectrum(L, fw1, fb1, fw2, fb2, fw3, fb3, fw4, ffreq)
    z = (x1 * v).astype(jnp.float32)
    y = jnp.fft.irfft(jnp.fft.rfft(z, n=2 * L, axis=1) * kf[None], n=2 * L, axis=1)[:, :L]
    y = y + z * fbias.astype(jnp.float32)
    return x0 * y.astype(u.dtype)


def block_attention(q, k, v):
    B, L, H, Dh = q.shape
    KVH = k.shape[2]
    G = H // KVH
    nblk = L // Q_BLOCK
    qb = q.reshape(B, nblk, Q_BLOCK, KVH, G, Dh).transpose(1, 0, 2, 3, 4, 5)
    scale = Dh ** -0.5

    def one_block(qi):
        s = jnp.einsum('bqkgd,bskd->bkgqs', qi, k, preferred_element_type=jnp.float32) * scale
        p = jax.nn.softmax(s, axis=-1).astype(v.dtype)
        return jnp.einsum('bkgqs,bskd->bqkgd', p, v)

    o = lax.map(one_block, qb)
    return o.transpose(1, 0, 2, 3, 4, 5).reshape(B, L, H * Dh)


def context_kv(hc, w_in_l, k_gain):
    B, Lc, _ = hc.shape
    kv = hc @ w_in_l[:, K_OFF:GA_OFF]
    kc = rmsnorm(kv[..., :KV_W].reshape(B, Lc, N_KV_HEADS, HEAD_DIM), k_gain)
    vc = kv[..., KV_W:].reshape(B, Lc, N_KV_HEADS, HEAD_DIM)
    return kc, vc


def token_mixer(proj, ctx_keys, ctx_values, rope_cos, rope_sin, latent,
                conv_w, conv_b, fw1, fb1, fw2, fb2, fw3, fb3, fw4, ffreq, fbias,
                q_gain, k_gain, w_ba, w_bb, w_o):
    B, L, _ = proj.shape
    y_a = hyena_branch(proj[..., :Q_OFF], conv_w, conv_b, fw1, fb1, fw2, fb2, fw3, fb3, fw4, ffreq, fbias)
    q = rmsnorm(proj[..., Q_OFF:K_OFF].reshape(B, L, N_HEADS, HEAD_DIM), q_gain)
    k = rmsnorm(proj[..., K_OFF:V_OFF].reshape(B, L, N_KV_HEADS, HEAD_DIM), k_gain)
    v = proj[..., V_OFF:GA_OFF].reshape(B, L, N_KV_HEADS, HEAD_DIM)
    if latent:
        q = apply_rope(q, rope_cos, rope_sin)
        k = apply_rope(k, rope_cos, rope_sin)
        k = jnp.concatenate([k, ctx_keys], axis=1)
        v = jnp.concatenate([v, ctx_values], axis=1)
    y_b = block_attention(q, k, v)
    gate_a = jax.nn.sigmoid(proj[..., GA_OFF:GB_OFF])
    gate_b = jax.nn.sigmoid(proj[..., GB_OFF:])
    merged = gate_a * (y_a @ w_ba) + gate_b * (y_b @ w_bb)
    return merged @ w_o


def sq_relu_mlp(h, w1, w2):
    return jnp.square(jax.nn.relu(h @ w1)) @ w2


def setup_inputs(seed: int = 0) -> dict:
    key = jax.random.key(seed)
    ks = jax.random.split(key, 25)
    f32 = jnp.float32
    C = HYENA_WIDTH

    def nrm(k, shape, scale):
        return jax.random.normal(k, shape, f32) * scale

    return {
        "x": nrm(ks[0], (BATCH, SEQ, D_MODEL), 1.0),
        "c": nrm(ks[1], (BATCH, D_MODEL), 1.0),
        "ctx": nrm(ks[2], (BATCH, CTX_LEN, D_MODEL), 1.0),
        "c_ctx": nrm(ks[3], (D_MODEL,), 1.0),
        "w_ada": nrm(ks[4], (DEPTH, D_MODEL, 6 * D_MODEL), 0.5 * D_MODEL ** -0.5),
        "b_ada": nrm(ks[5], (DEPTH, 6 * D_MODEL), 0.02),
        "norm_gains": 1.0 + nrm(ks[6], (DEPTH, 4, D_MODEL), 0.05),
        "w_in": nrm(ks[7], (DEPTH, D_MODEL, IN_W), D_MODEL ** -0.5),
        "conv_w": nrm(ks[8], (DEPTH, SHORT_CONV, 3 * C), SHORT_CONV ** -0.5),
        "conv_b": nrm(ks[9], (DEPTH, 3 * C), 0.02),
        "filt_w1": nrm(ks[10], (DEPTH, FILTER_EMB, FILTER_HIDDEN), FILTER_EMB ** -0.5),
        "filt_b1": nrm(ks[11], (DEPTH, FILTER_HIDDEN), 0.1),
        "filt_w2": nrm(ks[12], (DEPTH, FILTER_HIDDEN, FILTER_HIDDEN), FILTER_HIDDEN ** -0.5),
        "filt_b2": nrm(ks[13], (DEPTH, FILTER_HIDDEN), 0.1),
        "filt_w3": nrm(ks[14], (DEPTH, FILTER_HIDDEN, FILTER_HIDDEN), FILTER_HIDDEN ** -0.5),
        "filt_b3": nrm(ks[15], (DEPTH, FILTER_HIDDEN), 0.1),
        "filt_w4": nrm(ks[16], (DEPTH, FILTER_HIDDEN, 2 * C), FILTER_HIDDEN ** -0.5),
        "filt_freq": 1.0 + nrm(ks[17], (DEPTH, FILTER_HIDDEN), 0.1),
        "filt_bias": nrm(ks[18], (DEPTH, C), 1.0),
        "qk_gains": 1.0 + nrm(ks[19], (DEPTH, 2, HEAD_DIM), 0.05),
        "w_branch_a": nrm(ks[20], (DEPTH, C, D_MODEL), C ** -0.5),
        "w_branch_b": nrm(ks[21], (DEPTH, Q_W, D_MODEL), Q_W ** -0.5),
        "w_out": nrm(ks[22], (DEPTH, D_MODEL, D_MODEL), D_MODEL ** -0.5),
        "w_ff1": nrm(ks[23], (DEPTH, D_MODEL, D_FF), D_MODEL ** -0.5),
        "w_ff2": nrm(ks[24], (DEPTH, D_FF, D_MODEL), D_FF ** -0.5),
    }


def reference(x, c, ctx, c_ctx, w_ada, b_ada, norm_gains, w_in, conv_w, conv_b,
              filt_w1, filt_b1, filt_w2, filt_b2, filt_w3, filt_b3, filt_w4, filt_freq, filt_bias,
              qk_gains, w_branch_a, w_branch_b, w_out, w_ff1, w_ff2):
    L = x.shape[1]
    ROWS = L // GRID_W
    rope_cos, rope_sin = axial_rope_tables(ROWS)
    xc = ctx
    for l in range(DEPTH):
        update_ctx = l < DEPTH - 1
        mod = adaln(c, w_ada[l], b_ada[l])[:, None, :]
        mod_c = adaln(c_ctx, w_ada[l], b_ada[l])[None, None, :]
        sh1, sc1, gt1, sh2, sc2, gt2 = jnp.split(mod, 6, axis=-1)
        csh1, csc1, cgt1, csh2, csc2, cgt2 = jnp.split(mod_c, 6, axis=-1)
        g_pre1, g_post1, g_pre2, g_post2 = norm_gains[l]
        mixer_w = (conv_w[l], conv_b[l], filt_w1[l], filt_b1[l], filt_w2[l], filt_b2[l],
                   filt_w3[l], filt_b3[l], filt_w4[l], filt_freq[l], filt_bias[l],
                   qk_gains[l, 0], qk_gains[l, 1], w_branch_a[l], w_branch_b[l], w_out[l])

        h = modulate(rmsnorm(x, g_pre1), sh1, sc1)
        hc = modulate(rmsnorm(xc, g_pre1), csh1, csc1)
        kc, vc = context_kv(hc, w_in[l], qk_gains[l, 1])
        mix = token_mixer(h @ w_in[l], kc, vc, rope_cos, rope_sin, True, *mixer_w)
        x = x + gt1 * rmsnorm(mix, g_post1)
        if update_ctx:
            mix_c = token_mixer(hc @ w_in[l], None, None, None, None, False, *mixer_w)
            xc = xc + cgt1 * rmsnorm(mix_c, g_post1)

        h2 = modulate(rmsnorm(x, g_pre2), sh2, sc2)
        x = x + gt2 * rmsnorm(sq_relu_mlp(h2, w_ff1[l], w_ff2[l]), g_post2)
        if update_ctx:
            h2c = modulate(rmsnorm(xc, g_pre2), csh2, csc2)
            xc = xc + cgt2 * rmsnorm(sq_relu_mlp(h2c, w_ff1[l], w_ff2[l]), g_post2)
    return x
```

```python
import functools
import math

import jax
import jax.numpy as jnp
from jax import lax
from jax.experimental import pallas as pl
from jax.experimental.pallas import tpu as pltpu

F32 = jnp.float32
BF16 = jnp.bfloat16

D_MODEL = 2048
CTX_LEN = 256
GRID_W = 64
N_HEADS = 16
HEAD_DIM = 128
N_KV_HEADS = 4
GROUP = N_HEADS // N_KV_HEADS
AXIS_DIM = HEAD_DIM // 2
ROPE_THETA = 10000.0
HYENA_WIDTH = D_MODEL // 2
FILTER_HIDDEN = 64
FILTER_EMB = 17
FILTER_BANDS = (FILTER_EMB - 1) // 2
DECAY_TARGET = 1e-2
MIN_DECAY = math.log(DECAY_TARGET) / 1.5
MAX_DECAY = math.log(DECAY_TARGET) / 0.3
D_FF = 4 * D_MODEL
EPS = 1e-6
Q_W = N_HEADS * HEAD_DIM
KV_W = N_KV_HEADS * HEAD_DIM
Q_OFF = 3 * HYENA_WIDTH
K_OFF = Q_OFF + Q_W
V_OFF = K_OFF + KV_W
GA_OFF = V_OFF + KV_W
GB_OFF = GA_OFF + D_MODEL

LANES = 128
SUBLANES = 8
VMEM_LIMIT = 56 * 1024 * 1024
EMB_PAD = 128
MOD_ROWS = 8
HIGHEST = lax.Precision.HIGHEST


def _params(*sem):
    return pltpu.CompilerParams(dimension_semantics=sem, vmem_limit_bytes=VMEM_LIMIT)


def _dot(a, b):
    return jnp.dot(a, b, preferred_element_type=F32)


def _rms(x, g):
    return x * lax.rsqrt(jnp.mean(x * x, axis=-1, keepdims=True) + EPS) * g


def _adaln_kernel(c_ref, w_ref, b_ref, o_ref):
    c = c_ref[...]
    s = c * jax.nn.sigmoid(c)
    o_ref[...] = _dot(s.astype(BF16), w_ref[...].astype(BF16)) + b_ref[...]


def _adaln(cin, w, b, tn=1024):
    rows, d = cin.shape
    n = w.shape[1]
    return pl.pallas_call(
        _adaln_kernel,
        grid=(n // tn,),
        in_specs=[pl.BlockSpec((rows, d), lambda j: (0, 0)),
                  pl.BlockSpec((d, tn), lambda j: (0, j)),
                  pl.BlockSpec((1, tn), lambda j: (0, j))],
        out_specs=pl.BlockSpec((rows, tn), lambda j: (0, j)),
        out_shape=jax.ShapeDtypeStruct((rows, n), F32),
        compiler_params=_params("arbitrary"),
        name="adaln",
    )(cin, w, b)


def _mod_row(i, tiles_per_mod, mod_base):
    if tiles_per_mod is None:
        return mod_base
    return mod_base + i // tiles_per_mod


def _prenorm_kernel(x_ref, mod_ref, g_ref, o_ref, *, tiles_per_mod, mod_base):
    b = _mod_row(pl.program_id(0), tiles_per_mod, mod_base)
    y = _rms(x_ref[...], g_ref[...])
    sh = mod_ref[pl.ds(b, 1), 0:D_MODEL]
    sc = mod_ref[pl.ds(b, 1), D_MODEL:2 * D_MODEL]
    o_ref[...] = (y * (1.0 + sc) + sh).astype(o_ref.dtype)


def _prenorm(x, mod, g, *, tm, tiles_per_mod, mod_base):
    t, d = x.shape
    return pl.pallas_call(
        functools.partial(_prenorm_kernel, tiles_per_mod=tiles_per_mod, mod_base=mod_base),
        grid=(t // tm,),
        in_specs=[pl.BlockSpec((tm, d), lambda i: (i, 0)),
                  pl.BlockSpec(mod.shape, lambda i: (0, 0)),
                  pl.BlockSpec((1, d), lambda i: (0, 0))],
        out_specs=pl.BlockSpec((tm, d), lambda i: (i, 0)),
        out_shape=jax.ShapeDtypeStruct((t, d), BF16),
        compiler_params=_params("arbitrary"),
        name="prenorm",
    )(x, mod, g)


def _matmul_kernel(h_ref, w_ref, o_ref):
    o_ref[...] = _dot(h_ref[...], w_ref[...]).astype(o_ref.dtype)


def _matmul(h, w, out_dtype, *, tm, tn, name):
    t, d = h.shape
    n = w.shape[1]
    return pl.pallas_call(
        _matmul_kernel,
        grid=(t // tm, n // tn),
        in_specs=[pl.BlockSpec((tm, d), lambda i, j: (i, 0)),
                  pl.BlockSpec((d, tn), lambda i, j: (0, j))],
        out_specs=pl.BlockSpec((tm, tn), lambda i, j: (i, j)),
        out_shape=jax.ShapeDtypeStruct((t, n), out_dtype),
        compiler_params=_params("arbitrary", "arbitrary"),
        name=name,
    )(h, w)


def _proj_qk_kernel(*refs, heads, scale, rope):
    if rope:
        h_ref, w_ref, g_ref, cos_ref, sin_ref, o_ref = refs
    else:
        h_ref, w_ref, g_ref, o_ref = refs
    acc = _dot(h_ref[...], w_ref[...])
    g = g_ref[...]
    tm = acc.shape[0]
    if rope:
        cos = cos_ref[...]
        sin = sin_ref[...]
        lane = lax.broadcasted_iota(jnp.int32, (tm, HEAD_DIM), 1)
        first = (lane & (AXIS_DIM // 2)) == 0
    for hh in range(heads):
        y = _rms(acc[:, hh * HEAD_DIM:(hh + 1) * HEAD_DIM], g)
        if rope:
            partner = jnp.where(first,
                                pltpu.roll(y, HEAD_DIM - AXIS_DIM // 2, 1),
                                pltpu.roll(y, AXIS_DIM // 2, 1))
            y = y * cos + partner * sin
        if scale != 1.0:
            y = y * scale
        o_ref[:, hh * HEAD_DIM:(hh + 1) * HEAD_DIM] = y.astype(o_ref.dtype)


def _proj_qk(h, w, g, cos, sin, *, tm, tn, scale, rope, name):
    t, d = h.shape
    n = w.shape[1]
    in_specs = [pl.BlockSpec((tm, d), lambda i, j: (i, 0)),
                pl.BlockSpec((d, tn), lambda i, j: (0, j)),
                pl.BlockSpec((1, HEAD_DIM), lambda i, j: (0, 0))]
    args = [h, w, g]
    if rope:
        seq_tiles = cos.shape[0] // tm
        in_specs += [pl.BlockSpec((tm, HEAD_DIM), lambda i, j: (i % seq_tiles, 0))] * 2
        args += [cos, sin]
    return pl.pallas_call(
        functools.partial(_proj_qk_kernel, heads=tn // HEAD_DIM, scale=scale, rope=rope),
        grid=(t // tm, n // tn),
        in_specs=in_specs,
        out_specs=pl.BlockSpec((tm, tn), lambda i, j: (i, j)),
        out_shape=jax.ShapeDtypeStruct((t, n), BF16),
        compiler_params=_params("arbitrary", "arbitrary"),
        name=name,
    )(*args)


def _dot_hi(a, b):
    return jnp.dot(a, b, precision=HIGHEST, preferred_element_type=F32)


def _filter_hidden_kernel(emb_ref, w1_ref, b1_ref, w2_ref, b2_ref, w3_ref, b3_ref, fr_ref, o_ref):
    fr = fr_ref[...]
    h = jnp.sin(fr * (_dot_hi(emb_ref[0], w1_ref[...]) + b1_ref[...]))
    h = jnp.sin(fr * (_dot_hi(h, w2_ref[...]) + b2_ref[...]))
    o_ref[0] = jnp.sin(fr * (_dot_hi(h, w3_ref[...]) + b3_ref[...]))


def _filter_hidden(emb, w1, b1, w2, b2, w3, b3, fr, *, tr=256):
    sides, L, e = emb.shape
    full = lambda a: pl.BlockSpec(a.shape, lambda s, r: (0,) * a.ndim)
    return pl.pallas_call(
        _filter_hidden_kernel,
        grid=(sides, L // tr),
        in_specs=[pl.BlockSpec((1, tr, e), lambda s, r: (s, r, 0)),
                  full(w1), full(b1), full(w2), full(b2), full(w3), full(b3), full(fr)],
        out_specs=pl.BlockSpec((1, tr, FILTER_HIDDEN), lambda s, r: (s, r, 0)),
        out_shape=jax.ShapeDtypeStruct((sides, L, FILTER_HIDDEN), F32),
        compiler_params=_params("arbitrary", "arbitrary"),
        name="filter_hidden",
    )(emb, w1, b1, w2, b2, w3, b3, fr)


def _filter_kernel(hid_ref, emb_ref, w4f_ref, w4b_ref, dl_ref, k_ref, norm_ref):
    r = pl.program_id(1)
    dl = dl_ref[...]
    kf = _dot_hi(hid_ref[0], w4f_ref[...]) * jnp.exp(-emb_ref[0, :, 0:1] * dl)
    kb = _dot_hi(hid_ref[1], w4b_ref[...]) * jnp.exp(-emb_ref[1, :, 0:1] * dl)
    tr = kb.shape[0]
    row = r * tr + lax.broadcasted_iota(jnp.int32, kb.shape, 0)
    kb = jnp.where(row == 0, 0.0, kb)
    k_ref[0] = kf.astype(k_ref.dtype)
    k_ref[1] = kb.astype(k_ref.dtype)

    @pl.when(r == 0)
    def _():
        norm_ref[...] = jnp.zeros(norm_ref.shape, F32)

    norm_ref[...] += jnp.sum(jnp.abs(kf), axis=0, keepdims=True) + jnp.sum(jnp.abs(kb), axis=0, keepdims=True)


def _hyena_filter(hid, emb, w4, deltas, *, tr=512, tc=256):
    _, L, e = emb.shape
    C = deltas.shape[1]
    return pl.pallas_call(
        _filter_kernel,
        grid=(C // tc, L // tr),
        in_specs=[pl.BlockSpec((2, tr, FILTER_HIDDEN), lambda c, r: (0, r, 0)),
                  pl.BlockSpec((2, tr, e), lambda c, r: (0, r, 0)),
                  pl.BlockSpec((FILTER_HIDDEN, tc), lambda c, r: (0, c)),
                  pl.BlockSpec((FILTER_HIDDEN, tc), lambda c, r: (0, c + C // tc)),
                  pl.BlockSpec((1, tc), lambda c, r: (0, c))],
        out_specs=[pl.BlockSpec((2, tr, tc), lambda c, r: (0, r, c)),
                   pl.BlockSpec((1, tc), lambda c, r: (0, c))],
        out_shape=[jax.ShapeDtypeStruct((2, L, C), BF16), jax.ShapeDtypeStruct((1, C), F32)],
        compiler_params=_params("arbitrary", "arbitrary"),
        name="hyena_filter",
    )(hid, emb, w4, w4, deltas)


def _hyena_pre_kernel(x0m, x0p, x0n, x1m, x1p, x1n, vm, vp, vn, w0, w1, w2, b0, b1, b2, fb,
                      z_ref, zb_ref, x0_ref, *, tiles_per_seq):
    pos = pl.program_id(0) % tiles_per_seq
    first = pos == 0
    last = pos == tiles_per_seq - 1
    tr, tc = x0m.shape
    row = lax.broadcasted_iota(jnp.int32, (tr, tc), 0)

    def conv(m_ref, p_ref, n_ref, w_ref, b_ref):
        u = m_ref[...]
        prev = jnp.where(first, 0.0, p_ref[SUBLANES - 1:SUBLANES, :])
        nxt = jnp.where(last, 0.0, n_ref[0:1, :])
        um = jnp.where(row == 0, prev, pltpu.roll(u, 1, 0))
        up = jnp.where(row == tr - 1, nxt, pltpu.roll(u, tr - 1, 0))
        w = w_ref[...]
        return um * w[0:1] + u * w[1:2] + up * w[2:3] + b_ref[...]

    x0 = conv(x0m, x0p, x0n, w0, b0)
    z = conv(x1m, x1p, x1n, w1, b1) * conv(vm, vp, vn, w2, b2)
    z_ref[...] = z.astype(z_ref.dtype)
    zb_ref[...] = z * fb[...]
    x0_ref[...] = x0


def _hyena_pre(u, conv_w, conv_b, fbias, *, seq, tr=512, tc=512):
    t = u.shape[0]
    C = HYENA_WIDTH
    cb = C // tc
    rb = tr // SUBLANES
    last_rb = t // SUBLANES - 1
    in_specs, args = [], []
    for part in range(3):
        off = part * cb
        in_specs += [pl.BlockSpec((tr, tc), lambda i, j, off=off: (i, j + off)),
                     pl.BlockSpec((SUBLANES, tc), lambda i, j, off=off: (jnp.maximum(i * rb - 1, 0), j + off)),
                     pl.BlockSpec((SUBLANES, tc), lambda i, j, off=off: (jnp.minimum((i + 1) * rb, last_rb), j + off))]
        args += [u, u, u]
    for part in range(3):
        in_specs.append(pl.BlockSpec((3, tc), lambda i, j, off=part * cb: (0, j + off)))
        args.append(conv_w)
    for part in range(3):
        in_specs.append(pl.BlockSpec((1, tc), lambda i, j, off=part * cb: (0, j + off)))
        args.append(conv_b)
    in_specs.append(pl.BlockSpec((1, tc), lambda i, j: (0, j)))
    args.append(fbias)
    out_spec = pl.BlockSpec((tr, tc), lambda i, j: (i, j))
    return pl.pallas_call(
        functools.partial(_hyena_pre_kernel, tiles_per_seq=seq // tr),
        grid=(t // tr, cb),
        in_specs=in_specs,
        out_specs=[out_spec, out_spec, out_spec],
        out_shape=[jax.ShapeDtypeStruct((t, C), BF16),
                   jax.ShapeDtypeStruct((t, C), F32),
                   jax.ShapeDtypeStruct((t, C), F32)],
        compiler_params=_params("arbitrary", "arbitrary"),
        name="hyena_pre",
    )(*args)


def _dft_fwd_kernel(cm_ref, sm_ref, z_ref, zr_ref, zi_ref):
    z = z_ref[0]
    zr_ref[0] = _dot(cm_ref[...], z)
    zi_ref[0] = _dot(sm_ref[...], z)


def _dft_fwd(cm, sm, z, *, tf=512, tc=512):
    nb, L, C = z.shape
    mat = pl.BlockSpec((tf, L), lambda f, b, c: (f, 0))
    out = pl.BlockSpec((1, tf, tc), lambda f, b, c: (b, f, c))
    return pl.pallas_call(
        _dft_fwd_kernel,
        grid=(L // tf, nb, C // tc),
        in_specs=[mat, mat, pl.BlockSpec((1, L, tc), lambda f, b, c: (b, 0, c))],
        out_specs=[out, out],
        out_shape=[jax.ShapeDtypeStruct((nb, L, C), F32)] * 2,
        compiler_params=_params("arbitrary", "arbitrary", "arbitrary"),
        name="dft_filter",
    )(cm, sm, z)


def _dft_mul_kernel(cm_ref, sm_ref, z_ref, kr_ref, ki_ref, norm_ref, yr_ref, yi_ref, *, n_points):
    z = z_ref[0]
    zr = _dot(cm_ref[...], z)
    zi = _dot(sm_ref[...], z)
    tf = zr.shape[0]
    freq = pl.program_id(0) * tf + lax.broadcasted_iota(jnp.int32, zr.shape, 0)
    sign = (1 - 2 * (freq & 1)).astype(F32)
    kr = kr_ref[0] + sign * kr_ref[1]
    ki = ki_ref[0] + sign * ki_ref[1]
    dc = freq == 0
    scale = jnp.where(dc, 1.0 / n_points, 2.0 / n_points) / norm_ref[...]
    rr = zr * kr
    ii = zi * ki
    yr_ref[0] = (jnp.where(dc, rr, rr - ii) * scale).astype(yr_ref.dtype)
    yi_ref[0] = (jnp.where(dc, ii, zr * ki + zi * kr) * scale).astype(yi_ref.dtype)


def _dft_mul(cm, sm, z, kr, ki, norm, *, tf=512, tc=512):
    nb, L, C = z.shape
    mat = pl.BlockSpec((tf, L), lambda f, b, c: (f, 0))
    spec = pl.BlockSpec((2, tf, tc), lambda f, b, c: (0, f, c))
    out = pl.BlockSpec((1, tf, tc), lambda f, b, c: (b, f, c))
    return pl.pallas_call(
        functools.partial(_dft_mul_kernel, n_points=2 * L),
        grid=(L // tf, nb, C // tc),
        in_specs=[mat, mat, pl.BlockSpec((1, L, tc), lambda f, b, c: (b, 0, c)), spec, spec,
                  pl.BlockSpec((1, tc), lambda f, b, c: (0, c))],
        out_specs=[out, out],
        out_shape=[jax.ShapeDtypeStruct((nb, L, C), BF16)] * 2,
        compiler_params=_params("arbitrary", "arbitrary", "arbitrary"),
        name="dft_signal",
    )(cm, sm, z, kr, ki, norm)


def _dft_inv_kernel(cm_ref, si_ref, yr_ref, yi_ref, x0_ref, zb_ref, o_ref):
    y = _dot(cm_ref[...], yr_ref[0]) + _dot(si_ref[...], yi_ref[0])
    o_ref[0] = (x0_ref[0] * (y + zb_ref[0])).astype(o_ref.dtype)


def _dft_inv(cm, si, yr, yi, x0, zb, *, tt=512, tc=512):
    nb, L, C = yr.shape
    mat = pl.BlockSpec((tt, L), lambda t, b, c: (t, 0))
    col = pl.BlockSpec((1, L, tc), lambda t, b, c: (b, 0, c))
    tile = pl.BlockSpec((1, tt, tc), lambda t, b, c: (b, t, c))
    return pl.pallas_call(
        _dft_inv_kernel,
        grid=(L // tt, nb, C // tc),
        in_specs=[mat, mat, col, col, tile, tile],
        out_specs=tile,
        out_shape=jax.ShapeDtypeStruct((nb, L, C), BF16),
        compiler_params=_params("arbitrary", "arbitrary", "arbitrary"),
        name="dft_inverse",
    )(cm, si, yr, yi, x0, zb)


def _attn_kernel(q_ref, k_ref, v_ref, o_ref, m_sc, l_sc, acc_sc, *, tq, tk, n_full, tail):
    qs = jnp.concatenate([q_ref[0, :, g * HEAD_DIM:(g + 1) * HEAD_DIM] for g in range(GROUP)], axis=0)
    m_sc[...] = jnp.full(m_sc.shape, -jnp.inf, F32)
    l_sc[...] = jnp.zeros(l_sc.shape, F32)
    acc_sc[...] = jnp.zeros(acc_sc.shape, F32)

    def step(k, v):
        s = lax.dot_general(qs, k, (((1,), (1,)), ((), ())), preferred_element_type=F32)
        m_prev = m_sc[...]
        m_new = jnp.maximum(m_prev, jnp.max(s, axis=-1, keepdims=True))
        alpha = jnp.exp2(m_prev - m_new)
        p = jnp.exp2(s - m_new)
        l_sc[...] = alpha * l_sc[...] + jnp.sum(p, axis=-1, keepdims=True)
        acc_sc[...] = alpha * acc_sc[...] + _dot(p.astype(BF16), v)
        m_sc[...] = m_new

    def body(c, carry):
        start = pl.multiple_of(c * tk, tk)
        step(k_ref[0, pl.ds(start, tk), :], v_ref[0, pl.ds(start, tk), :])
        return carry

    lax.fori_loop(0, n_full, body, 0)
    if tail:
        step(k_ref[0, n_full * tk:n_full * tk + tail, :], v_ref[0, n_full * tk:n_full * tk + tail, :])
    out = acc_sc[...] / l_sc[...]
    for g in range(GROUP):
        o_ref[0, :, g * HEAD_DIM:(g + 1) * HEAD_DIM] = out[g * tq:(g + 1) * tq].astype(o_ref.dtype)


def _attention(q, k, v, *, tq=256, tk=512):
    B, L, _ = q.shape
    S = k.shape[1]
    gw = GROUP * HEAD_DIM
    kv_spec = pl.BlockSpec((1, S, HEAD_DIM), lambda b, h, i: (b, 0, h))
    q_spec = pl.BlockSpec((1, tq, gw), lambda b, h, i: (b, i, h))
    rows = GROUP * tq
    return pl.pallas_call(
        functools.partial(_attn_kernel, tq=tq, tk=tk, n_full=S // tk, tail=S % tk),
        grid=(B, N_KV_HEADS, L // tq),
        in_specs=[q_spec, kv_spec, kv_spec],
        out_specs=q_spec,
        out_shape=jax.ShapeDtypeStruct((B, L, Q_W), BF16),
        scratch_shapes=[pltpu.VMEM((rows, 1), F32), pltpu.VMEM((rows, 1), F32),
                        pltpu.VMEM((rows, HEAD_DIM), F32)],
        compiler_params=_params("arbitrary", "arbitrary", "arbitrary"),
        name="attention",
    )(q, k, v)


def _merge_kernel(h_ref, ya_ref, yb_ref, wga_ref, wgb_ref, wba_ref, wbb_ref, o_ref):
    h = h_ref[...]
    ga = jax.nn.sigmoid(_dot(h, wga_ref[...]))
    gb = jax.nn.sigmoid(_dot(h, wgb_ref[...]))
    a = _dot(ya_ref[...], wba_ref[...])
    b = _dot(yb_ref[...], wbb_ref[...])
    o_ref[...] = (ga * a + gb * b).astype(o_ref.dtype)


def _merge(h, ya, yb, wga, wgb, wba, wbb, *, tm=1024, tn=256):
    t, d = h.shape
    row = lambda w: pl.BlockSpec((tm, w), lambda i, j: (i, 0))
    col = lambda kdim: pl.BlockSpec((kdim, tn), lambda i, j: (0, j))
    return pl.pallas_call(
        _merge_kernel,
        grid=(t // tm, d // tn),
        in_specs=[row(d), row(ya.shape[1]), row(yb.shape[1]),
                  col(d), col(d), col(ya.shape[1]), col(yb.shape[1])],
        out_specs=pl.BlockSpec((tm, tn), lambda i, j: (i, j)),
        out_shape=jax.ShapeDtypeStruct((t, d), BF16),
        compiler_params=_params("arbitrary", "arbitrary"),
        name="merge",
    )(h, ya, yb, wga, wgb, wba, wbb)


def _outproj_kernel(m_ref, wo_ref, x_ref, mod_ref, g_ref, x1_ref, h2_ref, *, tiles_per_mod):
    b = _mod_row(pl.program_id(0), tiles_per_mod, 0)
    mix = _dot(m_ref[...], wo_ref[...])
    gt1 = mod_ref[pl.ds(b, 1), 2 * D_MODEL:3 * D_MODEL]
    sh2 = mod_ref[pl.ds(b, 1), 3 * D_MODEL:4 * D_MODEL]
    sc2 = mod_ref[pl.ds(b, 1), 4 * D_MODEL:5 * D_MODEL]
    x1 = x_ref[...] + gt1 * _rms(mix, g_ref[1:2, :])
    x1_ref[...] = x1
    h2_ref[...] = (_rms(x1, g_ref[2:3, :]) * (1.0 + sc2) + sh2).astype(h2_ref.dtype)


def _outproj(merged, wo, x, mod, gains, *, seq, tm=256):
    t, d = x.shape
    row = pl.BlockSpec((tm, d), lambda i: (i, 0))
    return pl.pallas_call(
        functools.partial(_outproj_kernel, tiles_per_mod=seq // tm),
        grid=(t // tm,),
        in_specs=[row, pl.BlockSpec((d, d), lambda i: (0, 0)), row,
                  pl.BlockSpec(mod.shape, lambda i: (0, 0)),
                  pl.BlockSpec(gains.shape, lambda i: (0, 0))],
        out_specs=[row, row],
        out_shape=[jax.ShapeDtypeStruct((t, d), F32), jax.ShapeDtypeStruct((t, d), BF16)],
        compiler_params=_params("arbitrary"),
        name="outproj",
    )(merged, wo, x, mod, gains)


def _mlp_kernel(h2_ref, w1_ref, w2_ref, x1_ref, mod_ref, g_ref, o_ref, acc_ref, *, tiles_per_mod):
    j = pl.program_id(1)

    @pl.when(j == 0)
    def _():
        acc_ref[...] = jnp.zeros(acc_ref.shape, F32)

    hid = jnp.maximum(_dot(h2_ref[...], w1_ref[...]), 0.0)
    acc_ref[...] += _dot((hid * hid).astype(BF16), w2_ref[...])

    @pl.when(j == pl.num_programs(1) - 1)
    def _():
        b = _mod_row(pl.program_id(0), tiles_per_mod, 0)
        gt2 = mod_ref[pl.ds(b, 1), 5 * D_MODEL:6 * D_MODEL]
        o_ref[...] = x1_ref[...] + gt2 * _rms(acc_ref[...], g_ref[3:4, :])


def _mlp(h2, w1, w2, x1, mod, gains, *, seq, tm=512, tf=512):
    t, d = x1.shape
    f = w1.shape[1]
    row = pl.BlockSpec((tm, d), lambda i, j: (i, 0))
    return pl.pallas_call(
        functools.partial(_mlp_kernel, tiles_per_mod=seq // tm),
        grid=(t // tm, f // tf),
        in_specs=[row, pl.BlockSpec((d, tf), lambda i, j: (0, j)), pl.BlockSpec((tf, d), lambda i, j: (j, 0)),
                  row, pl.BlockSpec(mod.shape, lambda i, j: (0, 0)),
                  pl.BlockSpec(gains.shape, lambda i, j: (0, 0))],
        out_specs=row,
        out_shape=jax.ShapeDtypeStruct((t, d), F32),
        scratch_shapes=[pltpu.VMEM((tm, d), F32)],
        compiler_params=_params("arbitrary", "arbitrary"),
        name="mlp",
    )(h2, w1, w2, x1, mod, gains)


def _rope_tables(seq):
    rows = jnp.repeat(jnp.arange(seq // GRID_W), GRID_W)
    cols = jnp.tile(jnp.arange(GRID_W), seq // GRID_W)
    inv = ROPE_THETA ** (-jnp.arange(0, AXIS_DIM, 2, dtype=F32) / AXIS_DIM)
    ar = rows[:, None] * inv
    ac = cols[:, None] * inv
    cos = jnp.concatenate([jnp.cos(ar), jnp.cos(ar), jnp.cos(ac), jnp.cos(ac)], axis=-1)
    sin = jnp.concatenate([-jnp.sin(ar), jnp.sin(ar), -jnp.sin(ac), jnp.sin(ac)], axis=-1)
    return cos, sin


def _filter_embedding(seq):
    t = jnp.linspace(0.0, 1.0, seq, dtype=F32)[:, None]
    wpos = 2.0 * math.pi * jnp.arange(seq, dtype=F32)[:, None] / seq
    bands = jnp.linspace(1e-4, FILTER_BANDS - 1, FILTER_BANDS, dtype=F32)
    emb = jnp.concatenate([t, jnp.cos(bands * wpos), -jnp.sin(bands * wpos)], axis=-1)
    emb = jnp.pad(emb, ((0, 0), (0, EMB_PAD - FILTER_EMB)))
    emb_b = jnp.concatenate([emb[:1], emb[:0:-1]], axis=0)
    return jnp.stack([emb, emb_b])


def _dft_tables(seq):
    idx = jnp.arange(seq, dtype=jnp.int32)
    phase = (idx[:, None] * idx[None, :]) & (2 * seq - 1)
    ang = phase.astype(F32) * (math.pi / seq)
    alt = (1 - 2 * (idx & 1)).astype(F32)
    cm = jnp.cos(ang)
    sn = jnp.sin(ang)
    sf = jnp.where(idx[:, None] == 0, alt[None, :], sn)
    si = jnp.where(idx[None, :] == 0, alt[:, None], sn)
    return cm.astype(BF16), sf.astype(BF16), si.astype(BF16)


def kernel(x, c, ctx, c_ctx, w_ada, b_ada, norm_gains, w_in, conv_w, conv_b, filt_w1, filt_b1, filt_w2, filt_b2, filt_w3, filt_b3, filt_w4, filt_freq, filt_bias, qk_gains, w_branch_a, w_branch_b, w_out, w_ff1, w_ff2):
    B, L, D = x.shape
    T = B * L
    C = HYENA_WIDTH
    lyr = 0
    gains = norm_gains[lyr]
    w_in_b = w_in[lyr].astype(BF16)

    cin = jnp.zeros((MOD_ROWS, D), F32).at[:B].set(c).at[B].set(c_ctx)
    mod = _adaln(cin, w_ada[lyr], b_ada[lyr][None])

    xf = x.reshape(T, D)
    tm_pre = 512
    h = _prenorm(xf, mod, gains[0:1], tm=tm_pre, tiles_per_mod=L // tm_pre, mod_base=0)
    hc = _prenorm(ctx.reshape(B * CTX_LEN, D), mod, gains[0:1], tm=B * CTX_LEN, tiles_per_mod=None, mod_base=B)

    cos, sin = _rope_tables(L)
    qg = qk_gains[lyr, 0][None]
    kg = qk_gains[lyr, 1][None]
    q_scale = HEAD_DIM ** -0.5 * math.log2(math.e)
    u = _matmul(h, w_in_b[:, :Q_OFF], F32, tm=1024, tn=1024, name="proj_u")
    q = _proj_qk(h, w_in_b[:, Q_OFF:K_OFF], qg, cos, sin, tm=1024, tn=512, scale=q_scale, rope=True, name="proj_q")
    k = _proj_qk(h, w_in_b[:, K_OFF:V_OFF], kg, cos, sin, tm=1024, tn=512, scale=1.0, rope=True, name="proj_k")
    v = _matmul(h, w_in_b[:, V_OFF:GA_OFF], BF16, tm=1024, tn=512, name="proj_v")
    kc = _proj_qk(hc, w_in_b[:, K_OFF:V_OFF], kg, None, None, tm=B * CTX_LEN, tn=512, scale=1.0, rope=False,
                  name="proj_kc")
    vc = _matmul(hc, w_in_b[:, V_OFF:GA_OFF], BF16, tm=B * CTX_LEN, tn=512, name="proj_vc")

    emb = _filter_embedding(L)
    w1p = jnp.pad(filt_w1[lyr], ((0, EMB_PAD - FILTER_EMB), (0, 0)))
    deltas = jnp.abs(jnp.linspace(MIN_DECAY, MAX_DECAY, C, dtype=F32))[None]
    hid = _filter_hidden(emb, w1p, filt_b1[lyr][None], filt_w2[lyr], filt_b2[lyr][None],
                         filt_w3[lyr], filt_b3[lyr][None], filt_freq[lyr][None])
    filt, norm = _hyena_filter(hid, emb, filt_w4[lyr], deltas)
    cm, sf, si = _dft_tables(L)
    kr, ki = _dft_fwd(cm, sf, filt)
    z, zb, x0 = _hyena_pre(u, conv_w[lyr], conv_b[lyr][None], filt_bias[lyr][None], seq=L)
    yr, yi = _dft_mul(cm, sf, z.reshape(B, L, C), kr, ki, norm)
    ya = _dft_inv(cm, si, yr, yi, x0.reshape(B, L, C), zb.reshape(B, L, C)).reshape(T, C)

    k_all = jnp.concatenate([k.reshape(B, L, KV_W), kc.reshape(B, CTX_LEN, KV_W)], axis=1)
    v_all = jnp.concatenate([v.reshape(B, L, KV_W), vc.reshape(B, CTX_LEN, KV_W)], axis=1)
    yb = _attention(q.reshape(B, L, Q_W), k_all, v_all).reshape(T, Q_W)

    merged = _merge(h, ya, yb, w_in_b[:, GA_OFF:GB_OFF], w_in_b[:, GB_OFF:],
                    w_branch_a[lyr].astype(BF16), w_branch_b[lyr].astype(BF16))
    x1, h2 = _outproj(merged, w_out[lyr].astype(BF16), xf, mod, gains, seq=L)
    out = _mlp(h2, w_ff1[lyr].astype(BF16), w_ff2[lyr].astype(BF16), x1, mod, gains, seq=L)
    return out.reshape(B, L, D)
```

```python
import functools
import math

import jax
import jax.numpy as jnp
from jax import lax
from jax.experimental import pallas as pl
from jax.experimental.pallas import tpu as pltpu

F32 = jnp.float32
BF16 = jnp.bfloat16

D_MODEL = 2048
CTX_LEN = 256
GRID_W = 64
N_HEADS = 16
HEAD_DIM = 128
N_KV_HEADS = 4
GROUP = N_HEADS // N_KV_HEADS
AXIS_DIM = HEAD_DIM // 2
ROPE_THETA = 10000.0
HYENA_WIDTH = D_MODEL // 2
FILTER_HIDDEN = 64
FILTER_EMB = 17
FILTER_BANDS = (FILTER_EMB - 1) // 2
DECAY_TARGET = 1e-2
MIN_DECAY = math.log(DECAY_TARGET) / 1.5
MAX_DECAY = math.log(DECAY_TARGET) / 0.3
D_FF = 4 * D_MODEL
EPS = 1e-6
Q_W = N_HEADS * HEAD_DIM
KV_W = N_KV_HEADS * HEAD_DIM
Q_OFF = 3 * HYENA_WIDTH
K_OFF = Q_OFF + Q_W
V_OFF = K_OFF + KV_W
GA_OFF = V_OFF + KV_W
GB_OFF = GA_OFF + D_MODEL

LANES = 128
SUBLANES = 8
VMEM_LIMIT = 56 * 1024 * 1024
EMB_PAD = 128
MOD_ROWS = 8
HIGHEST = lax.Precision.HIGHEST


def _params(*sem):
    return pltpu.CompilerParams(dimension_semantics=sem, vmem_limit_bytes=VMEM_LIMIT)


def _dot(a, b):
    return jnp.dot(a, b, preferred_element_type=F32)


def _rms(x, g):
    return x * lax.rsqrt(jnp.mean(x * x, axis=-1, keepdims=True) + EPS) * g


def _adaln_kernel(c_ref, w_ref, b_ref, o_ref):
    c = c_ref[...]
    s = c * jax.nn.sigmoid(c)
    o_ref[...] = _dot(s.astype(BF16), w_ref[...].astype(BF16)) + b_ref[...]


def _adaln(cin, w, b, tn=1024):
    rows, d = cin.shape
    n = w.shape[1]
    return pl.pallas_call(
        _adaln_kernel,
        grid=(n // tn,),
        in_specs=[pl.BlockSpec((rows, d), lambda j: (0, 0)),
                  pl.BlockSpec((d, tn), lambda j: (0, j)),
                  pl.BlockSpec((1, tn), lambda j: (0, j))],
        out_specs=pl.BlockSpec((rows, tn), lambda j: (0, j)),
        out_shape=jax.ShapeDtypeStruct((rows, n), F32),
        compiler_params=_params("arbitrary"),
        name="adaln",
    )(cin, w, b)


def _mod_row(i, tiles_per_mod, mod_base):
    if tiles_per_mod is None:
        return mod_base
    return mod_base + i // tiles_per_mod


def _prenorm_kernel(x_ref, mod_ref, g_ref, o_ref, *, tiles_per_mod, mod_base):
    b = _mod_row(pl.program_id(0), tiles_per_mod, mod_base)
    y = _rms(x_ref[...], g_ref[...])
    sh = mod_ref[pl.ds(b, 1), 0:D_MODEL]
    sc = mod_ref[pl.ds(b, 1), D_MODEL:2 * D_MODEL]
    o_ref[...] = (y * (1.0 + sc) + sh).astype(o_ref.dtype)


def _prenorm(x, mod, g, *, tm, tiles_per_mod, mod_base):
    t, d = x.shape
    return pl.pallas_call(
        functools.partial(_prenorm_kernel, tiles_per_mod=tiles_per_mod, mod_base=mod_base),
        grid=(t // tm,),
        in_specs=[pl.BlockSpec((tm, d), lambda i: (i, 0)),
                  pl.BlockSpec(mod.shape, lambda i: (0, 0)),
                  pl.BlockSpec((1, d), lambda i: (0, 0))],
        out_specs=pl.BlockSpec((tm, d), lambda i: (i, 0)),
        out_shape=jax.ShapeDtypeStruct((t, d), BF16),
        compiler_params=_params("arbitrary"),
        name="prenorm",
    )(x, mod, g)


def _matmul_kernel(h_ref, w_ref, o_ref):
    o_ref[...] = _dot(h_ref[...], w_ref[...]).astype(o_ref.dtype)


def _matmul(h, w, out_dtype, *, tm, tn, name):
    t, d = h.shape
    n = w.shape[1]
    return pl.pallas_call(
        _matmul_kernel,
        grid=(t // tm, n // tn),
        in_specs=[pl.BlockSpec((tm, d), lambda i, j: (i, 0)),
                  pl.BlockSpec((d, tn), lambda i, j: (0, j))],
        out_specs=pl.BlockSpec((tm, tn), lambda i, j: (i, j)),
        out_shape=jax.ShapeDtypeStruct((t, n), out_dtype),
        compiler_params=_params("arbitrary", "arbitrary"),
        name=name,
    )(h, w)


def _proj_qk_kernel(*refs, heads, scale, rope):
    if rope:
        h_ref, w_ref, g_ref, cos_ref, sin_ref, o_ref = refs
    else:
        h_ref, w_ref, g_ref, o_ref = refs
    acc = _dot(h_ref[...], w_ref[...])
    g = g_ref[...]
    tm = acc.shape[0]
    if rope:
        cos = cos_ref[...]
        sin = sin_ref[...]
        lane = lax.broadcasted_iota(jnp.int32, (tm, HEAD_DIM), 1)
        first = (lane & (AXIS_DIM // 2)) == 0
    for hh in range(heads):
        y = _rms(acc[:, hh * HEAD_DIM:(hh + 1) * HEAD_DIM], g)
        if rope:
            partner = jnp.where(first,
                                pltpu.roll(y, HEAD_DIM - AXIS_DIM // 2, 1),
                                pltpu.roll(y, AXIS_DIM // 2, 1))
            y = y * cos + partner * sin
        if scale != 1.0:
            y = y * scale
        o_ref[:, hh * HEAD_DIM:(hh + 1) * HEAD_DIM] = y.astype(o_ref.dtype)


def _proj_qk(h, w, g, cos, sin, *, tm, tn, scale, rope, name):
    t, d = h.shape
    n = w.shape[1]
    in_specs = [pl.BlockSpec((tm, d), lambda i, j: (i, 0)),
                pl.BlockSpec((d, tn), lambda i, j: (0, j)),
                pl.BlockSpec((1, HEAD_DIM), lambda i, j: (0, 0))]
    args = [h, w, g]
    if rope:
        seq_tiles = cos.shape[0] // tm
        in_specs += [pl.BlockSpec((tm, HEAD_DIM), lambda i, j: (i % seq_tiles, 0))] * 2
        args += [cos, sin]
    return pl.pallas_call(
        functools.partial(_proj_qk_kernel, heads=tn // HEAD_DIM, scale=scale, rope=rope),
        grid=(t // tm, n // tn),
        in_specs=in_specs,
        out_specs=pl.BlockSpec((tm, tn), lambda i, j: (i, j)),
        out_shape=jax.ShapeDtypeStruct((t, n), BF16),
        compiler_params=_params("arbitrary", "arbitrary"),
        name=name,
    )(*args)


def _dot_hi(a, b):
    return jnp.dot(a, b, precision=HIGHEST, preferred_element_type=F32)


def _filter_hidden_kernel(emb_ref, w1_ref, b1_ref, w2_ref, b2_ref, w3_ref, b3_ref, fr_ref, o_ref):
    fr = fr_ref[...]
    h = jnp.sin(fr * (_dot_hi(emb_ref[0], w1_ref[...]) + b1_ref[...]))
    h = jnp.sin(fr * (_dot_hi(h, w2_ref[...]) + b2_ref[...]))
    o_ref[0] = jnp.sin(fr * (_dot_hi(h, w3_ref[...]) + b3_ref[...]))


def _filter_hidden(emb, w1, b1, w2, b2, w3, b3, fr, *, tr=256):
    sides, L, e = emb.shape
    full = lambda a: pl.BlockSpec(a.shape, lambda s, r: (0,) * a.ndim)
    return pl.pallas_call(
        _filter_hidden_kernel,
        grid=(sides, L // tr),
        in_specs=[pl.BlockSpec((1, tr, e), lambda s, r: (s, r, 0)),
                  full(w1), full(b1), full(w2), full(b2), full(w3), full(b3), full(fr)],
        out_specs=pl.BlockSpec((1, tr, FILTER_HIDDEN), lambda s, r: (s, r, 0)),
        out_shape=jax.ShapeDtypeStruct((sides, L, FILTER_HIDDEN), F32),
        compiler_params=_params("arbitrary", "arbitrary"),
        name="filter_hidden",
    )(emb, w1, b1, w2, b2, w3, b3, fr)


def _filter_kernel(hid_ref, emb_ref, w4f_ref, w4b_ref, dl_ref, k_ref, norm_ref):
    r = pl.program_id(1)
    dl = dl_ref[...]
    kf = _dot_hi(hid_ref[0], w4f_ref[...]) * jnp.exp(-emb_ref[0, :, 0:1] * dl)
    kb = _dot_hi(hid_ref[1], w4b_ref[...]) * jnp.exp(-emb_ref[1, :, 0:1] * dl)
    tr = kb.shape[0]
    row = r * tr + lax.broadcasted_iota(jnp.int32, kb.shape, 0)
    kb = jnp.where(row == 0, 0.0, kb)
    k_ref[0] = kf.astype(k_ref.dtype)
    k_ref[1] = kb.astype(k_ref.dtype)

    @pl.when(r == 0)
    def _():
        norm_ref[...] = jnp.zeros(norm_ref.shape, F32)

    norm_ref[...] += jnp.sum(jnp.abs(kf), axis=0, keepdims=True) + jnp.sum(jnp.abs(kb), axis=0, keepdims=True)


def _hyena_filter(hid, emb, w4, deltas, *, tr=512, tc=256):
    _, L, e = emb.shape
    C = deltas.shape[1]
    return pl.pallas_call(
        _filter_kernel,
        grid=(C // tc, L // tr),
        in_specs=[pl.BlockSpec((2, tr, FILTER_HIDDEN), lambda c, r: (0, r, 0)),
                  pl.BlockSpec((2, tr, e), lambda c, r: (0, r, 0)),
                  pl.BlockSpec((FILTER_HIDDEN, tc), lambda c, r: (0, c)),
                  pl.BlockSpec((FILTER_HIDDEN, tc), lambda c, r: (0, c + C // tc)),
                  pl.BlockSpec((1, tc), lambda c, r: (0, c))],
        out_specs=[pl.BlockSpec((2, tr, tc), lambda c, r: (0, r, c)),
                   pl.BlockSpec((1, tc), lambda c, r: (0, c))],
        out_shape=[jax.ShapeDtypeStruct((2, L, C), BF16), jax.ShapeDtypeStruct((1, C), F32)],
        compiler_params=_params("arbitrary", "arbitrary"),
        name="hyena_filter",
    )(hid, emb, w4, w4, deltas)


def _hyena_pre_kernel(x0m, x0p, x0n, x1m, x1p, x1n, vm, vp, vn, w0, w1, w2, b0, b1, b2, fb,
                      z_ref, zb_ref, x0_ref, *, tiles_per_seq):
    pos = pl.program_id(0) % tiles_per_seq
    first = pos == 0
    last = pos == tiles_per_seq - 1
    tr, tc = x0m.shape
    row = lax.broadcasted_iota(jnp.int32, (tr, tc), 0)

    def conv(m_ref, p_ref, n_ref, w_ref, b_ref):
        u = m_ref[...]
        prev = jnp.where(first, 0.0, p_ref[SUBLANES - 1:SUBLANES, :])
        nxt = jnp.where(last, 0.0, n_ref[0:1, :])
        um = jnp.where(row == 0, prev, pltpu.roll(u, 1, 0))
        up = jnp.where(row == tr - 1, nxt, pltpu.roll(u, tr - 1, 0))
        w = w_ref[...]
        return um * w[0:1] + u * w[1:2] + up * w[2:3] + b_ref[...]

    x0 = conv(x0m, x0p, x0n, w0, b0)
    z = conv(x1m, x1p, x1n, w1, b1) * conv(vm, vp, vn, w2, b2)
    z_ref[...] = z.astype(z_ref.dtype)
    zb_ref[...] = z * fb[...]
    x0_ref[...] = x0


def _hyena_pre(u, conv_w, conv_b, fbias, *, seq, tr=512, tc=512):
    t = u.shape[0]
    C = HYENA_WIDTH
    cb = C // tc
    rb = tr // SUBLANES
    last_rb = t // SUBLANES - 1
    in_specs, args = [], []
    for part in range(3):
        off = part * cb
        in_specs += [pl.BlockSpec((tr, tc), lambda i, j, off=off: (i, j + off)),
                     pl.BlockSpec((SUBLANES, tc), lambda i, j, off=off: (jnp.maximum(i * rb - 1, 0), j + off)),
                     pl.BlockSpec((SUBLANES, tc), lambda i, j, off=off: (jnp.minimum((i + 1) * rb, last_rb), j + off))]
        args += [u, u, u]
    for part in range(3):
        in_specs.append(pl.BlockSpec((3, tc), lambda i, j, off=part * cb: (0, j + off)))
        args.append(conv_w)
    for part in range(3):
        in_specs.append(pl.BlockSpec((1, tc), lambda i, j, off=part * cb: (0, j + off)))
        args.append(conv_b)
    in_specs.append(pl.BlockSpec((1, tc), lambda i, j: (0, j)))
    args.append(fbias)
    out_spec = pl.BlockSpec((tr, tc), lambda i, j: (i, j))
    return pl.pallas_call(
        functools.partial(_hyena_pre_kernel, tiles_per_seq=seq // tr),
        grid=(t // tr, cb),
        in_specs=in_specs,
        out_specs=[out_spec, out_spec, out_spec],
        out_shape=[jax.ShapeDtypeStruct((t, C), BF16),
                   jax.ShapeDtypeStruct((t, C), F32),
                   jax.ShapeDtypeStruct((t, C), F32)],
        compiler_params=_params("arbitrary", "arbitrary"),
        name="hyena_pre",
    )(*args)


def _dft_fwd_kernel(cm_ref, sm_ref, z_ref, zr_ref, zi_ref):
    z = z_ref[0]
    zr_ref[0] = _dot(cm_ref[...], z)
    zi_ref[0] = _dot(sm_ref[...], z)


def _dft_fwd(cm, sm, z, *, tf=512, tc=512):
    nb, L, C = z.shape
    mat = pl.BlockSpec((tf, L), lambda f, b, c: (f, 0))
    out = pl.BlockSpec((1, tf, tc), lambda f, b, c: (b, f, c))
    return pl.pallas_call(
        _dft_fwd_kernel,
        grid=(L // tf, nb, C // tc),
        in_specs=[mat, mat, pl.BlockSpec((1, L, tc), lambda f, b, c: (b, 0, c))],
        out_specs=[out, out],
        out_shape=[jax.ShapeDtypeStruct((nb, L, C), F32)] * 2,
        compiler_params=_params("arbitrary", "arbitrary", "arbitrary"),
        name="dft_filter",
    )(cm, sm, z)


def _dft_mul_kernel(cm_ref, sm_ref, z_ref, kr_ref, ki_ref, norm_ref, yr_ref, yi_ref, *, n_points):
    z = z_ref[0]
    zr = _dot(cm_ref[...], z)
    zi = _dot(sm_ref[...], z)
    tf = zr.shape[0]
    freq = pl.program_id(0) * tf + lax.broadcasted_iota(jnp.int32, zr.shape, 0)
    sign = (1 - 2 * (freq & 1)).astype(F32)
    kr = kr_ref[0] + sign * kr_ref[1]
    ki = ki_ref[0] + sign * ki_ref[1]
    dc = freq == 0
    scale = jnp.where(dc, 1.0 / n_points, 2.0 / n_points) / norm_ref[...]
    rr = zr * kr
    ii = zi * ki
    yr_ref[0] = (jnp.where(dc, rr, rr - ii) * scale).astype(yr_ref.dtype)
    yi_ref[0] = (jnp.where(dc, ii, zr * ki + zi * kr) * scale).astype(yi_ref.dtype)


def _dft_mul(cm, sm, z, kr, ki, norm, *, tf=512, tc=512):
    nb, L, C = z.shape
    mat = pl.BlockSpec((tf, L), lambda f, b, c: (f, 0))
    spec = pl.BlockSpec((2, tf, tc), lambda f, b, c: (0, f, c))
    out = pl.BlockSpec((1, tf, tc), lambda f, b, c: (b, f, c))
    return pl.pallas_call(
        functools.partial(_dft_mul_kernel, n_points=2 * L),
        grid=(L // tf, nb, C // tc),
        in_specs=[mat, mat, pl.BlockSpec((1, L, tc), lambda f, b, c: (b, 0, c)), spec, spec,
                  pl.BlockSpec((1, tc), lambda f, b, c: (0, c))],
        out_specs=[out, out],
        out_shape=[jax.ShapeDtypeStruct((nb, L, C), BF16)] * 2,
        compiler_params=_params("arbitrary", "arbitrary", "arbitrary"),
        name="dft_signal",
    )(cm, sm, z, kr, ki, norm)


def _dft_inv_kernel(cm_ref, si_ref, yr_ref, yi_ref, x0_ref, zb_ref, o_ref):
    y = _dot(cm_ref[...], yr_ref[0]) + _dot(si_ref[...], yi_ref[0])
    o_ref[0] = (x0_ref[0] * (y + zb_ref[0])).astype(o_ref.dtype)


def _dft_inv(cm, si, yr, yi, x0, zb, *, tt=512, tc=512):
    nb, L, C = yr.shape
    mat = pl.BlockSpec((tt, L), lambda t, b, c: (t, 0))
    col = pl.BlockSpec((1, L, tc), lambda t, b, c: (b, 0, c))
    tile = pl.BlockSpec((1, tt, tc), lambda t, b, c: (b, t, c))
    return pl.pallas_call(
        _dft_inv_kernel,
        grid=(L // tt, nb, C // tc),
        in_specs=[mat, mat, col, col, tile, tile],
        out_specs=tile,
        out_shape=jax.ShapeDtypeStruct((nb, L, C), BF16),
        compiler_params=_params("arbitrary", "arbitrary", "arbitrary"),
        name="dft_inverse",
    )(cm, si, yr, yi, x0, zb)


def _attn_kernel(q_ref, k_ref, v_ref, o_ref, *, tq, chunks):
    qs = jnp.concatenate([q_ref[0, :, g * HEAD_DIM:(g + 1) * HEAD_DIM] for g in range(GROUP)], axis=0)
    n = GROUP * tq
    m = jnp.full((1, n), -jnp.inf, F32)
    l = jnp.zeros((1, n), F32)
    acc = jnp.zeros((HEAD_DIM, n), F32)
    for start, size in chunks:
        k = k_ref[0, start:start + size, :]
        v = v_ref[0, start:start + size, :]
        st = lax.dot_general(k, qs, (((1,), (1,)), ((), ())), preferred_element_type=F32)
        m_new = jnp.maximum(m, jnp.max(st, axis=0, keepdims=True))
        alpha = jnp.exp2(m - m_new)
        p = jnp.exp2(st - m_new)
        l = alpha * l + jnp.sum(p, axis=0, keepdims=True)
        pv = lax.dot_general(v, p.astype(BF16), (((0,), (0,)), ((), ())), preferred_element_type=F32)
        acc = alpha * acc + pv
        m = m_new
    out = acc * (1.0 / l)
    for g in range(GROUP):
        o_ref[0, :, g * HEAD_DIM:(g + 1) * HEAD_DIM] = out[:, g * tq:(g + 1) * tq].T.astype(o_ref.dtype)


def _attention(q, k, v, *, tq=256, tk=512):
    B, L, _ = q.shape
    S = k.shape[1]
    gw = GROUP * HEAD_DIM
    kv_spec = pl.BlockSpec((1, S, HEAD_DIM), lambda b, h, i: (b, 0, h))
    q_spec = pl.BlockSpec((1, tq, gw), lambda b, h, i: (b, i, h))
    chunks = tuple((c, min(tk, S - c)) for c in range(0, S, tk))
    return pl.pallas_call(
        functools.partial(_attn_kernel, tq=tq, chunks=chunks),
        grid=(B, N_KV_HEADS, L // tq),
        in_specs=[q_spec, kv_spec, kv_spec],
        out_specs=q_spec,
        out_shape=jax.ShapeDtypeStruct((B, L, Q_W), BF16),
        compiler_params=_params("arbitrary", "arbitrary", "arbitrary"),
        name="attention",
    )(q, k, v)


def _merge_kernel(h_ref, ya_ref, yb_ref, wga_ref, wgb_ref, wba_ref, wbb_ref, o_ref):
    h = h_ref[...]
    ga = jax.nn.sigmoid(_dot(h, wga_ref[...]))
    gb = jax.nn.sigmoid(_dot(h, wgb_ref[...]))
    a = _dot(ya_ref[...], wba_ref[...])
    b = _dot(yb_ref[...], wbb_ref[...])
    o_ref[...] = (ga * a + gb * b).astype(o_ref.dtype)


def _merge(h, ya, yb, wga, wgb, wba, wbb, *, tm=1024, tn=256):
    t, d = h.shape
    row = lambda w: pl.BlockSpec((tm, w), lambda i, j: (i, 0))
    col = lambda kdim: pl.BlockSpec((kdim, tn), lambda i, j: (0, j))
    return pl.pallas_call(
        _merge_kernel,
        grid=(t // tm, d // tn),
        in_specs=[row(d), row(ya.shape[1]), row(yb.shape[1]),
                  col(d), col(d), col(ya.shape[1]), col(yb.shape[1])],
        out_specs=pl.BlockSpec((tm, tn), lambda i, j: (i, j)),
        out_shape=jax.ShapeDtypeStruct((t, d), BF16),
        compiler_params=_params("arbitrary", "arbitrary"),
        name="merge",
    )(h, ya, yb, wga, wgb, wba, wbb)


def _outproj_kernel(m_ref, wo_ref, x_ref, mod_ref, g_ref, x1_ref, h2_ref, *, tiles_per_mod):
    b = _mod_row(pl.program_id(0), tiles_per_mod, 0)
    mix = _dot(m_ref[...], wo_ref[...])
    gt1 = mod_ref[pl.ds(b, 1), 2 * D_MODEL:3 * D_MODEL]
    sh2 = mod_ref[pl.ds(b, 1), 3 * D_MODEL:4 * D_MODEL]
    sc2 = mod_ref[pl.ds(b, 1), 4 * D_MODEL:5 * D_MODEL]
    x1 = x_ref[...] + gt1 * _rms(mix, g_ref[1:2, :])
    x1_ref[...] = x1
    h2_ref[...] = (_rms(x1, g_ref[2:3, :]) * (1.0 + sc2) + sh2).astype(h2_ref.dtype)


def _outproj(merged, wo, x, mod, gains, *, seq, tm=256):
    t, d = x.shape
    row = pl.BlockSpec((tm, d), lambda i: (i, 0))
    return pl.pallas_call(
        functools.partial(_outproj_kernel, tiles_per_mod=seq // tm),
        grid=(t // tm,),
        in_specs=[row, pl.BlockSpec((d, d), lambda i: (0, 0)), row,
                  pl.BlockSpec(mod.shape, lambda i: (0, 0)),
                  pl.BlockSpec(gains.shape, lambda i: (0, 0))],
        out_specs=[row, row],
        out_shape=[jax.ShapeDtypeStruct((t, d), F32), jax.ShapeDtypeStruct((t, d), BF16)],
        compiler_params=_params("arbitrary"),
        name="outproj",
    )(merged, wo, x, mod, gains)


def _mlp_kernel(h2_ref, w1_ref, w2_ref, x1_ref, mod_ref, g_ref, o_ref, acc_ref, *, tiles_per_mod):
    j = pl.program_id(1)

    @pl.when(j == 0)
    def _():
        acc_ref[...] = jnp.zeros(acc_ref.shape, F32)

    hid = jnp.maximum(_dot(h2_ref[...], w1_ref[...]), 0.0)
    acc_ref[...] += _dot((hid * hid).astype(BF16), w2_ref[...])

    @pl.when(j == pl.num_programs(1) - 1)
    def _():
        b = _mod_row(pl.program_id(0), tiles_per_mod, 0)
        gt2 = mod_ref[pl.ds(b, 1), 5 * D_MODEL:6 * D_MODEL]
        o_ref[...] = x1_ref[...] + gt2 * _rms(acc_ref[...], g_ref[3:4, :])


def _mlp(h2, w1, w2, x1, mod, gains, *, seq, tm=512, tf=512):
    t, d = x1.shape
    f = w1.shape[1]
    row = pl.BlockSpec((tm, d), lambda i, j: (i, 0))
    return pl.pallas_call(
        functools.partial(_mlp_kernel, tiles_per_mod=seq // tm),
        grid=(t // tm, f // tf),
        in_specs=[row, pl.BlockSpec((d, tf), lambda i, j: (0, j)), pl.BlockSpec((tf, d), lambda i, j: (j, 0)),
                  row, pl.BlockSpec(mod.shape, lambda i, j: (0, 0)),
                  pl.BlockSpec(gains.shape, lambda i, j: (0, 0))],
        out_specs=row,
        out_shape=jax.ShapeDtypeStruct((t, d), F32),
        scratch_shapes=[pltpu.VMEM((tm, d), F32)],
        compiler_params=_params("arbitrary", "arbitrary"),
        name="mlp",
    )(h2, w1, w2, x1, mod, gains)


def _rope_tables(seq):
    rows = jnp.repeat(jnp.arange(seq // GRID_W), GRID_W)
    cols = jnp.tile(jnp.arange(GRID_W), seq // GRID_W)
    inv = ROPE_THETA ** (-jnp.arange(0, AXIS_DIM, 2, dtype=F32) / AXIS_DIM)
    ar = rows[:, None] * inv
    ac = cols[:, None] * inv
    cos = jnp.concatenate([jnp.cos(ar), jnp.cos(ar), jnp.cos(ac), jnp.cos(ac)], axis=-1)
    sin = jnp.concatenate([-jnp.sin(ar), jnp.sin(ar), -jnp.sin(ac), jnp.sin(ac)], axis=-1)
    return cos, sin


def _filter_embedding(seq):
    t = jnp.linspace(0.0, 1.0, seq, dtype=F32)[:, None]
    wpos = 2.0 * math.pi * jnp.arange(seq, dtype=F32)[:, None] / seq
    bands = jnp.linspace(1e-4, FILTER_BANDS - 1, FILTER_BANDS, dtype=F32)
    emb = jnp.concatenate([t, jnp.cos(bands * wpos), -jnp.sin(bands * wpos)], axis=-1)
    emb = jnp.pad(emb, ((0, 0), (0, EMB_PAD - FILTER_EMB)))
    emb_b = jnp.concatenate([emb[:1], emb[:0:-1]], axis=0)
    return jnp.stack([emb, emb_b])


def _dft_tables(seq):
    idx = jnp.arange(seq, dtype=jnp.int32)
    phase = (idx[:, None] * idx[None, :]) & (2 * seq - 1)
    ang = phase.astype(F32) * (math.pi / seq)
    alt = (1 - 2 * (idx & 1)).astype(F32)
    cm = jnp.cos(ang)
    sn = jnp.sin(ang)
    sf = jnp.where(idx[:, None] == 0, alt[None, :], sn)
    si = jnp.where(idx[None, :] == 0, alt[:, None], sn)
    return cm.astype(BF16), sf.astype(BF16), si.astype(BF16)


def kernel(x, c, ctx, c_ctx, w_ada, b_ada, norm_gains, w_in, conv_w, conv_b, filt_w1, filt_b1, filt_w2, filt_b2, filt_w3, filt_b3, filt_w4, filt_freq, filt_bias, qk_gains, w_branch_a, w_branch_b, w_out, w_ff1, w_ff2):
    B, L, D = x.shape
    T = B * L
    C = HYENA_WIDTH
    lyr = 0
    gains = norm_gains[lyr]
    w_in_b = w_in[lyr].astype(BF16)

    cin = jnp.zeros((MOD_ROWS, D), F32).at[:B].set(c).at[B].set(c_ctx)
    mod = _adaln(cin, w_ada[lyr], b_ada[lyr][None])

    xf = x.reshape(T, D)
    tm_pre = 512
    h = _prenorm(xf, mod, gains[0:1], tm=tm_pre, tiles_per_mod=L // tm_pre, mod_base=0)
    hc = _prenorm(ctx.reshape(B * CTX_LEN, D), mod, gains[0:1], tm=B * CTX_LEN, tiles_per_mod=None, mod_base=B)

    cos, sin = _rope_tables(L)
    qg = qk_gains[lyr, 0][None]
    kg = qk_gains[lyr, 1][None]
    q_scale = HEAD_DIM ** -0.5 * math.log2(math.e)
    u = _matmul(h, w_in_b[:, :Q_OFF], F32, tm=1024, tn=1024, name="proj_u")
    q = _proj_qk(h, w_in_b[:, Q_OFF:K_OFF], qg, cos, sin, tm=1024, tn=512, scale=q_scale, rope=True, name="proj_q")
    k = _proj_qk(h, w_in_b[:, K_OFF:V_OFF], kg, cos, sin, tm=1024, tn=512, scale=1.0, rope=True, name="proj_k")
    v = _matmul(h, w_in_b[:, V_OFF:GA_OFF], BF16, tm=1024, tn=512, name="proj_v")
    kc = _proj_qk(hc, w_in_b[:, K_OFF:V_OFF], kg, None, None, tm=B * CTX_LEN, tn=512, scale=1.0, rope=False,
                  name="proj_kc")
    vc = _matmul(hc, w_in_b[:, V_OFF:GA_OFF], BF16, tm=B * CTX_LEN, tn=512, name="proj_vc")

    emb = _filter_embedding(L)
    w1p = jnp.pad(filt_w1[lyr], ((0, EMB_PAD - FILTER_EMB), (0, 0)))
    deltas = jnp.abs(jnp.linspace(MIN_DECAY, MAX_DECAY, C, dtype=F32))[None]
    hid = _filter_hidden(emb, w1p, filt_b1[lyr][None], filt_w2[lyr], filt_b2[lyr][None],
                         filt_w3[lyr], filt_b3[lyr][None], filt_freq[lyr][None])
    filt, norm = _hyena_filter(hid, emb, filt_w4[lyr], deltas)
    cm, sf, si = _dft_tables(L)
    kr, ki = _dft_fwd(cm, sf, filt)
    z, zb, x0 = _hyena_pre(u, conv_w[lyr], conv_b[lyr][None], filt_bias[lyr][None], seq=L)
    yr, yi = _dft_mul(cm, sf, z.reshape(B, L, C), kr, ki, norm)
    ya = _dft_inv(cm, si, yr, yi, x0.reshape(B, L, C), zb.reshape(B, L, C)).reshape(T, C)

    k_all = jnp.concatenate([k.reshape(B, L, KV_W), kc.reshape(B, CTX_LEN, KV_W)], axis=1)
    v_all = jnp.concatenate([v.reshape(B, L, KV_W), vc.reshape(B, CTX_LEN, KV_W)], axis=1)
    yb = _attention(q.reshape(B, L, Q_W), k_all, v_all).reshape(T, Q_W)

    merged = _merge(h, ya, yb, w_in_b[:, GA_OFF:GB_OFF], w_in_b[:, GB_OFF:],
                    w_branch_a[lyr].astype(BF16), w_branch_b[lyr].astype(BF16))
    x1, h2 = _outproj(merged, w_out[lyr].astype(BF16), xf, mod, gains, seq=L)
    out = _mlp(h2, w_ff1[lyr].astype(BF16), w_ff2[lyr].astype(BF16), x1, mod, gains, seq=L)
    return out.reshape(B, L, D)
```

```python
import functools
import math

import jax
import jax.numpy as jnp
from jax import lax
from jax.experimental import pallas as pl
from jax.experimental.pallas import tpu as pltpu

F32 = jnp.float32
BF16 = jnp.bfloat16

D_MODEL = 2048
CTX_LEN = 256
GRID_W = 64
N_HEADS = 16
HEAD_DIM = 128
N_KV_HEADS = 4
GROUP = N_HEADS // N_KV_HEADS
AXIS_DIM = HEAD_DIM // 2
ROPE_THETA = 10000.0
HYENA_WIDTH = D_MODEL // 2
FILTER_HIDDEN = 64
FILTER_EMB = 17
FILTER_BANDS = (FILTER_EMB - 1) // 2
DECAY_TARGET = 1e-2
MIN_DECAY = math.log(DECAY_TARGET) / 1.5
MAX_DECAY = math.log(DECAY_TARGET) / 0.3
D_FF = 4 * D_MODEL
EPS = 1e-6
Q_W = N_HEADS * HEAD_DIM
KV_W = N_KV_HEADS * HEAD_DIM
Q_OFF = 3 * HYENA_WIDTH
K_OFF = Q_OFF + Q_W
V_OFF = K_OFF + KV_W
GA_OFF = V_OFF + KV_W
GB_OFF = GA_OFF + D_MODEL

LANES = 128
SUBLANES = 8
VMEM_LIMIT = 56 * 1024 * 1024
EMB_PAD = 128
MOD_ROWS = 8
HIGHEST = lax.Precision.HIGHEST
QK_SUB_ROWS = 256
FFT_BLOCK = 512


def _params(*sem):
    return pltpu.CompilerParams(dimension_semantics=sem, vmem_limit_bytes=VMEM_LIMIT)


def _dot(a, b):
    return jnp.dot(a, b, preferred_element_type=F32)


def _rms(x, g):
    return x * lax.rsqrt(jnp.mean(x * x, axis=-1, keepdims=True) + EPS) * g


def _adaln_kernel(c_ref, w_ref, b_ref, o_ref):
    c = c_ref[...]
    s = c * jax.nn.sigmoid(c)
    o_ref[...] = _dot(s.astype(BF16), w_ref[...].astype(BF16)) + b_ref[...]


def _adaln(cin, w, b, tn=1024):
    rows, d = cin.shape
    n = w.shape[1]
    return pl.pallas_call(
        _adaln_kernel,
        grid=(n // tn,),
        in_specs=[pl.BlockSpec((rows, d), lambda j: (0, 0)),
                  pl.BlockSpec((d, tn), lambda j: (0, j)),
                  pl.BlockSpec((1, tn), lambda j: (0, j))],
        out_specs=pl.BlockSpec((rows, tn), lambda j: (0, j)),
        out_shape=jax.ShapeDtypeStruct((rows, n), F32),
        compiler_params=_params("arbitrary"),
        name="adaln",
    )(cin, w, b)


def _mod_row(i, tiles_per_mod, mod_base):
    if tiles_per_mod is None:
        return mod_base
    return mod_base + i // tiles_per_mod


def _prenorm_kernel(x_ref, mod_ref, g_ref, o_ref, *, tiles_per_mod, mod_base):
    b = _mod_row(pl.program_id(0), tiles_per_mod, mod_base)
    y = _rms(x_ref[...], g_ref[...])
    sh = mod_ref[pl.ds(b, 1), 0:D_MODEL]
    sc = mod_ref[pl.ds(b, 1), D_MODEL:2 * D_MODEL]
    o_ref[...] = (y * (1.0 + sc) + sh).astype(o_ref.dtype)


def _prenorm(x, mod, g, *, tm, tiles_per_mod, mod_base):
    t, d = x.shape
    return pl.pallas_call(
        functools.partial(_prenorm_kernel, tiles_per_mod=tiles_per_mod, mod_base=mod_base),
        grid=(t // tm,),
        in_specs=[pl.BlockSpec((tm, d), lambda i: (i, 0)),
                  pl.BlockSpec(mod.shape, lambda i: (0, 0)),
                  pl.BlockSpec((1, d), lambda i: (0, 0))],
        out_specs=pl.BlockSpec((tm, d), lambda i: (i, 0)),
        out_shape=jax.ShapeDtypeStruct((t, d), BF16),
        compiler_params=_params("arbitrary"),
        name="prenorm",
    )(x, mod, g)


def _matmul_kernel(h_ref, w_ref, o_ref):
    o_ref[...] = _dot(h_ref[...], w_ref[...]).astype(o_ref.dtype)


def _matmul(h, w, out_dtype, *, tm, tn, name):
    t, d = h.shape
    n = w.shape[1]
    return pl.pallas_call(
        _matmul_kernel,
        grid=(t // tm, n // tn),
        in_specs=[pl.BlockSpec((tm, d), lambda i, j: (i, 0)),
                  pl.BlockSpec((d, tn), lambda i, j: (0, j))],
        out_specs=pl.BlockSpec((tm, tn), lambda i, j: (i, j)),
        out_shape=jax.ShapeDtypeStruct((t, n), out_dtype),
        compiler_params=_params("arbitrary", "arbitrary"),
        name=name,
    )(h, w)


def _proj_qk_kernel(*refs, heads, scale, rope):
    if rope:
        h_ref, w_ref, g_ref, cos_ref, sin_ref, o_ref = refs
    else:
        h_ref, w_ref, g_ref, o_ref = refs
    g = g_ref[...]
    tm = h_ref.shape[0]
    sub = min(tm, QK_SUB_ROWS)
    if rope:
        lane = lax.broadcasted_iota(jnp.int32, (sub, HEAD_DIM), 1)
        first = (lane & (AXIS_DIM // 2)) == 0
    for r in range(0, tm, sub):
        acc = _dot(h_ref[r:r + sub, :], w_ref[...])
        if rope:
            cos = cos_ref[r:r + sub, :]
            sin = sin_ref[r:r + sub, :]
        for hh in range(heads):
            y = _rms(acc[:, hh * HEAD_DIM:(hh + 1) * HEAD_DIM], g)
            if rope:
                partner = jnp.where(first,
                                    pltpu.roll(y, HEAD_DIM - AXIS_DIM // 2, 1),
                                    pltpu.roll(y, AXIS_DIM // 2, 1))
                y = y * cos + partner * sin
            if scale != 1.0:
                y = y * scale
            o_ref[r:r + sub, hh * HEAD_DIM:(hh + 1) * HEAD_DIM] = y.astype(o_ref.dtype)


def _proj_qk(h, w, g, cos, sin, *, tm, tn, scale, rope, name):
    t, d = h.shape
    n = w.shape[1]
    in_specs = [pl.BlockSpec((tm, d), lambda i, j: (i, 0)),
                pl.BlockSpec((d, tn), lambda i, j: (0, j)),
                pl.BlockSpec((1, HEAD_DIM), lambda i, j: (0, 0))]
    args = [h, w, g]
    if rope:
        seq_tiles = cos.shape[0] // tm
        in_specs += [pl.BlockSpec((tm, HEAD_DIM), lambda i, j: (i % seq_tiles, 0))] * 2
        args += [cos, sin]
    return pl.pallas_call(
        functools.partial(_proj_qk_kernel, heads=tn // HEAD_DIM, scale=scale, rope=rope),
        grid=(t // tm, n // tn),
        in_specs=in_specs,
        out_specs=pl.BlockSpec((tm, tn), lambda i, j: (i, j)),
        out_shape=jax.ShapeDtypeStruct((t, n), BF16),
        compiler_params=_params("arbitrary", "arbitrary"),
        name=name,
    )(*args)


def _dot_hi(a, b):
    return jnp.dot(a, b, precision=HIGHEST, preferred_element_type=F32)


def _filter_hidden_kernel(emb_ref, w1_ref, b1_ref, w2_ref, b2_ref, w3_ref, b3_ref, fr_ref, o_ref):
    fr = fr_ref[...]
    h = jnp.sin(fr * (_dot_hi(emb_ref[0], w1_ref[...]) + b1_ref[...]))
    h = jnp.sin(fr * (_dot_hi(h, w2_ref[...]) + b2_ref[...]))
    o_ref[0] = jnp.sin(fr * (_dot_hi(h, w3_ref[...]) + b3_ref[...]))


def _filter_hidden(emb, w1, b1, w2, b2, w3, b3, fr, *, tr=256):
    sides, L, e = emb.shape
    full = lambda a: pl.BlockSpec(a.shape, lambda s, r: (0,) * a.ndim)
    return pl.pallas_call(
        _filter_hidden_kernel,
        grid=(sides, L // tr),
        in_specs=[pl.BlockSpec((1, tr, e), lambda s, r: (s, r, 0)),
                  full(w1), full(b1), full(w2), full(b2), full(w3), full(b3), full(fr)],
        out_specs=pl.BlockSpec((1, tr, FILTER_HIDDEN), lambda s, r: (s, r, 0)),
        out_shape=jax.ShapeDtypeStruct((sides, L, FILTER_HIDDEN), F32),
        compiler_params=_params("arbitrary", "arbitrary"),
        name="filter_hidden",
    )(emb, w1, b1, w2, b2, w3, b3, fr)


def _filter_kernel(hid_ref, emb_ref, w4f_ref, w4b_ref, dl_ref, k_ref, norm_ref):
    r = pl.program_id(1)
    dl = dl_ref[...]
    kf = _dot_hi(hid_ref[0], w4f_ref[...]) * jnp.exp(-emb_ref[0, :, 0:1] * dl)
    kb = _dot_hi(hid_ref[1], w4b_ref[...]) * jnp.exp(-emb_ref[1, :, 0:1] * dl)
    tr = kb.shape[0]
    row = r * tr + lax.broadcasted_iota(jnp.int32, kb.shape, 0)
    kb = jnp.where(row == 0, 0.0, kb)
    k_ref[0] = kb.astype(k_ref.dtype)
    k_ref[1] = kf.astype(k_ref.dtype)

    @pl.when(r == 0)
    def _():
        norm_ref[...] = jnp.zeros(norm_ref.shape, F32)

    norm_ref[...] += jnp.sum(jnp.abs(kf), axis=0, keepdims=True) + jnp.sum(jnp.abs(kb), axis=0, keepdims=True)


def _hyena_filter(hid, emb, w4, deltas, *, tr=512, tc=256):
    _, L, e = emb.shape
    C = deltas.shape[1]
    return pl.pallas_call(
        _filter_kernel,
        grid=(C // tc, L // tr),
        in_specs=[pl.BlockSpec((2, tr, FILTER_HIDDEN), lambda c, r: (0, r, 0)),
                  pl.BlockSpec((2, tr, e), lambda c, r: (0, r, 0)),
                  pl.BlockSpec((FILTER_HIDDEN, tc), lambda c, r: (0, c)),
                  pl.BlockSpec((FILTER_HIDDEN, tc), lambda c, r: (0, c + C // tc)),
                  pl.BlockSpec((1, tc), lambda c, r: (0, c))],
        out_specs=[pl.BlockSpec((2, tr, tc), lambda c, r: (0, r, c)),
                   pl.BlockSpec((1, tc), lambda c, r: (0, c))],
        out_shape=[jax.ShapeDtypeStruct((2, L, C), BF16), jax.ShapeDtypeStruct((1, C), F32)],
        compiler_params=_params("arbitrary", "arbitrary"),
        name="hyena_filter",
    )(hid, emb, w4, w4, deltas)


def _hyena_pre_kernel(x0m, x0p, x0n, x1m, x1p, x1n, vm, vp, vn, w0, w1, w2, b0, b1, b2, fb,
                      z_ref, zb_ref, x0_ref, *, tiles_per_seq):
    pos = pl.program_id(0) % tiles_per_seq
    first = pos == 0
    last = pos == tiles_per_seq - 1
    tr, tc = x0m.shape
    row = lax.broadcasted_iota(jnp.int32, (tr, tc), 0)

    def conv(m_ref, p_ref, n_ref, w_ref, b_ref):
        u = m_ref[...]
        prev = jnp.where(first, 0.0, p_ref[SUBLANES - 1:SUBLANES, :])
        nxt = jnp.where(last, 0.0, n_ref[0:1, :])
        um = jnp.where(row == 0, prev, pltpu.roll(u, 1, 0))
        up = jnp.where(row == tr - 1, nxt, pltpu.roll(u, tr - 1, 0))
        w = w_ref[...]
        return um * w[0:1] + u * w[1:2] + up * w[2:3] + b_ref[...]

    x0 = conv(x0m, x0p, x0n, w0, b0)
    z = conv(x1m, x1p, x1n, w1, b1) * conv(vm, vp, vn, w2, b2)
    z_ref[...] = z.astype(z_ref.dtype)
    zb_ref[...] = z * fb[...]
    x0_ref[...] = x0


def _hyena_pre(u, conv_w, conv_b, fbias, *, seq, tr=512, tc=512):
    t = u.shape[0]
    C = HYENA_WIDTH
    cb = C // tc
    rb = tr // SUBLANES
    last_rb = t // SUBLANES - 1
    in_specs, args = [], []
    for part in range(3):
        off = part * cb
        in_specs += [pl.BlockSpec((tr, tc), lambda i, j, off=off: (i, j + off)),
                     pl.BlockSpec((SUBLANES, tc), lambda i, j, off=off: (jnp.maximum(i * rb - 1, 0), j + off)),
                     pl.BlockSpec((SUBLANES, tc), lambda i, j, off=off: (jnp.minimum((i + 1) * rb, last_rb), j + off))]
        args += [u, u, u]
    for part in range(3):
        in_specs.append(pl.BlockSpec((3, tc), lambda i, j, off=part * cb: (0, j + off)))
        args.append(conv_w)
    for part in range(3):
        in_specs.append(pl.BlockSpec((1, tc), lambda i, j, off=part * cb: (0, j + off)))
        args.append(conv_b)
    in_specs.append(pl.BlockSpec((1, tc), lambda i, j: (0, j)))
    args.append(fbias)
    out_spec = pl.BlockSpec((tr, tc), lambda i, j: (i, j))
    return pl.pallas_call(
        functools.partial(_hyena_pre_kernel, tiles_per_seq=seq // tr),
        grid=(t // tr, cb),
        in_specs=in_specs,
        out_specs=[out_spec, out_spec, out_spec],
        out_shape=[jax.ShapeDtypeStruct((t, C), BF16),
                   jax.ShapeDtypeStruct((t, C), F32),
                   jax.ShapeDtypeStruct((t, C), F32)],
        compiler_params=_params("arbitrary", "arbitrary"),
        name="hyena_pre",
    )(*args)


def _dft_fwd_kernel(cm_ref, sm_ref, z_ref, zr_ref, zi_ref):
    z = z_ref[0]
    zr_ref[0] = _dot(cm_ref[...], z)
    zi_ref[0] = _dot(sm_ref[...], z)


def _dft_fwd(cm, sm, z, *, tc=1024):
    nb, P, C = z.shape
    mat = pl.BlockSpec((P, P), lambda b, c: (0, 0))
    blk = pl.BlockSpec((1, P, tc), lambda b, c: (b, 0, c))
    return pl.pallas_call(
        _dft_fwd_kernel,
        grid=(nb, C // tc),
        in_specs=[mat, mat, blk],
        out_specs=[blk, blk],
        out_shape=[jax.ShapeDtypeStruct((nb, P, C), F32)] * 2,
        compiler_params=_params("arbitrary", "arbitrary"),
        name="dft_signal",
    )(cm, sm, z)


def _filter_spectrum_kernel(cm_ref, sf_ref, cms_ref, sfs_ref, hi_ref, lo_ref, norm_ref, gr_ref, gi_ref):
    hi = hi_ref[0]
    lo = lo_ref[0]
    gr = _dot(cm_ref[...], hi) + _dot(cms_ref[...], lo)
    gi = _dot(sf_ref[...], hi) + _dot(sfs_ref[...], lo)
    p = gr.shape[0]
    dc = lax.broadcasted_iota(jnp.int32, gr.shape, 0) == 0
    scale = jnp.where(dc, 0.5 / p, 1.0 / p) / norm_ref[...]
    gr_ref[0] = gr * scale
    gi_ref[0] = gi * scale


def _filter_spectrum(cm, sf, cms, sfs, taps, norm, *, tc=1024):
    nseg, P, C = taps.shape
    mat = pl.BlockSpec((P, P), lambda d, c: (0, 0))
    out = pl.BlockSpec((1, P, tc), lambda d, c: (d, 0, c))
    return pl.pallas_call(
        _filter_spectrum_kernel,
        grid=(nseg - 1, C // tc),
        in_specs=[mat, mat, mat, mat,
                  pl.BlockSpec((1, P, tc), lambda d, c: (d + 1, 0, c)),
                  pl.BlockSpec((1, P, tc), lambda d, c: (d, 0, c)),
                  pl.BlockSpec((1, tc), lambda d, c: (0, c))],
        out_specs=[out, out],
        out_shape=[jax.ShapeDtypeStruct((nseg - 1, P, C), F32)] * 2,
        compiler_params=_params("arbitrary", "arbitrary"),
        name="dft_filter",
    )(cm, sf, cms, sfs, taps, taps, norm)


def _conv_kernel(cm_ref, si_ref, zr_ref, zi_ref, gr_ref, gi_ref, x0_ref, zb_ref, o_ref, yr_sc, yi_sc, *, rows):
    nblk, P, tc = zr_ref.shape
    dc = lax.broadcasted_iota(jnp.int32, (rows, tc), 0) == 0

    def out_block(i, carry):
        for r in range(0, P, rows):
            rr = ii = ri = ir = None
            for j in range(nblk):
                lag = i - j + (nblk - 1)
                gr = gr_ref[lag, r:r + rows, :]
                gi = gi_ref[lag, r:r + rows, :]
                zr = zr_ref[j, r:r + rows, :]
                zi = zi_ref[j, r:r + rows, :]
                if j == 0:
                    rr, ii, ri, ir = gr * zr, gi * zi, gr * zi, gi * zr
                else:
                    rr, ii, ri, ir = rr + gr * zr, ii + gi * zi, ri + gr * zi, ir + gi * zr
            if r == 0:
                yr = jnp.where(dc, rr, rr - ii)
                yi = jnp.where(dc, ii, ri + ir)
            else:
                yr = rr - ii
                yi = ri + ir
            yr_sc[r:r + rows, :] = yr.astype(yr_sc.dtype)
            yi_sc[r:r + rows, :] = yi.astype(yi_sc.dtype)
        y = _dot(cm_ref[...], yr_sc[...]) + _dot(si_ref[...], yi_sc[...])
        t0 = pl.multiple_of(i * P, P)
        o_ref[pl.ds(t0, P), :] = (x0_ref[pl.ds(t0, P), :] * (y + zb_ref[pl.ds(t0, P), :])).astype(o_ref.dtype)
        return carry

    lax.fori_loop(0, nblk, out_block, 0)


def _conv(cm, si, zr, zi, gr, gi, x0, zb, *, seq, tc=128, rows=64):
    P = cm.shape[0]
    nblk = seq // P
    nlag = gr.shape[0]
    t, C = x0.shape
    mat = pl.BlockSpec((P, P), lambda c, b: (0, 0))
    zspec = pl.BlockSpec((nblk, P, tc), lambda c, b: (b, 0, c))
    gspec = pl.BlockSpec((nlag, P, tc), lambda c, b: (0, 0, c))
    tile = pl.BlockSpec((seq, tc), lambda c, b: (b, c))
    return pl.pallas_call(
        functools.partial(_conv_kernel, rows=rows),
        grid=(C // tc, t // seq),
        in_specs=[mat, mat, zspec, zspec, gspec, gspec, tile, tile],
        out_specs=tile,
        out_shape=jax.ShapeDtypeStruct((t, C), BF16),
        scratch_shapes=[pltpu.VMEM((P, tc), BF16), pltpu.VMEM((P, tc), BF16)],
        compiler_params=_params("arbitrary", "arbitrary"),
        name="hyena_conv",
    )(cm, si, zr, zi, gr, gi, x0, zb)


def _attn_kernel(q_ref, k_ref, v_ref, o_ref, *, tq, chunks):
    qs = jnp.concatenate([q_ref[0, :, g * HEAD_DIM:(g + 1) * HEAD_DIM] for g in range(GROUP)], axis=0)
    n = GROUP * tq
    m = jnp.full((1, n), -jnp.inf, F32)
    l = jnp.zeros((1, n), F32)
    acc = jnp.zeros((HEAD_DIM, n), F32)
    for start, size in chunks:
        k = k_ref[0, start:start + size, :]
        v = v_ref[0, start:start + size, :]
        st = lax.dot_general(k, qs, (((1,), (1,)), ((), ())), preferred_element_type=F32)
        m_new = jnp.maximum(m, jnp.max(st, axis=0, keepdims=True))
        alpha = jnp.exp2(m - m_new)
        p = jnp.exp2(st - m_new)
        l = alpha * l + jnp.sum(p, axis=0, keepdims=True)
        pv = lax.dot_general(v, p.astype(BF16), (((0,), (0,)), ((), ())), preferred_element_type=F32)
        acc = alpha * acc + pv
        m = m_new
    out = acc * (1.0 / l)
    for g in range(GROUP):
        o_ref[0, :, g * HEAD_DIM:(g + 1) * HEAD_DIM] = out[:, g * tq:(g + 1) * tq].T.astype(o_ref.dtype)


def _attention(q, k, v, *, tq=256, tk=512):
    B, L, _ = q.shape
    S = k.shape[1]
    gw = GROUP * HEAD_DIM
    kv_spec = pl.BlockSpec((1, S, HEAD_DIM), lambda b, h, i: (b, 0, h))
    q_spec = pl.BlockSpec((1, tq, gw), lambda b, h, i: (b, i, h))
    chunks = tuple((c, min(tk, S - c)) for c in range(0, S, tk))
    return pl.pallas_call(
        functools.partial(_attn_kernel, tq=tq, chunks=chunks),
        grid=(B, N_KV_HEADS, L // tq),
        in_specs=[q_spec, kv_spec, kv_spec],
        out_specs=q_spec,
        out_shape=jax.ShapeDtypeStruct((B, L, Q_W), BF16),
        compiler_params=_params("arbitrary", "arbitrary", "arbitrary"),
        name="attention",
    )(q, k, v)


def _merge_kernel(h_ref, ya_ref, yb_ref, wga_ref, wgb_ref, wba_ref, wbb_ref, o_ref):
    h = h_ref[...]
    ga = jax.nn.sigmoid(_dot(h, wga_ref[...]))
    gb = jax.nn.sigmoid(_dot(h, wgb_ref[...]))
    a = _dot(ya_ref[...], wba_ref[...])
    b = _dot(yb_ref[...], wbb_ref[...])
    o_ref[...] = (ga * a + gb * b).astype(o_ref.dtype)


def _merge(h, ya, yb, wga, wgb, wba, wbb, *, tm=1024, tn=256):
    t, d = h.shape
    row = lambda w: pl.BlockSpec((tm, w), lambda i, j: (i, 0))
    col = lambda kdim: pl.BlockSpec((kdim, tn), lambda i, j: (0, j))
    return pl.pallas_call(
        _merge_kernel,
        grid=(t // tm, d // tn),
        in_specs=[row(d), row(ya.shape[1]), row(yb.shape[1]),
                  col(d), col(d), col(ya.shape[1]), col(yb.shape[1])],
        out_specs=pl.BlockSpec((tm, tn), lambda i, j: (i, j)),
        out_shape=jax.ShapeDtypeStruct((t, d), BF16),
        compiler_params=_params("arbitrary", "arbitrary"),
        name="merge",
    )(h, ya, yb, wga, wgb, wba, wbb)


def _outproj_kernel(m_ref, wo_ref, x_ref, mod_ref, g_ref, x1_ref, h2_ref, *, tiles_per_mod):
    b = _mod_row(pl.program_id(0), tiles_per_mod, 0)
    mix = _dot(m_ref[...], wo_ref[...])
    gt1 = mod_ref[pl.ds(b, 1), 2 * D_MODEL:3 * D_MODEL]
    sh2 = mod_ref[pl.ds(b, 1), 3 * D_MODEL:4 * D_MODEL]
    sc2 = mod_ref[pl.ds(b, 1), 4 * D_MODEL:5 * D_MODEL]
    x1 = x_ref[...] + gt1 * _rms(mix, g_ref[1:2, :])
    x1_ref[...] = x1
    h2_ref[...] = (_rms(x1, g_ref[2:3, :]) * (1.0 + sc2) + sh2).astype(h2_ref.dtype)


def _outproj(merged, wo, x, mod, gains, *, seq, tm=256):
    t, d = x.shape
    row = pl.BlockSpec((tm, d), lambda i: (i, 0))
    return pl.pallas_call(
        functools.partial(_outproj_kernel, tiles_per_mod=seq // tm),
        grid=(t // tm,),
        in_specs=[row, pl.BlockSpec((d, d), lambda i: (0, 0)), row,
                  pl.BlockSpec(mod.shape, lambda i: (0, 0)),
                  pl.BlockSpec(gains.shape, lambda i: (0, 0))],
        out_specs=[row, row],
        out_shape=[jax.ShapeDtypeStruct((t, d), F32), jax.ShapeDtypeStruct((t, d), BF16)],
        compiler_params=_params("arbitrary"),
        name="outproj",
    )(merged, wo, x, mod, gains)


def _mlp_kernel(h2_ref, w1_ref, w2_ref, x1_ref, mod_ref, g_ref, o_ref, acc_ref, *, tiles_per_mod):
    j = pl.program_id(1)

    @pl.when(j == 0)
    def _():
        acc_ref[...] = jnp.zeros(acc_ref.shape, F32)

    hid = jnp.maximum(_dot(h2_ref[...], w1_ref[...]), 0.0)
    acc_ref[...] += _dot((hid * hid).astype(BF16), w2_ref[...])

    @pl.when(j == pl.num_programs(1) - 1)
    def _():
        b = _mod_row(pl.program_id(0), tiles_per_mod, 0)
        gt2 = mod_ref[pl.ds(b, 1), 5 * D_MODEL:6 * D_MODEL]
        o_ref[...] = x1_ref[...] + gt2 * _rms(acc_ref[...], g_ref[3:4, :])


def _mlp(h2, w1, w2, x1, mod, gains, *, seq, tm=512, tf=512):
    t, d = x1.shape
    f = w1.shape[1]
    row = pl.BlockSpec((tm, d), lambda i, j: (i, 0))
    return pl.pallas_call(
        functools.partial(_mlp_kernel, tiles_per_mod=seq // tm),
        grid=(t // tm, f // tf),
        in_specs=[row, pl.BlockSpec((d, tf), lambda i, j: (0, j)), pl.BlockSpec((tf, d), lambda i, j: (j, 0)),
                  row, pl.BlockSpec(mod.shape, lambda i, j: (0, 0)),
                  pl.BlockSpec(gains.shape, lambda i, j: (0, 0))],
        out_specs=row,
        out_shape=jax.ShapeDtypeStruct((t, d), F32),
        scratch_shapes=[pltpu.VMEM((tm, d), F32)],
        compiler_params=_params("arbitrary", "arbitrary"),
        name="mlp",
    )(h2, w1, w2, x1, mod, gains)


def _rope_tables(seq):
    rows = jnp.repeat(jnp.arange(seq // GRID_W), GRID_W)
    cols = jnp.tile(jnp.arange(GRID_W), seq // GRID_W)
    inv = ROPE_THETA ** (-jnp.arange(0, AXIS_DIM, 2, dtype=F32) / AXIS_DIM)
    ar = rows[:, None] * inv
    ac = cols[:, None] * inv
    cos = jnp.concatenate([jnp.cos(ar), jnp.cos(ar), jnp.cos(ac), jnp.cos(ac)], axis=-1)
    sin = jnp.concatenate([-jnp.sin(ar), jnp.sin(ar), -jnp.sin(ac), jnp.sin(ac)], axis=-1)
    return cos, sin


def _filter_embedding(seq):
    t = jnp.linspace(0.0, 1.0, seq, dtype=F32)[:, None]
    wpos = 2.0 * math.pi * jnp.arange(seq, dtype=F32)[:, None] / seq
    bands = jnp.linspace(1e-4, FILTER_BANDS - 1, FILTER_BANDS, dtype=F32)
    emb = jnp.concatenate([t, jnp.cos(bands * wpos), -jnp.sin(bands * wpos)], axis=-1)
    emb = jnp.pad(emb, ((0, 0), (0, EMB_PAD - FILTER_EMB)))
    emb_b = jnp.concatenate([emb[:1], emb[:0:-1]], axis=0)
    return jnp.stack([emb, emb_b])


def _dft_tables(seq):
    idx = jnp.arange(seq, dtype=jnp.int32)
    phase = (idx[:, None] * idx[None, :]) & (2 * seq - 1)
    ang = phase.astype(F32) * (math.pi / seq)
    alt = (1 - 2 * (idx & 1)).astype(F32)
    cm = jnp.cos(ang)
    sn = jnp.sin(ang)
    sf = jnp.where(idx[:, None] == 0, alt[None, :], sn)
    si = jnp.where(idx[None, :] == 0, alt[:, None], sn)
    cms = alt[:, None] * cm
    sfs = alt[:, None] * sf
    return tuple(t.astype(BF16) for t in (cm, sf, si, cms, sfs))


def kernel(x, c, ctx, c_ctx, w_ada, b_ada, norm_gains, w_in, conv_w, conv_b, filt_w1, filt_b1, filt_w2, filt_b2, filt_w3, filt_b3, filt_w4, filt_freq, filt_bias, qk_gains, w_branch_a, w_branch_b, w_out, w_ff1, w_ff2):
    B, L, D = x.shape
    T = B * L
    C = HYENA_WIDTH
    lyr = 0
    gains = norm_gains[lyr]
    w_in_b = w_in[lyr].astype(BF16)

    cin = jnp.zeros((MOD_ROWS, D), F32).at[:B].set(c).at[B].set(c_ctx)
    mod = _adaln(cin, w_ada[lyr], b_ada[lyr][None])

    xf = x.reshape(T, D)
    tm_pre = 512
    h = _prenorm(xf, mod, gains[0:1], tm=tm_pre, tiles_per_mod=L // tm_pre, mod_base=0)
    hc = _prenorm(ctx.reshape(B * CTX_LEN, D), mod, gains[0:1], tm=B * CTX_LEN, tiles_per_mod=None, mod_base=B)

    cos, sin = _rope_tables(L)
    qg = qk_gains[lyr, 0][None]
    kg = qk_gains[lyr, 1][None]
    q_scale = HEAD_DIM ** -0.5 * math.log2(math.e)
    u = _matmul(h, w_in_b[:, :Q_OFF], F32, tm=1024, tn=1024, name="proj_u")
    q = _proj_qk(h, w_in_b[:, Q_OFF:K_OFF], qg, cos, sin, tm=1024, tn=512, scale=q_scale, rope=True, name="proj_q")
    k = _proj_qk(h, w_in_b[:, K_OFF:V_OFF], kg, cos, sin, tm=1024, tn=512, scale=1.0, rope=True, name="proj_k")
    v = _matmul(h, w_in_b[:, V_OFF:GA_OFF], BF16, tm=1024, tn=512, name="proj_v")
    kc = _proj_qk(hc, w_in_b[:, K_OFF:V_OFF], kg, None, None, tm=B * CTX_LEN, tn=512, scale=1.0, rope=False,
                  name="proj_kc")
    vc = _matmul(hc, w_in_b[:, V_OFF:GA_OFF], BF16, tm=B * CTX_LEN, tn=512, name="proj_vc")

    emb = _filter_embedding(L)
    w1p = jnp.pad(filt_w1[lyr], ((0, EMB_PAD - FILTER_EMB), (0, 0)))
    deltas = jnp.abs(jnp.linspace(MIN_DECAY, MAX_DECAY, C, dtype=F32))[None]
    hid = _filter_hidden(emb, w1p, filt_b1[lyr][None], filt_w2[lyr], filt_b2[lyr][None],
                         filt_w3[lyr], filt_b3[lyr][None], filt_freq[lyr][None])
    filt, norm = _hyena_filter(hid, emb, filt_w4[lyr], deltas)
    P = FFT_BLOCK
    cm, sf, si, cms, sfs = _dft_tables(P)
    gr, gi = _filter_spectrum(cm, sf, cms, sfs, filt.reshape(2 * L // P, P, C), norm)
    z, zb, x0 = _hyena_pre(u, conv_w[lyr], conv_b[lyr][None], filt_bias[lyr][None], seq=L)
    zr, zi = _dft_fwd(cm, sf, z.reshape(T // P, P, C))
    ya = _conv(cm, si, zr, zi, gr, gi, x0, zb, seq=L)

    k_all = jnp.concatenate([k.reshape(B, L, KV_W), kc.reshape(B, CTX_LEN, KV_W)], axis=1)
    v_all = jnp.concatenate([v.reshape(B, L, KV_W), vc.reshape(B, CTX_LEN, KV_W)], axis=1)
    yb = _attention(q.reshape(B, L, Q_W), k_all, v_all).reshape(T, Q_W)

    merged = _merge(h, ya, yb, w_in_b[:, GA_OFF:GB_OFF], w_in_b[:, GB_OFF:],
                    w_branch_a[lyr].astype(BF16), w_branch_b[lyr].astype(BF16))
    x1, h2 = _outproj(merged, w_out[lyr].astype(BF16), xf, mod, gains, seq=L)
    out = _mlp(h2, w_ff1[lyr].astype(BF16), w_ff2[lyr].astype(BF16), x1, mod, gains, seq=L)
    return out.reshape(B, L, D)
```

```python
import functools
import math

import jax
import jax.numpy as jnp
from jax import lax
from jax.experimental import pallas as pl
from jax.experimental.pallas import tpu as pltpu

F32 = jnp.float32
BF16 = jnp.bfloat16

D_MODEL = 2048
CTX_LEN = 256
GRID_W = 64
N_HEADS = 16
HEAD_DIM = 128
N_KV_HEADS = 4
GROUP = N_HEADS // N_KV_HEADS
AXIS_DIM = HEAD_DIM // 2
ROPE_THETA = 10000.0
HYENA_WIDTH = D_MODEL // 2
FILTER_HIDDEN = 64
FILTER_EMB = 17
FILTER_BANDS = (FILTER_EMB - 1) // 2
DECAY_TARGET = 1e-2
MIN_DECAY = math.log(DECAY_TARGET) / 1.5
MAX_DECAY = math.log(DECAY_TARGET) / 0.3
D_FF = 4 * D_MODEL
EPS = 1e-6
Q_W = N_HEADS * HEAD_DIM
KV_W = N_KV_HEADS * HEAD_DIM
Q_OFF = 3 * HYENA_WIDTH
K_OFF = Q_OFF + Q_W
V_OFF = K_OFF + KV_W
GA_OFF = V_OFF + KV_W
GB_OFF = GA_OFF + D_MODEL

LANES = 128
SUBLANES = 8
VMEM_LIMIT = 60 * 1024 * 1024
EMB_PAD = 128
MOD_ROWS = 8
HIGHEST = lax.Precision.HIGHEST
ONES_ROWS = 16
QK_SUB_ROWS = 256
OUT_SUB_ROWS = 128
FFT_BLOCK = 512


def _params(*sem):
    return pltpu.CompilerParams(dimension_semantics=sem, vmem_limit_bytes=VMEM_LIMIT)


def _dot(a, b):
    return jnp.dot(a, b, preferred_element_type=F32)


def _rms(x, g):
    return x * lax.rsqrt(jnp.mean(x * x, axis=-1, keepdims=True) + EPS) * g


def _adaln_kernel(c_ref, w_ref, b_ref, o_ref):
    c = c_ref[...]
    s = c * jax.nn.sigmoid(c)
    o_ref[...] = _dot(s.astype(BF16), w_ref[...].astype(BF16)) + b_ref[...]


def _adaln(cin, w, b, tn=1024):
    rows, d = cin.shape
    n = w.shape[1]
    return pl.pallas_call(
        _adaln_kernel,
        grid=(n // tn,),
        in_specs=[pl.BlockSpec((rows, d), lambda j: (0, 0)),
                  pl.BlockSpec((d, tn), lambda j: (0, j)),
                  pl.BlockSpec((1, tn), lambda j: (0, j))],
        out_specs=pl.BlockSpec((rows, tn), lambda j: (0, j)),
        out_shape=jax.ShapeDtypeStruct((rows, n), F32),
        compiler_params=_params("arbitrary"),
        name="adaln",
    )(cin, w, b)


def _mod_row(i, tiles_per_mod, mod_base):
    if tiles_per_mod is None:
        return mod_base
    return mod_base + i // tiles_per_mod


def _prenorm_kernel(x_ref, mod_ref, g_ref, o_ref, *, tiles_per_mod, mod_base):
    b = _mod_row(pl.program_id(0), tiles_per_mod, mod_base)
    y = _rms(x_ref[...], g_ref[...])
    sh = mod_ref[pl.ds(b, 1), 0:D_MODEL]
    sc = mod_ref[pl.ds(b, 1), D_MODEL:2 * D_MODEL]
    o_ref[...] = (y * (1.0 + sc) + sh).astype(o_ref.dtype)


def _prenorm(x, mod, g, *, tm, tiles_per_mod, mod_base):
    t, d = x.shape
    return pl.pallas_call(
        functools.partial(_prenorm_kernel, tiles_per_mod=tiles_per_mod, mod_base=mod_base),
        grid=(t // tm,),
        in_specs=[pl.BlockSpec((tm, d), lambda i: (i, 0)),
                  pl.BlockSpec(mod.shape, lambda i: (0, 0)),
                  pl.BlockSpec((1, d), lambda i: (0, 0))],
        out_specs=pl.BlockSpec((tm, d), lambda i: (i, 0)),
        out_shape=jax.ShapeDtypeStruct((t, d), BF16),
        compiler_params=_params("arbitrary"),
        name="prenorm",
    )(x, mod, g)


def _matmul_kernel(h_ref, w_ref, o_ref):
    o_ref[...] = _dot(h_ref[...], w_ref[...]).astype(o_ref.dtype)


def _matmul(h, w, out_dtype, *, tm, tn, name):
    t, d = h.shape
    n = w.shape[1]
    return pl.pallas_call(
        _matmul_kernel,
        grid=(t // tm, n // tn),
        in_specs=[pl.BlockSpec((tm, d), lambda i, j: (i, 0)),
                  pl.BlockSpec((d, tn), lambda i, j: (0, j))],
        out_specs=pl.BlockSpec((tm, tn), lambda i, j: (i, j)),
        out_shape=jax.ShapeDtypeStruct((t, n), out_dtype),
        compiler_params=_params("arbitrary", "arbitrary"),
        name=name,
    )(h, w)


def _proj_qk_kernel(*refs, heads, scale, rope):
    if rope:
        h_ref, w_ref, g_ref, cos_ref, sin_ref, o_ref = refs
    else:
        h_ref, w_ref, g_ref, o_ref = refs
    g = g_ref[...]
    tm = h_ref.shape[0]
    sub = min(tm, QK_SUB_ROWS)
    if rope:
        lane = lax.broadcasted_iota(jnp.int32, (sub, HEAD_DIM), 1)
        first = (lane & (AXIS_DIM // 2)) == 0
    for r in range(0, tm, sub):
        acc = _dot(h_ref[r:r + sub, :], w_ref[...])
        if rope:
            cos = cos_ref[r:r + sub, :]
            sin = sin_ref[r:r + sub, :]
        for hh in range(heads):
            y = _rms(acc[:, hh * HEAD_DIM:(hh + 1) * HEAD_DIM], g)
            if rope:
                partner = jnp.where(first,
                                    pltpu.roll(y, HEAD_DIM - AXIS_DIM // 2, 1),
                                    pltpu.roll(y, AXIS_DIM // 2, 1))
                y = y * cos + partner * sin
            if scale != 1.0:
                y = y * scale
            o_ref[r:r + sub, hh * HEAD_DIM:(hh + 1) * HEAD_DIM] = y.astype(o_ref.dtype)


def _proj_qk(h, w, g, cos, sin, *, tm, tn, scale, rope, name):
    t, d = h.shape
    n = w.shape[1]
    in_specs = [pl.BlockSpec((tm, d), lambda i, j: (i, 0)),
                pl.BlockSpec((d, tn), lambda i, j: (0, j)),
                pl.BlockSpec((1, HEAD_DIM), lambda i, j: (0, 0))]
    args = [h, w, g]
    if rope:
        seq_tiles = cos.shape[0] // tm
        in_specs += [pl.BlockSpec((tm, HEAD_DIM), lambda i, j: (i % seq_tiles, 0))] * 2
        args += [cos, sin]
    return pl.pallas_call(
        functools.partial(_proj_qk_kernel, heads=tn // HEAD_DIM, scale=scale, rope=rope),
        grid=(t // tm, n // tn),
        in_specs=in_specs,
        out_specs=pl.BlockSpec((tm, tn), lambda i, j: (i, j)),
        out_shape=jax.ShapeDtypeStruct((t, n), BF16),
        compiler_params=_params("arbitrary", "arbitrary"),
        name=name,
    )(*args)


def _dot_hi(a, b):
    return jnp.dot(a, b, precision=HIGHEST, preferred_element_type=F32)


def _filter_hidden_kernel(emb_ref, w1_ref, b1_ref, w2_ref, b2_ref, w3_ref, b3_ref, fr_ref, o_ref):
    fr = fr_ref[...]
    h = jnp.sin(fr * (_dot_hi(emb_ref[0], w1_ref[...]) + b1_ref[...]))
    h = jnp.sin(fr * (_dot_hi(h, w2_ref[...]) + b2_ref[...]))
    o_ref[0] = jnp.sin(fr * (_dot_hi(h, w3_ref[...]) + b3_ref[...]))


def _filter_hidden(emb, w1, b1, w2, b2, w3, b3, fr, *, tr=256):
    sides, L, e = emb.shape
    full = lambda a: pl.BlockSpec(a.shape, lambda s, r: (0,) * a.ndim)
    return pl.pallas_call(
        _filter_hidden_kernel,
        grid=(sides, L // tr),
        in_specs=[pl.BlockSpec((1, tr, e), lambda s, r: (s, r, 0)),
                  full(w1), full(b1), full(w2), full(b2), full(w3), full(b3), full(fr)],
        out_specs=pl.BlockSpec((1, tr, FILTER_HIDDEN), lambda s, r: (s, r, 0)),
        out_shape=jax.ShapeDtypeStruct((sides, L, FILTER_HIDDEN), F32),
        compiler_params=_params("arbitrary", "arbitrary"),
        name="filter_hidden",
    )(emb, w1, b1, w2, b2, w3, b3, fr)


def _filter_kernel(hid_ref, emb_ref, w4f_ref, w4b_ref, dl_ref, k_ref, norm_ref):
    r = pl.program_id(1)
    dl = dl_ref[...]
    kf = _dot_hi(hid_ref[0], w4f_ref[...]) * jnp.exp(-emb_ref[0, :, 0:1] * dl)
    kb = _dot_hi(hid_ref[1], w4b_ref[...]) * jnp.exp(-emb_ref[1, :, 0:1] * dl)
    tr = kb.shape[0]
    row = r * tr + lax.broadcasted_iota(jnp.int32, kb.shape, 0)
    kb = jnp.where(row == 0, 0.0, kb)
    k_ref[0] = kb.astype(k_ref.dtype)
    k_ref[1] = kf.astype(k_ref.dtype)

    @pl.when(r == 0)
    def _():
        norm_ref[...] = jnp.zeros(norm_ref.shape, F32)

    norm_ref[...] += jnp.sum(jnp.abs(kf), axis=0, keepdims=True) + jnp.sum(jnp.abs(kb), axis=0, keepdims=True)


def _hyena_filter(hid, emb, w4, deltas, *, tr=512, tc=256):
    _, L, e = emb.shape
    C = deltas.shape[1]
    return pl.pallas_call(
        _filter_kernel,
        grid=(C // tc, L // tr),
        in_specs=[pl.BlockSpec((2, tr, FILTER_HIDDEN), lambda c, r: (0, r, 0)),
                  pl.BlockSpec((2, tr, e), lambda c, r: (0, r, 0)),
                  pl.BlockSpec((FILTER_HIDDEN, tc), lambda c, r: (0, c)),
                  pl.BlockSpec((FILTER_HIDDEN, tc), lambda c, r: (0, c + C // tc)),
                  pl.BlockSpec((1, tc), lambda c, r: (0, c))],
        out_specs=[pl.BlockSpec((2, tr, tc), lambda c, r: (0, r, c)),
                   pl.BlockSpec((1, tc), lambda c, r: (0, c))],
        out_shape=[jax.ShapeDtypeStruct((2, L, C), BF16), jax.ShapeDtypeStruct((1, C), F32)],
        compiler_params=_params("arbitrary", "arbitrary"),
        name="hyena_filter",
    )(hid, emb, w4, w4, deltas)


def _hyena_pre_kernel(x0m, x0p, x0n, x1m, x1p, x1n, vm, vp, vn, w0, w1, w2, b0, b1, b2, fb,
                      z_ref, zb_ref, x0_ref, *, tiles_per_seq):
    pos = pl.program_id(0) % tiles_per_seq
    first = pos == 0
    last = pos == tiles_per_seq - 1
    tr, tc = x0m.shape
    row = lax.broadcasted_iota(jnp.int32, (tr, tc), 0)

    def conv(m_ref, p_ref, n_ref, w_ref, b_ref):
        u = m_ref[...]
        prev = jnp.where(first, 0.0, p_ref[SUBLANES - 1:SUBLANES, :])
        nxt = jnp.where(last, 0.0, n_ref[0:1, :])
        um = jnp.where(row == 0, prev, pltpu.roll(u, 1, 0))
        up = jnp.where(row == tr - 1, nxt, pltpu.roll(u, tr - 1, 0))
        w = w_ref[...]
        return um * w[0:1] + u * w[1:2] + up * w[2:3] + b_ref[...]

    x0 = conv(x0m, x0p, x0n, w0, b0)
    z = conv(x1m, x1p, x1n, w1, b1) * conv(vm, vp, vn, w2, b2)
    z_ref[...] = z.astype(z_ref.dtype)
    zb_ref[...] = z * fb[...]
    x0_ref[...] = x0


def _hyena_pre(u, conv_w, conv_b, fbias, *, seq, tr=512, tc=512):
    t = u.shape[0]
    C = HYENA_WIDTH
    cb = C // tc
    rb = tr // SUBLANES
    last_rb = t // SUBLANES - 1
    in_specs, args = [], []
    for part in range(3):
        off = part * cb
        in_specs += [pl.BlockSpec((tr, tc), lambda i, j, off=off: (i, j + off)),
                     pl.BlockSpec((SUBLANES, tc), lambda i, j, off=off: (jnp.maximum(i * rb - 1, 0), j + off)),
                     pl.BlockSpec((SUBLANES, tc), lambda i, j, off=off: (jnp.minimum((i + 1) * rb, last_rb), j + off))]
        args += [u, u, u]
    for part in range(3):
        in_specs.append(pl.BlockSpec((3, tc), lambda i, j, off=part * cb: (0, j + off)))
        args.append(conv_w)
    for part in range(3):
        in_specs.append(pl.BlockSpec((1, tc), lambda i, j, off=part * cb: (0, j + off)))
        args.append(conv_b)
    in_specs.append(pl.BlockSpec((1, tc), lambda i, j: (0, j)))
    args.append(fbias)
    out_spec = pl.BlockSpec((tr, tc), lambda i, j: (i, j))
    return pl.pallas_call(
        functools.partial(_hyena_pre_kernel, tiles_per_seq=seq // tr),
        grid=(t // tr, cb),
        in_specs=in_specs,
        out_specs=[out_spec, out_spec, out_spec],
        out_shape=[jax.ShapeDtypeStruct((t, C), BF16),
                   jax.ShapeDtypeStruct((t, C), F32),
                   jax.ShapeDtypeStruct((t, C), F32)],
        compiler_params=_params("arbitrary", "arbitrary"),
        name="hyena_pre",
    )(*args)


def _dft_fwd_kernel(cm_ref, sm_ref, z_ref, zr_ref, zi_ref):
    z = z_ref[0]
    zr_ref[0] = _dot(cm_ref[...], z)
    zi_ref[0] = _dot(sm_ref[...], z)


def _dft_fwd(cm, sm, z, *, tc=1024):
    nb, P, C = z.shape
    mat = pl.BlockSpec((P, P), lambda b, c: (0, 0))
    blk = pl.BlockSpec((1, P, tc), lambda b, c: (b, 0, c))
    return pl.pallas_call(
        _dft_fwd_kernel,
        grid=(nb, C // tc),
        in_specs=[mat, mat, blk],
        out_specs=[blk, blk],
        out_shape=[jax.ShapeDtypeStruct((nb, P, C), F32)] * 2,
        compiler_params=_params("arbitrary", "arbitrary"),
        name="dft_signal",
    )(cm, sm, z)


def _filter_spectrum_kernel(cm_ref, sf_ref, cms_ref, sfs_ref, hi_ref, lo_ref, norm_ref, gr_ref, gi_ref):
    hi = hi_ref[0]
    lo = lo_ref[0]
    gr = _dot(cm_ref[...], hi) + _dot(cms_ref[...], lo)
    gi = _dot(sf_ref[...], hi) + _dot(sfs_ref[...], lo)
    p = gr.shape[0]
    dc = lax.broadcasted_iota(jnp.int32, gr.shape, 0) == 0
    scale = jnp.where(dc, 0.5 / p, 1.0 / p) / norm_ref[...]
    gr_ref[0] = gr * scale
    gi_ref[0] = gi * scale


def _filter_spectrum(cm, sf, cms, sfs, taps, norm, *, tc=1024):
    nseg, P, C = taps.shape
    mat = pl.BlockSpec((P, P), lambda d, c: (0, 0))
    out = pl.BlockSpec((1, P, tc), lambda d, c: (d, 0, c))
    return pl.pallas_call(
        _filter_spectrum_kernel,
        grid=(nseg - 1, C // tc),
        in_specs=[mat, mat, mat, mat,
                  pl.BlockSpec((1, P, tc), lambda d, c: (d + 1, 0, c)),
                  pl.BlockSpec((1, P, tc), lambda d, c: (d, 0, c)),
                  pl.BlockSpec((1, tc), lambda d, c: (0, c))],
        out_specs=[out, out],
        out_shape=[jax.ShapeDtypeStruct((nseg - 1, P, C), F32)] * 2,
        compiler_params=_params("arbitrary", "arbitrary"),
        name="dft_filter",
    )(cm, sf, cms, sfs, taps, taps, norm)


def _conv_kernel(cm_ref, si_ref, zr_ref, zi_ref, gr_ref, gi_ref, x0_ref, zb_ref, o_ref, yr_sc, yi_sc, *, rows):
    nblk, P, tc = zr_ref.shape
    dc = lax.broadcasted_iota(jnp.int32, (rows, tc), 0) == 0

    def out_block(i, carry):
        for r in range(0, P, rows):
            rr = ii = ri = ir = None
            for j in range(nblk):
                lag = i - j + (nblk - 1)
                gr = gr_ref[lag, r:r + rows, :]
                gi = gi_ref[lag, r:r + rows, :]
                zr = zr_ref[j, r:r + rows, :]
                zi = zi_ref[j, r:r + rows, :]
                if j == 0:
                    rr, ii, ri, ir = gr * zr, gi * zi, gr * zi, gi * zr
                else:
                    rr, ii, ri, ir = rr + gr * zr, ii + gi * zi, ri + gr * zi, ir + gi * zr
            if r == 0:
                yr = jnp.where(dc, rr, rr - ii)
                yi = jnp.where(dc, ii, ri + ir)
            else:
                yr = rr - ii
                yi = ri + ir
            yr_sc[r:r + rows, :] = yr.astype(yr_sc.dtype)
            yi_sc[r:r + rows, :] = yi.astype(yi_sc.dtype)
        y = _dot(cm_ref[...], yr_sc[...]) + _dot(si_ref[...], yi_sc[...])
        t0 = pl.multiple_of(i * P, P)
        o_ref[pl.ds(t0, P), :] = (x0_ref[pl.ds(t0, P), :] * (y + zb_ref[pl.ds(t0, P), :])).astype(o_ref.dtype)
        return carry

    lax.fori_loop(0, nblk, out_block, 0)


def _conv(cm, si, zr, zi, gr, gi, x0, zb, *, seq, tc=128, rows=64):
    P = cm.shape[0]
    nblk = seq // P
    nlag = gr.shape[0]
    t, C = x0.shape
    mat = pl.BlockSpec((P, P), lambda c, b: (0, 0))
    zspec = pl.BlockSpec((nblk, P, tc), lambda c, b: (b, 0, c))
    gspec = pl.BlockSpec((nlag, P, tc), lambda c, b: (0, 0, c))
    tile = pl.BlockSpec((seq, tc), lambda c, b: (b, c))
    return pl.pallas_call(
        functools.partial(_conv_kernel, rows=rows),
        grid=(C // tc, t // seq),
        in_specs=[mat, mat, zspec, zspec, gspec, gspec, tile, tile],
        out_specs=tile,
        out_shape=jax.ShapeDtypeStruct((t, C), BF16),
        scratch_shapes=[pltpu.VMEM((P, tc), BF16), pltpu.VMEM((P, tc), BF16)],
        compiler_params=_params("arbitrary", "arbitrary"),
        name="hyena_conv",
    )(cm, si, zr, zi, gr, gi, x0, zb)


def _attn_kernel(q_ref, k_ref, v_ref, o_ref, vt_sc, *, tq, chunks):
    @pl.when(pl.program_id(2) == 0)
    def _():
        vt_sc[0:HEAD_DIM, :] = v_ref[0].T
        vt_sc[HEAD_DIM:, :] = jnp.ones((vt_sc.shape[0] - HEAD_DIM, vt_sc.shape[1]), vt_sc.dtype)

    qs = jnp.concatenate([q_ref[0, :, g * HEAD_DIM:(g + 1) * HEAD_DIM] for g in range(GROUP)], axis=0)
    n = GROUP * tq
    m = jnp.full((1, n), -jnp.inf, F32)
    acc = jnp.zeros((vt_sc.shape[0], n), F32)

    def scores(c):
        start, size = chunks[c]
        return lax.dot_general(k_ref[0, start:start + size, :], qs, (((1,), (1,)), ((), ())),
                               preferred_element_type=F32)

    st_next = scores(0)
    for c, (start, size) in enumerate(chunks):
        st = st_next
        if c + 1 < len(chunks):
            st_next = scores(c + 1)
        m_new = jnp.maximum(m, jnp.max(st, axis=0, keepdims=True))
        alpha = jnp.exp2(m - m_new)
        p = jnp.exp2(st - m_new).astype(BF16)
        acc = alpha * acc + _dot(vt_sc[:, start:start + size], p)
        m = m_new
    out = acc[0:HEAD_DIM] * (1.0 / acc[HEAD_DIM:HEAD_DIM + 1])
    for g in range(GROUP):
        o_ref[0, :, g * HEAD_DIM:(g + 1) * HEAD_DIM] = out[:, g * tq:(g + 1) * tq].T.astype(o_ref.dtype)


def _attention(q, k, v, *, tq=256, tk=512):
    B, L, _ = q.shape
    S = k.shape[1]
    gw = GROUP * HEAD_DIM
    kv_spec = pl.BlockSpec((1, S, HEAD_DIM), lambda b, h, i: (b, 0, h))
    q_spec = pl.BlockSpec((1, tq, gw), lambda b, h, i: (b, i, h))
    chunks = tuple((c, min(tk, S - c)) for c in range(0, S, tk))
    return pl.pallas_call(
        functools.partial(_attn_kernel, tq=tq, chunks=chunks),
        grid=(B, N_KV_HEADS, L // tq),
        in_specs=[q_spec, kv_spec, kv_spec],
        out_specs=q_spec,
        out_shape=jax.ShapeDtypeStruct((B, L, Q_W), BF16),
        scratch_shapes=[pltpu.VMEM((HEAD_DIM + ONES_ROWS, S), BF16)],
        compiler_params=_params("arbitrary", "arbitrary", "arbitrary"),
        name="attention",
    )(q, k, v)


def _merge_kernel(h_ref, ya_ref, yb_ref, wga_ref, wgb_ref, wba_ref, wbb_ref, o_ref):
    h = h_ref[...]
    ga = jax.nn.sigmoid(_dot(h, wga_ref[...]))
    gb = jax.nn.sigmoid(_dot(h, wgb_ref[...]))
    a = _dot(ya_ref[...], wba_ref[...])
    b = _dot(yb_ref[...], wbb_ref[...])
    o_ref[...] = (ga * a + gb * b).astype(o_ref.dtype)


def _merge(h, ya, yb, wga, wgb, wba, wbb, *, tm=1024, tn=256):
    t, d = h.shape
    row = lambda w: pl.BlockSpec((tm, w), lambda i, j: (i, 0))
    col = lambda kdim: pl.BlockSpec((kdim, tn), lambda i, j: (0, j))
    return pl.pallas_call(
        _merge_kernel,
        grid=(t // tm, d // tn),
        in_specs=[row(d), row(ya.shape[1]), row(yb.shape[1]),
                  col(d), col(d), col(ya.shape[1]), col(yb.shape[1])],
        out_specs=pl.BlockSpec((tm, tn), lambda i, j: (i, j)),
        out_shape=jax.ShapeDtypeStruct((t, d), BF16),
        compiler_params=_params("arbitrary", "arbitrary"),
        name="merge",
    )(h, ya, yb, wga, wgb, wba, wbb)


def _outproj_kernel(m_ref, wo_ref, x_ref, mod_ref, g_ref, x1_ref, h2_ref, *, tiles_per_mod):
    b = _mod_row(pl.program_id(0), tiles_per_mod, 0)
    gt1 = mod_ref[pl.ds(b, 1), 2 * D_MODEL:3 * D_MODEL]
    sh2 = mod_ref[pl.ds(b, 1), 3 * D_MODEL:4 * D_MODEL]
    sc2 = mod_ref[pl.ds(b, 1), 4 * D_MODEL:5 * D_MODEL]
    tm = m_ref.shape[0]
    sub = min(tm, OUT_SUB_ROWS)
    for r in range(0, tm, sub):
        mix = _dot(m_ref[r:r + sub, :], wo_ref[...])
        x1 = x_ref[r:r + sub, :] + gt1 * _rms(mix, g_ref[1:2, :])
        x1_ref[r:r + sub, :] = x1
        h2_ref[r:r + sub, :] = (_rms(x1, g_ref[2:3, :]) * (1.0 + sc2) + sh2).astype(h2_ref.dtype)


def _outproj(merged, wo, x, mod, gains, *, seq, tm=512):
    t, d = x.shape
    row = pl.BlockSpec((tm, d), lambda i: (i, 0))
    return pl.pallas_call(
        functools.partial(_outproj_kernel, tiles_per_mod=seq // tm),
        grid=(t // tm,),
        in_specs=[row, pl.BlockSpec((d, d), lambda i: (0, 0)), row,
                  pl.BlockSpec(mod.shape, lambda i: (0, 0)),
                  pl.BlockSpec(gains.shape, lambda i: (0, 0))],
        out_specs=[row, row],
        out_shape=[jax.ShapeDtypeStruct((t, d), F32), jax.ShapeDtypeStruct((t, d), BF16)],
        compiler_params=_params("arbitrary"),
        name="outproj",
    )(merged, wo, x, mod, gains)


def _mlp_kernel(h2_ref, w1_ref, w2_ref, x1_ref, mod_ref, g_ref, o_ref, *, tiles_per_mod):
    j = pl.program_id(1)

    @pl.when(j == 0)
    def _():
        o_ref[...] = jnp.zeros(o_ref.shape, F32)

    hid = jnp.maximum(_dot(h2_ref[...], w1_ref[...]), 0.0)
    o_ref[...] += _dot((hid * hid).astype(BF16), w2_ref[...])

    @pl.when(j == pl.num_programs(1) - 1)
    def _():
        b = _mod_row(pl.program_id(0), tiles_per_mod, 0)
        gt2 = mod_ref[pl.ds(b, 1), 5 * D_MODEL:6 * D_MODEL]
        o_ref[...] = x1_ref[...] + gt2 * _rms(o_ref[...], g_ref[3:4, :])


def _mlp(h2, w1, w2, x1, mod, gains, *, seq, tm=512, tf=512):
    t, d = x1.shape
    f = w1.shape[1]
    row = pl.BlockSpec((tm, d), lambda i, j: (i, 0))
    return pl.pallas_call(
        functools.partial(_mlp_kernel, tiles_per_mod=seq // tm),
        grid=(t // tm, f // tf),
        in_specs=[row, pl.BlockSpec((d, tf), lambda i, j: (0, j)), pl.BlockSpec((tf, d), lambda i, j: (j, 0)),
                  row, pl.BlockSpec(mod.shape, lambda i, j: (0, 0)),
                  pl.BlockSpec(gains.shape, lambda i, j: (0, 0))],
        out_specs=row,
        out_shape=jax.ShapeDtypeStruct((t, d), F32),
        compiler_params=_params("arbitrary", "arbitrary"),
        name="mlp",
    )(h2, w1, w2, x1, mod, gains)


def _rope_tables(seq):
    rows = jnp.repeat(jnp.arange(seq // GRID_W), GRID_W)
    cols = jnp.tile(jnp.arange(GRID_W), seq // GRID_W)
    inv = ROPE_THETA ** (-jnp.arange(0, AXIS_DIM, 2, dtype=F32) / AXIS_DIM)
    ar = rows[:, None] * inv
    ac = cols[:, None] * inv
    cos = jnp.concatenate([jnp.cos(ar), jnp.cos(ar), jnp.cos(ac), jnp.cos(ac)], axis=-1)
    sin = jnp.concatenate([-jnp.sin(ar), jnp.sin(ar), -jnp.sin(ac), jnp.sin(ac)], axis=-1)
    return cos, sin


def _filter_embedding(seq):
    t = jnp.linspace(0.0, 1.0, seq, dtype=F32)[:, None]
    wpos = 2.0 * math.pi * jnp.arange(seq, dtype=F32)[:, None] / seq
    bands = jnp.linspace(1e-4, FILTER_BANDS - 1, FILTER_BANDS, dtype=F32)
    emb = jnp.concatenate([t, jnp.cos(bands * wpos), -jnp.sin(bands * wpos)], axis=-1)
    emb = jnp.pad(emb, ((0, 0), (0, EMB_PAD - FILTER_EMB)))
    emb_b = jnp.concatenate([emb[:1], emb[:0:-1]], axis=0)
    return jnp.stack([emb, emb_b])


def _dft_tables(seq):
    idx = jnp.arange(seq, dtype=jnp.int32)
    phase = (idx[:, None] * idx[None, :]) & (2 * seq - 1)
    ang = phase.astype(F32) * (math.pi / seq)
    alt = (1 - 2 * (idx & 1)).astype(F32)
    cm = jnp.cos(ang)
    sn = jnp.sin(ang)
    sf = jnp.where(idx[:, None] == 0, alt[None, :], sn)
    si = jnp.where(idx[None, :] == 0, alt[:, None], sn)
    cms = alt[:, None] * cm
    sfs = alt[:, None] * sf
    return tuple(t.astype(BF16) for t in (cm, sf, si, cms, sfs))


def kernel(x, c, ctx, c_ctx, w_ada, b_ada, norm_gains, w_in, conv_w, conv_b, filt_w1, filt_b1, filt_w2, filt_b2, filt_w3, filt_b3, filt_w4, filt_freq, filt_bias, qk_gains, w_branch_a, w_branch_b, w_out, w_ff1, w_ff2):
    B, L, D = x.shape
    T = B * L
    C = HYENA_WIDTH
    lyr = 0
    gains = norm_gains[lyr]
    w_in_b = w_in[lyr].astype(BF16)

    cin = jnp.zeros((MOD_ROWS, D), F32).at[:B].set(c).at[B].set(c_ctx)
    mod = _adaln(cin, w_ada[lyr], b_ada[lyr][None])

    xf = x.reshape(T, D)
    tm_pre = 512
    h = _prenorm(xf, mod, gains[0:1], tm=tm_pre, tiles_per_mod=L // tm_pre, mod_base=0)
    hc = _prenorm(ctx.reshape(B * CTX_LEN, D), mod, gains[0:1], tm=B * CTX_LEN, tiles_per_mod=None, mod_base=B)

    cos, sin = _rope_tables(L)
    qg = qk_gains[lyr, 0][None]
    kg = qk_gains[lyr, 1][None]
    q_scale = HEAD_DIM ** -0.5 * math.log2(math.e)
    u = _matmul(h, w_in_b[:, :Q_OFF], F32, tm=1024, tn=1024, name="proj_u")
    q = _proj_qk(h, w_in_b[:, Q_OFF:K_OFF], qg, cos, sin, tm=1024, tn=512, scale=q_scale, rope=True, name="proj_q")
    k = _proj_qk(h, w_in_b[:, K_OFF:V_OFF], kg, cos, sin, tm=1024, tn=512, scale=1.0, rope=True, name="proj_k")
    v = _matmul(h, w_in_b[:, V_OFF:GA_OFF], BF16, tm=1024, tn=512, name="proj_v")
    kc = _proj_qk(hc, w_in_b[:, K_OFF:V_OFF], kg, None, None, tm=B * CTX_LEN, tn=512, scale=1.0, rope=False,
                  name="proj_kc")
    vc = _matmul(hc, w_in_b[:, V_OFF:GA_OFF], BF16, tm=B * CTX_LEN, tn=512, name="proj_vc")

    emb = _filter_embedding(L)
    w1p = jnp.pad(filt_w1[lyr], ((0, EMB_PAD - FILTER_EMB), (0, 0)))
    deltas = jnp.abs(jnp.linspace(MIN_DECAY, MAX_DECAY, C, dtype=F32))[None]
    hid = _filter_hidden(emb, w1p, filt_b1[lyr][None], filt_w2[lyr], filt_b2[lyr][None],
                         filt_w3[lyr], filt_b3[lyr][None], filt_freq[lyr][None])
    filt, norm = _hyena_filter(hid, emb, filt_w4[lyr], deltas)
    P = FFT_BLOCK
    cm, sf, si, cms, sfs = _dft_tables(P)
    gr, gi = _filter_spectrum(cm, sf, cms, sfs, filt.reshape(2 * L // P, P, C), norm)
    z, zb, x0 = _hyena_pre(u, conv_w[lyr], conv_b[lyr][None], filt_bias[lyr][None], seq=L)
    zr, zi = _dft_fwd(cm, sf, z.reshape(T // P, P, C))
    ya = _conv(cm, si, zr, zi, gr, gi, x0, zb, seq=L)

    k_all = jnp.concatenate([k.reshape(B, L, KV_W), kc.reshape(B, CTX_LEN, KV_W)], axis=1)
    v_all = jnp.concatenate([v.reshape(B, L, KV_W), vc.reshape(B, CTX_LEN, KV_W)], axis=1)
    yb = _attention(q.reshape(B, L, Q_W), k_all, v_all, tq=512, tk=512).reshape(T, Q_W)

    merged = _merge(h, ya, yb, w_in_b[:, GA_OFF:GB_OFF], w_in_b[:, GB_OFF:],
                    w_branch_a[lyr].astype(BF16), w_branch_b[lyr].astype(BF16))
    x1, h2 = _outproj(merged, w_out[lyr].astype(BF16), xf, mod, gains, seq=L)
    out = _mlp(h2, w_ff1[lyr].astype(BF16), w_ff2[lyr].astype(BF16), x1, mod, gains, seq=L, tm=1024, tf=512)
    return out.reshape(B, L, D)
```

```python
import functools
import math

import jax
import jax.numpy as jnp
from jax import lax
from jax.experimental import pallas as pl
from jax.experimental.pallas import tpu as pltpu

F32 = jnp.float32
BF16 = jnp.bfloat16

D_MODEL = 2048
CTX_LEN = 256
GRID_W = 64
N_HEADS = 16
HEAD_DIM = 128
N_KV_HEADS = 4
GROUP = N_HEADS // N_KV_HEADS
AXIS_DIM = HEAD_DIM // 2
ROPE_THETA = 10000.0
HYENA_WIDTH = D_MODEL // 2
FILTER_HIDDEN = 64
FILTER_EMB = 17
FILTER_BANDS = (FILTER_EMB - 1) // 2
DECAY_TARGET = 1e-2
MIN_DECAY = math.log(DECAY_TARGET) / 1.5
MAX_DECAY = math.log(DECAY_TARGET) / 0.3
D_FF = 4 * D_MODEL
EPS = 1e-6
Q_W = N_HEADS * HEAD_DIM
KV_W = N_KV_HEADS * HEAD_DIM
Q_OFF = 3 * HYENA_WIDTH
K_OFF = Q_OFF + Q_W
V_OFF = K_OFF + KV_W
GA_OFF = V_OFF + KV_W
GB_OFF = GA_OFF + D_MODEL

LANES = 128
SUBLANES = 8
VMEM_LIMIT = 60 * 1024 * 1024
EMB_PAD = 128
MOD_ROWS = 8
HIGHEST = lax.Precision.HIGHEST
ONES_ROWS = 16
IN_TN = KV_W
QK_SUB_ROWS = 256
OUT_SUB_ROWS = 128
FFT_BLOCK = 512


def _params(*sem):
    return pltpu.CompilerParams(dimension_semantics=sem, vmem_limit_bytes=VMEM_LIMIT)


def _dot(a, b):
    return jnp.dot(a, b, preferred_element_type=F32)


def _rms(x, g):
    return x * lax.rsqrt(jnp.mean(x * x, axis=-1, keepdims=True) + EPS) * g


def _adaln_kernel(c_ref, w_ref, b_ref, o_ref):
    c = c_ref[...]
    s = c * jax.nn.sigmoid(c)
    o_ref[...] = _dot(s.astype(BF16), w_ref[...].astype(BF16)) + b_ref[...]


def _adaln(cin, w, b, tn=1024):
    rows, d = cin.shape
    n = w.shape[1]
    return pl.pallas_call(
        _adaln_kernel,
        grid=(n // tn,),
        in_specs=[pl.BlockSpec((rows, d), lambda j: (0, 0)),
                  pl.BlockSpec((d, tn), lambda j: (0, j)),
                  pl.BlockSpec((1, tn), lambda j: (0, j))],
        out_specs=pl.BlockSpec((rows, tn), lambda j: (0, j)),
        out_shape=jax.ShapeDtypeStruct((rows, n), F32),
        compiler_params=_params("arbitrary"),
        name="adaln",
    )(cin, w, b)


def _mod_row(i, tiles_per_mod, mod_base):
    if tiles_per_mod is None:
        return mod_base
    return mod_base + i // tiles_per_mod


def _modnorm(x, g, mod_ref, b):
    sh = mod_ref[pl.ds(b, 1), 0:D_MODEL]
    sc = mod_ref[pl.ds(b, 1), D_MODEL:2 * D_MODEL]
    return (_rms(x, g) * (1.0 + sc) + sh).astype(BF16)


def _qk_head_epilogue(acc, g, cos, sin, first, scale, store):
    for hh in range(acc.shape[1] // HEAD_DIM):
        y = _rms(acc[:, hh * HEAD_DIM:(hh + 1) * HEAD_DIM], g)
        if cos is not None:
            partner = jnp.where(first,
                                pltpu.roll(y, HEAD_DIM - AXIS_DIM // 2, 1),
                                pltpu.roll(y, AXIS_DIM // 2, 1))
            y = y * cos + partner * sin
        if scale != 1.0:
            y = y * scale
        store(hh, y.astype(BF16))


def _inproj_kernel(x_ref, mod_ref, g_ref, w_ref, qg_ref, kg_ref, cos_ref, sin_ref,
                   h_ref, u_ref, q_ref, k_ref, v_ref, *, tiles_per_mod, n_u, n_q, q_scale):
    i = pl.program_id(0)
    j = pl.program_id(1)
    tm = x_ref.shape[0]
    sub = min(tm, QK_SUB_ROWS)

    @pl.when(j == 0)
    def _():
        b = _mod_row(i, tiles_per_mod, 0)
        for r in range(0, tm, sub):
            hr = _modnorm(x_ref[r:r + sub, :], g_ref[...], mod_ref, b)
            h_ref[r:r + sub, :] = hr
            u_ref[r:r + sub, :] = _dot(hr, w_ref[...])

    @pl.when((j > 0) & (j < n_u))
    def _():
        u_ref[...] = _dot(h_ref[...], w_ref[...])

    def qk_tile(g_ref_, scale, store):
        lane = lax.broadcasted_iota(jnp.int32, (sub, HEAD_DIM), 1)
        first = (lane & (AXIS_DIM // 2)) == 0
        for r in range(0, tm, sub):
            acc = _dot(h_ref[r:r + sub, :], w_ref[...])
            _qk_head_epilogue(acc, g_ref_[...], cos_ref[r:r + sub, :], sin_ref[r:r + sub, :], first, scale,
                              functools.partial(store, r))

    @pl.when((j >= n_u) & (j < n_u + n_q))
    def _():
        def store(r, hh, y):
            q_ref[r:r + sub, hh * HEAD_DIM:(hh + 1) * HEAD_DIM] = y
        qk_tile(qg_ref, q_scale, store)

    @pl.when(j == n_u + n_q)
    def _():
        def store(r, hh, y):
            k_ref[0, r:r + sub, hh * HEAD_DIM:(hh + 1) * HEAD_DIM] = y
        qk_tile(kg_ref, 1.0, store)

    @pl.when(j == n_u + n_q + 1)
    def _():
        v_ref[0] = _dot(h_ref[...], w_ref[...]).astype(v_ref.dtype)


def _inproj(x, mod, g, w_in_b, qg, kg, cos, sin, *, seq, kv_len, q_scale, tm=1024):
    t, d = x.shape
    tn = IN_TN
    n_u, n_q = Q_OFF // tn, Q_W // tn
    seq_tiles = seq // tm
    row = pl.BlockSpec((tm, d), lambda i, j: (i, 0))
    gain = pl.BlockSpec((1, HEAD_DIM), lambda i, j: (0, 0))
    table = pl.BlockSpec((tm, HEAD_DIM), lambda i, j: (i % seq_tiles, 0))
    kv = pl.BlockSpec((1, tm, KV_W), lambda i, j: (i // seq_tiles, i % seq_tiles, 0))
    return pl.pallas_call(
        functools.partial(_inproj_kernel, tiles_per_mod=seq_tiles, n_u=n_u, n_q=n_q, q_scale=q_scale),
        grid=(t // tm, n_u + n_q + 2),
        in_specs=[row, pl.BlockSpec(mod.shape, lambda i, j: (0, 0)), pl.BlockSpec((1, d), lambda i, j: (0, 0)),
                  pl.BlockSpec((d, tn), lambda i, j: (0, j)), gain, gain, table, table],
        out_specs=[row,
                   pl.BlockSpec((tm, tn), lambda i, j: (i, jnp.minimum(j, n_u - 1))),
                   pl.BlockSpec((tm, tn), lambda i, j: (i, jnp.clip(j - n_u, 0, n_q - 1))),
                   kv, kv],
        out_shape=[jax.ShapeDtypeStruct((t, d), BF16),
                   jax.ShapeDtypeStruct((t, Q_OFF), F32),
                   jax.ShapeDtypeStruct((t, Q_W), BF16),
                   jax.ShapeDtypeStruct((t // seq, kv_len, KV_W), BF16),
                   jax.ShapeDtypeStruct((t // seq, kv_len, KV_W), BF16)],
        compiler_params=_params("arbitrary", "arbitrary"),
        name="inproj",
    )(x, mod, g, w_in_b, qg, kg, cos, sin)


def _ctx_kv_kernel(x_ref, mod_ref, g_ref, wk_ref, wv_ref, kg_ref, kin_ref, vin_ref, k_ref, v_ref, *, mod_row):
    del kin_ref, vin_ref
    hc = _modnorm(x_ref[0], g_ref[...], mod_ref, mod_row)

    def store(hh, y):
        k_ref[0, :, hh * HEAD_DIM:(hh + 1) * HEAD_DIM] = y
    _qk_head_epilogue(_dot(hc, wk_ref[...]), kg_ref[...], None, None, None, 1.0, store)
    v_ref[0] = _dot(hc, wv_ref[...]).astype(v_ref.dtype)


def _ctx_kv(ctx, mod, g, w_in_b, kg, k_all, v_all, *, seq, mod_row):
    nb, lc, d = ctx.shape
    anyspace = pl.BlockSpec(memory_space=pl.ANY)
    out = pl.BlockSpec((1, lc, KV_W), lambda b: (b, seq // lc, 0))
    return pl.pallas_call(
        functools.partial(_ctx_kv_kernel, mod_row=mod_row),
        grid=(nb,),
        in_specs=[pl.BlockSpec((1, lc, d), lambda b: (b, 0, 0)),
                  pl.BlockSpec(mod.shape, lambda b: (0, 0)), pl.BlockSpec((1, d), lambda b: (0, 0)),
                  pl.BlockSpec((d, KV_W), lambda b: (0, K_OFF // KV_W)),
                  pl.BlockSpec((d, KV_W), lambda b: (0, V_OFF // KV_W)),
                  pl.BlockSpec((1, HEAD_DIM), lambda b: (0, 0)), anyspace, anyspace],
        out_specs=[out, out],
        out_shape=[jax.ShapeDtypeStruct(k_all.shape, k_all.dtype), jax.ShapeDtypeStruct(v_all.shape, v_all.dtype)],
        input_output_aliases={6: 0, 7: 1},
        compiler_params=_params("arbitrary"),
        name="ctx_kv",
    )(ctx, mod, g, w_in_b, w_in_b, kg, k_all, v_all)


def _dot_hi(a, b):
    return jnp.dot(a, b, precision=HIGHEST, preferred_element_type=F32)


def _filter_hidden_kernel(emb_ref, w1_ref, b1_ref, w2_ref, b2_ref, w3_ref, b3_ref, fr_ref, o_ref):
    fr = fr_ref[...]
    h = jnp.sin(fr * (_dot_hi(emb_ref[0], w1_ref[...]) + b1_ref[...]))
    h = jnp.sin(fr * (_dot_hi(h, w2_ref[...]) + b2_ref[...]))
    o_ref[0] = jnp.sin(fr * (_dot_hi(h, w3_ref[...]) + b3_ref[...]))


def _filter_hidden(emb, w1, b1, w2, b2, w3, b3, fr, *, tr=256):
    sides, L, e = emb.shape
    full = lambda a: pl.BlockSpec(a.shape, lambda s, r: (0,) * a.ndim)
    return pl.pallas_call(
        _filter_hidden_kernel,
        grid=(sides, L // tr),
        in_specs=[pl.BlockSpec((1, tr, e), lambda s, r: (s, r, 0)),
                  full(w1), full(b1), full(w2), full(b2), full(w3), full(b3), full(fr)],
        out_specs=pl.BlockSpec((1, tr, FILTER_HIDDEN), lambda s, r: (s, r, 0)),
        out_shape=jax.ShapeDtypeStruct((sides, L, FILTER_HIDDEN), F32),
        compiler_params=_params("arbitrary", "arbitrary"),
        name="filter_hidden",
    )(emb, w1, b1, w2, b2, w3, b3, fr)


def _filter_kernel(hid_ref, emb_ref, w4f_ref, w4b_ref, dl_ref, k_ref, norm_ref):
    r = pl.program_id(1)
    dl = dl_ref[...]
    kf = _dot_hi(hid_ref[0], w4f_ref[...]) * jnp.exp(-emb_ref[0, :, 0:1] * dl)
    kb = _dot_hi(hid_ref[1], w4b_ref[...]) * jnp.exp(-emb_ref[1, :, 0:1] * dl)
    tr = kb.shape[0]
    row = r * tr + lax.broadcasted_iota(jnp.int32, kb.shape, 0)
    kb = jnp.where(row == 0, 0.0, kb)
    k_ref[0] = kb.astype(k_ref.dtype)
    k_ref[1] = kf.astype(k_ref.dtype)

    @pl.when(r == 0)
    def _():
        norm_ref[...] = jnp.zeros(norm_ref.shape, F32)

    norm_ref[...] += jnp.sum(jnp.abs(kf), axis=0, keepdims=True) + jnp.sum(jnp.abs(kb), axis=0, keepdims=True)


def _hyena_filter(hid, emb, w4, deltas, *, tr=512, tc=256):
    _, L, e = emb.shape
    C = deltas.shape[1]
    return pl.pallas_call(
        _filter_kernel,
        grid=(C // tc, L // tr),
        in_specs=[pl.BlockSpec((2, tr, FILTER_HIDDEN), lambda c, r: (0, r, 0)),
                  pl.BlockSpec((2, tr, e), lambda c, r: (0, r, 0)),
                  pl.BlockSpec((FILTER_HIDDEN, tc), lambda c, r: (0, c)),
                  pl.BlockSpec((FILTER_HIDDEN, tc), lambda c, r: (0, c + C // tc)),
                  pl.BlockSpec((1, tc), lambda c, r: (0, c))],
        out_specs=[pl.BlockSpec((2, tr, tc), lambda c, r: (0, r, c)),
                   pl.BlockSpec((1, tc), lambda c, r: (0, c))],
        out_shape=[jax.ShapeDtypeStruct((2, L, C), BF16), jax.ShapeDtypeStruct((1, C), F32)],
        compiler_params=_params("arbitrary", "arbitrary"),
        name="hyena_filter",
    )(hid, emb, w4, w4, deltas)


def _hyena_pre_kernel(x0m, x0p, x0n, x1m, x1p, x1n, vm, vp, vn, w0, w1, w2, b0, b1, b2, fb,
                      z_ref, zb_ref, x0_ref, *, tiles_per_seq):
    pos = pl.program_id(0) % tiles_per_seq
    first = pos == 0
    last = pos == tiles_per_seq - 1
    tr, tc = x0m.shape
    row = lax.broadcasted_iota(jnp.int32, (tr, tc), 0)

    def conv(m_ref, p_ref, n_ref, w_ref, b_ref):
        u = m_ref[...]
        prev = jnp.where(first, 0.0, p_ref[SUBLANES - 1:SUBLANES, :])
        nxt = jnp.where(last, 0.0, n_ref[0:1, :])
        um = jnp.where(row == 0, prev, pltpu.roll(u, 1, 0))
        up = jnp.where(row == tr - 1, nxt, pltpu.roll(u, tr - 1, 0))
        w = w_ref[...]
        return um * w[0:1] + u * w[1:2] + up * w[2:3] + b_ref[...]

    x0 = conv(x0m, x0p, x0n, w0, b0)
    z = conv(x1m, x1p, x1n, w1, b1) * conv(vm, vp, vn, w2, b2)
    z_ref[...] = z.astype(z_ref.dtype)
    zb_ref[...] = z * fb[...]
    x0_ref[...] = x0


def _hyena_pre(u, conv_w, conv_b, fbias, *, seq, tr=512, tc=512):
    t = u.shape[0]
    C = HYENA_WIDTH
    cb = C // tc
    rb = tr // SUBLANES
    last_rb = t // SUBLANES - 1
    in_specs, args = [], []
    for part in range(3):
        off = part * cb
        in_specs += [pl.BlockSpec((tr, tc), lambda i, j, off=off: (i, j + off)),
                     pl.BlockSpec((SUBLANES, tc), lambda i, j, off=off: (jnp.maximum(i * rb - 1, 0), j + off)),
                     pl.BlockSpec((SUBLANES, tc), lambda i, j, off=off: (jnp.minimum((i + 1) * rb, last_rb), j + off))]
        args += [u, u, u]
    for part in range(3):
        in_specs.append(pl.BlockSpec((3, tc), lambda i, j, off=part * cb: (0, j + off)))
        args.append(conv_w)
    for part in range(3):
        in_specs.append(pl.BlockSpec((1, tc), lambda i, j, off=part * cb: (0, j + off)))
        args.append(conv_b)
    in_specs.append(pl.BlockSpec((1, tc), lambda i, j: (0, j)))
    args.append(fbias)
    out_spec = pl.BlockSpec((tr, tc), lambda i, j: (i, j))
    return pl.pallas_call(
        functools.partial(_hyena_pre_kernel, tiles_per_seq=seq // tr),
        grid=(t // tr, cb),
        in_specs=in_specs,
        out_specs=[out_spec, out_spec, out_spec],
        out_shape=[jax.ShapeDtypeStruct((t, C), BF16),
                   jax.ShapeDtypeStruct((t, C), F32),
                   jax.ShapeDtypeStruct((t, C), F32)],
        compiler_params=_params("arbitrary", "arbitrary"),
        name="hyena_pre",
    )(*args)


def _dft_fwd_kernel(cm_ref, sm_ref, z_ref, zr_ref, zi_ref):
    z = z_ref[0]
    zr_ref[0] = _dot(cm_ref[...], z)
    zi_ref[0] = _dot(sm_ref[...], z)


def _dft_fwd(cm, sm, z, *, tc=1024):
    nb, P, C = z.shape
    mat = pl.BlockSpec((P, P), lambda b, c: (0, 0))
    blk = pl.BlockSpec((1, P, tc), lambda b, c: (b, 0, c))
    return pl.pallas_call(
        _dft_fwd_kernel,
        grid=(nb, C // tc),
        in_specs=[mat, mat, blk],
        out_specs=[blk, blk],
        out_shape=[jax.ShapeDtypeStruct((nb, P, C), F32)] * 2,
        compiler_params=_params("arbitrary", "arbitrary"),
        name="dft_signal",
    )(cm, sm, z)


def _filter_spectrum_kernel(cm_ref, sf_ref, cms_ref, sfs_ref, hi_ref, lo_ref, norm_ref, gr_ref, gi_ref):
    hi = hi_ref[0]
    lo = lo_ref[0]
    gr = _dot(cm_ref[...], hi) + _dot(cms_ref[...], lo)
    gi = _dot(sf_ref[...], hi) + _dot(sfs_ref[...], lo)
    p = gr.shape[0]
    dc = lax.broadcasted_iota(jnp.int32, gr.shape, 0) == 0
    scale = jnp.where(dc, 0.5 / p, 1.0 / p) / norm_ref[...]
    gr_ref[0] = gr * scale
    gi_ref[0] = gi * scale


def _filter_spectrum(cm, sf, cms, sfs, taps, norm, *, tc=1024):
    nseg, P, C = taps.shape
    mat = pl.BlockSpec((P, P), lambda d, c: (0, 0))
    out = pl.BlockSpec((1, P, tc), lambda d, c: (d, 0, c))
    return pl.pallas_call(
        _filter_spectrum_kernel,
        grid=(nseg - 1, C // tc),
        in_specs=[mat, mat, mat, mat,
                  pl.BlockSpec((1, P, tc), lambda d, c: (d + 1, 0, c)),
                  pl.BlockSpec((1, P, tc), lambda d, c: (d, 0, c)),
                  pl.BlockSpec((1, tc), lambda d, c: (0, c))],
        out_specs=[out, out],
        out_shape=[jax.ShapeDtypeStruct((nseg - 1, P, C), F32)] * 2,
        compiler_params=_params("arbitrary", "arbitrary"),
        name="dft_filter",
    )(cm, sf, cms, sfs, taps, taps, norm)


def _conv_kernel(cm_ref, si_ref, zr_ref, zi_ref, gr_ref, gi_ref, x0_ref, zb_ref, o_ref, yr_sc, yi_sc, *, rows):
    nblk, P, tc = zr_ref.shape
    dc = lax.broadcasted_iota(jnp.int32, (rows, tc), 0) == 0

    def out_block(i, carry):
        for r in range(0, P, rows):
            rr = ii = ri = ir = None
            for j in range(nblk):
                lag = i - j + (nblk - 1)
                gr = gr_ref[lag, r:r + rows, :]
                gi = gi_ref[lag, r:r + rows, :]
                zr = zr_ref[j, r:r + rows, :]
                zi = zi_ref[j, r:r + rows, :]
                if j == 0:
                    rr, ii, ri, ir = gr * zr, gi * zi, gr * zi, gi * zr
                else:
                    rr, ii, ri, ir = rr + gr * zr, ii + gi * zi, ri + gr * zi, ir + gi * zr
            if r == 0:
                yr = jnp.where(dc, rr, rr - ii)
                yi = jnp.where(dc, ii, ri + ir)
            else:
                yr = rr - ii
                yi = ri + ir
            yr_sc[r:r + rows, :] = yr.astype(yr_sc.dtype)
            yi_sc[r:r + rows, :] = yi.astype(yi_sc.dtype)
        y = _dot(cm_ref[...], yr_sc[...]) + _dot(si_ref[...], yi_sc[...])
        t0 = pl.multiple_of(i * P, P)
        o_ref[pl.ds(t0, P), :] = (x0_ref[pl.ds(t0, P), :] * (y + zb_ref[pl.ds(t0, P), :])).astype(o_ref.dtype)
        return carry

    lax.fori_loop(0, nblk, out_block, 0)


def _conv(cm, si, zr, zi, gr, gi, x0, zb, *, seq, tc=128, rows=64):
    P = cm.shape[0]
    nblk = seq // P
    nlag = gr.shape[0]
    t, C = x0.shape
    mat = pl.BlockSpec((P, P), lambda c, b: (0, 0))
    zspec = pl.BlockSpec((nblk, P, tc), lambda c, b: (b, 0, c))
    gspec = pl.BlockSpec((nlag, P, tc), lambda c, b: (0, 0, c))
    tile = pl.BlockSpec((seq, tc), lambda c, b: (b, c))
    return pl.pallas_call(
        functools.partial(_conv_kernel, rows=rows),
        grid=(C // tc, t // seq),
        in_specs=[mat, mat, zspec, zspec, gspec, gspec, tile, tile],
        out_specs=tile,
        out_shape=jax.ShapeDtypeStruct((t, C), BF16),
        scratch_shapes=[pltpu.VMEM((P, tc), BF16), pltpu.VMEM((P, tc), BF16)],
        compiler_params=_params("arbitrary", "arbitrary"),
        name="hyena_conv",
    )(cm, si, zr, zi, gr, gi, x0, zb)


def _attn_kernel(q_ref, k_ref, v_ref, o_ref, vt_sc, *, tq, chunks):
    @pl.when(pl.program_id(2) == 0)
    def _():
        vt_sc[0:HEAD_DIM, :] = v_ref[0].T
        vt_sc[HEAD_DIM:, :] = jnp.ones((vt_sc.shape[0] - HEAD_DIM, vt_sc.shape[1]), vt_sc.dtype)

    qs = jnp.concatenate([q_ref[0, :, g * HEAD_DIM:(g + 1) * HEAD_DIM] for g in range(GROUP)], axis=0)
    n = GROUP * tq
    m = jnp.full((1, n), -jnp.inf, F32)
    acc = jnp.zeros((vt_sc.shape[0], n), F32)

    def scores(c):
        start, size = chunks[c]
        return lax.dot_general(k_ref[0, start:start + size, :], qs, (((1,), (1,)), ((), ())),
                               preferred_element_type=F32)

    st_next = scores(0)
    for c, (start, size) in enumerate(chunks):
        st = st_next
        if c + 1 < len(chunks):
            st_next = scores(c + 1)
        m_new = jnp.maximum(m, jnp.max(st, axis=0, keepdims=True))
        alpha = jnp.exp2(m - m_new)
        p = jnp.exp2(st - m_new).astype(BF16)
        acc = alpha * acc + _dot(vt_sc[:, start:start + size], p)
        m = m_new
    out = acc[0:HEAD_DIM] * (1.0 / acc[HEAD_DIM:HEAD_DIM + 1])
    for g in range(GROUP):
        o_ref[0, :, g * HEAD_DIM:(g + 1) * HEAD_DIM] = out[:, g * tq:(g + 1) * tq].T.astype(o_ref.dtype)


def _attention(q, k, v, *, tq=256, tk=512):
    B, L, _ = q.shape
    S = k.shape[1]
    gw = GROUP * HEAD_DIM
    kv_spec = pl.BlockSpec((1, S, HEAD_DIM), lambda b, h, i: (b, 0, h))
    q_spec = pl.BlockSpec((1, tq, gw), lambda b, h, i: (b, i, h))
    chunks = tuple((c, min(tk, S - c)) for c in range(0, S, tk))
    return pl.pallas_call(
        functools.partial(_attn_kernel, tq=tq, chunks=chunks),
        grid=(B, N_KV_HEADS, L // tq),
        in_specs=[q_spec, kv_spec, kv_spec],
        out_specs=q_spec,
        out_shape=jax.ShapeDtypeStruct((B, L, Q_W), BF16),
        scratch_shapes=[pltpu.VMEM((HEAD_DIM + ONES_ROWS, S), BF16)],
        compiler_params=_params("arbitrary", "arbitrary", "arbitrary"),
        name="attention",
    )(q, k, v)


def _merge_kernel(h_ref, ya_ref, yb_ref, wga_ref, wgb_ref, wba_ref, wbb_ref, o_ref):
    h = h_ref[...]
    ga = jax.nn.sigmoid(_dot(h, wga_ref[...]))
    gb = jax.nn.sigmoid(_dot(h, wgb_ref[...]))
    a = _dot(ya_ref[...], wba_ref[...])
    b = _dot(yb_ref[...], wbb_ref[...])
    o_ref[...] = (ga * a + gb * b).astype(o_ref.dtype)


def _merge(h, ya, yb, w_in_b, wba, wbb, *, tm=1024, tn=256):
    t, d = h.shape
    row = lambda w: pl.BlockSpec((tm, w), lambda i, j: (i, 0))
    col = lambda kdim, off=0: pl.BlockSpec((kdim, tn), lambda i, j: (0, j + off // tn))
    return pl.pallas_call(
        _merge_kernel,
        grid=(t // tm, d // tn),
        in_specs=[row(d), row(ya.shape[1]), row(yb.shape[1]),
                  col(d, GA_OFF), col(d, GB_OFF), col(ya.shape[1]), col(yb.shape[1])],
        out_specs=pl.BlockSpec((tm, tn), lambda i, j: (i, j)),
        out_shape=jax.ShapeDtypeStruct((t, d), BF16),
        compiler_params=_params("arbitrary", "arbitrary"),
        name="merge",
    )(h, ya, yb, w_in_b, w_in_b, wba, wbb)


def _outproj_kernel(m_ref, wo_ref, x_ref, mod_ref, g_ref, x1_ref, h2_ref, *, tiles_per_mod):
    b = _mod_row(pl.program_id(0), tiles_per_mod, 0)
    gt1 = mod_ref[pl.ds(b, 1), 2 * D_MODEL:3 * D_MODEL]
    sh2 = mod_ref[pl.ds(b, 1), 3 * D_MODEL:4 * D_MODEL]
    sc2 = mod_ref[pl.ds(b, 1), 4 * D_MODEL:5 * D_MODEL]
    tm = m_ref.shape[0]
    sub = min(tm, OUT_SUB_ROWS)
    for r in range(0, tm, sub):
        mix = _dot(m_ref[r:r + sub, :], wo_ref[...])
        x1 = x_ref[r:r + sub, :] + gt1 * _rms(mix, g_ref[1:2, :])
        x1_ref[r:r + sub, :] = x1
        h2_ref[r:r + sub, :] = (_rms(x1, g_ref[2:3, :]) * (1.0 + sc2) + sh2).astype(h2_ref.dtype)


def _outproj(merged, wo, x, mod, gains, *, seq, tm=512):
    t, d = x.shape
    row = pl.BlockSpec((tm, d), lambda i: (i, 0))
    return pl.pallas_call(
        functools.partial(_outproj_kernel, tiles_per_mod=seq // tm),
        grid=(t // tm,),
        in_specs=[row, pl.BlockSpec((d, d), lambda i: (0, 0)), row,
                  pl.BlockSpec(mod.shape, lambda i: (0, 0)),
                  pl.BlockSpec(gains.shape, lambda i: (0, 0))],
        out_specs=[row, row],
        out_shape=[jax.ShapeDtypeStruct((t, d), F32), jax.ShapeDtypeStruct((t, d), BF16)],
        compiler_params=_params("arbitrary"),
        name="outproj",
    )(merged, wo, x, mod, gains)


def _mlp_kernel(h2_ref, w1_ref, w2_ref, x1_ref, mod_ref, g_ref, o_ref, *, tiles_per_mod):
    j = pl.program_id(1)

    @pl.when(j == 0)
    def _():
        o_ref[...] = jnp.zeros(o_ref.shape, F32)

    last = pl.num_programs(1) - 1

    def partial_sum(rows):
        hid = jnp.maximum(_dot(h2_ref[rows, :], w1_ref[...]), 0.0)
        return o_ref[rows, :] + _dot((hid * hid).astype(BF16), w2_ref[...])

    @pl.when(j < last)
    def _():
        o_ref[...] = partial_sum(slice(None))

    @pl.when(j == last)
    def _():
        b = _mod_row(pl.program_id(0), tiles_per_mod, 0)
        gt2 = mod_ref[pl.ds(b, 1), 5 * D_MODEL:6 * D_MODEL]
        tm = o_ref.shape[0]
        sub = min(tm, OUT_SUB_ROWS)
        for r in range(0, tm, sub):
            rows = slice(r, r + sub)
            o_ref[rows, :] = x1_ref[rows, :] + gt2 * _rms(partial_sum(rows), g_ref[3:4, :])


def _mlp(h2, w1, w2, x1, mod, gains, *, seq, tm=512, tf=512):
    t, d = x1.shape
    f = w1.shape[1]
    row = pl.BlockSpec((tm, d), lambda i, j: (i, 0))
    return pl.pallas_call(
        functools.partial(_mlp_kernel, tiles_per_mod=seq // tm),
        grid=(t // tm, f // tf),
        in_specs=[row, pl.BlockSpec((d, tf), lambda i, j: (0, j)), pl.BlockSpec((tf, d), lambda i, j: (j, 0)),
                  row, pl.BlockSpec(mod.shape, lambda i, j: (0, 0)),
                  pl.BlockSpec(gains.shape, lambda i, j: (0, 0))],
        out_specs=row,
        out_shape=jax.ShapeDtypeStruct((t, d), F32),
        compiler_params=_params("arbitrary", "arbitrary"),
        name="mlp",
    )(h2, w1, w2, x1, mod, gains)


def _rope_tables(seq):
    rows = jnp.repeat(jnp.arange(seq // GRID_W), GRID_W)
    cols = jnp.tile(jnp.arange(GRID_W), seq // GRID_W)
    inv = ROPE_THETA ** (-jnp.arange(0, AXIS_DIM, 2, dtype=F32) / AXIS_DIM)
    ar = rows[:, None] * inv
    ac = cols[:, None] * inv
    cos = jnp.concatenate([jnp.cos(ar), jnp.cos(ar), jnp.cos(ac), jnp.cos(ac)], axis=-1)
    sin = jnp.concatenate([-jnp.sin(ar), jnp.sin(ar), -jnp.sin(ac), jnp.sin(ac)], axis=-1)
    return cos, sin


def _filter_embedding(seq):
    t = jnp.linspace(0.0, 1.0, seq, dtype=F32)[:, None]
    wpos = 2.0 * math.pi * jnp.arange(seq, dtype=F32)[:, None] / seq
    bands = jnp.linspace(1e-4, FILTER_BANDS - 1, FILTER_BANDS, dtype=F32)
    emb = jnp.concatenate([t, jnp.cos(bands * wpos), -jnp.sin(bands * wpos)], axis=-1)
    emb = jnp.pad(emb, ((0, 0), (0, EMB_PAD - FILTER_EMB)))
    emb_b = jnp.concatenate([emb[:1], emb[:0:-1]], axis=0)
    return jnp.stack([emb, emb_b])


def _dft_tables(seq):
    idx = jnp.arange(seq, dtype=jnp.int32)
    phase = (idx[:, None] * idx[None, :]) & (2 * seq - 1)
    ang = phase.astype(F32) * (math.pi / seq)
    alt = (1 - 2 * (idx & 1)).astype(F32)
    cm = jnp.cos(ang)
    sn = jnp.sin(ang)
    sf = jnp.where(idx[:, None] == 0, alt[None, :], sn)
    si = jnp.where(idx[None, :] == 0, alt[:, None], sn)
    cms = alt[:, None] * cm
    sfs = alt[:, None] * sf
    return tuple(t.astype(BF16) for t in (cm, sf, si, cms, sfs))


def kernel(x, c, ctx, c_ctx, w_ada, b_ada, norm_gains, w_in, conv_w, conv_b, filt_w1, filt_b1, filt_w2, filt_b2, filt_w3, filt_b3, filt_w4, filt_freq, filt_bias, qk_gains, w_branch_a, w_branch_b, w_out, w_ff1, w_ff2):
    B, L, D = x.shape
    T = B * L
    C = HYENA_WIDTH
    lyr = 0
    gains = norm_gains[lyr]
    w_in_b = w_in[lyr].astype(BF16)

    cin = jnp.zeros((MOD_ROWS, D), F32).at[:B].set(c).at[B].set(c_ctx)
    mod = _adaln(cin, w_ada[lyr], b_ada[lyr][None])

    xf = x.reshape(T, D)

    cos, sin = _rope_tables(L)
    qg = qk_gains[lyr, 0][None]
    kg = qk_gains[lyr, 1][None]
    q_scale = HEAD_DIM ** -0.5 * math.log2(math.e)
    h, u, q, k_all, v_all = _inproj(xf, mod, gains[0:1], w_in_b, qg, kg, cos, sin, seq=L, kv_len=L + ctx.shape[1],
                                    q_scale=q_scale, tm=min(1024, L))
    k_all, v_all = _ctx_kv(ctx, mod, gains[0:1], w_in_b, kg, k_all, v_all, seq=L, mod_row=B)

    emb = _filter_embedding(L)
    w1p = jnp.pad(filt_w1[lyr], ((0, EMB_PAD - FILTER_EMB), (0, 0)))
    deltas = jnp.abs(jnp.linspace(MIN_DECAY, MAX_DECAY, C, dtype=F32))[None]
    hid = _filter_hidden(emb, w1p, filt_b1[lyr][None], filt_w2[lyr], filt_b2[lyr][None],
                         filt_w3[lyr], filt_b3[lyr][None], filt_freq[lyr][None])
    filt, norm = _hyena_filter(hid, emb, filt_w4[lyr], deltas)
    P = FFT_BLOCK
    cm, sf, si, cms, sfs = _dft_tables(P)
    gr, gi = _filter_spectrum(cm, sf, cms, sfs, filt.reshape(2 * L // P, P, C), norm)
    z, zb, x0 = _hyena_pre(u, conv_w[lyr], conv_b[lyr][None], filt_bias[lyr][None], seq=L)
    zr, zi = _dft_fwd(cm, sf, z.reshape(T // P, P, C))
    ya = _conv(cm, si, zr, zi, gr, gi, x0, zb, seq=L)

    yb = _attention(q.reshape(B, L, Q_W), k_all, v_all, tq=512, tk=512).reshape(T, Q_W)

    merged = _merge(h, ya, yb, w_in_b, w_branch_a[lyr].astype(BF16), w_branch_b[lyr].astype(BF16))
    x1, h2 = _outproj(merged, w_out[lyr].astype(BF16), xf, mod, gains, seq=L)
    out = _mlp(h2, w_ff1[lyr].astype(BF16), w_ff2[lyr].astype(BF16), x1, mod, gains, seq=L, tm=1024, tf=512)
    return out.reshape(B, L, D)
```

```python
import functools
import math

import jax
import jax.numpy as jnp
from jax import lax
from jax.experimental import pallas as pl
from jax.experimental.pallas import tpu as pltpu

F32 = jnp.float32
BF16 = jnp.bfloat16

D_MODEL = 2048
CTX_LEN = 256
GRID_W = 64
N_HEADS = 16
HEAD_DIM = 128
N_KV_HEADS = 4
GROUP = N_HEADS // N_KV_HEADS
AXIS_DIM = HEAD_DIM // 2
ROPE_THETA = 10000.0
HYENA_WIDTH = D_MODEL // 2
FILTER_HIDDEN = 64
FILTER_EMB = 17
FILTER_BANDS = (FILTER_EMB - 1) // 2
DECAY_TARGET = 1e-2
MIN_DECAY = math.log(DECAY_TARGET) / 1.5
MAX_DECAY = math.log(DECAY_TARGET) / 0.3
D_FF = 4 * D_MODEL
EPS = 1e-6
Q_W = N_HEADS * HEAD_DIM
KV_W = N_KV_HEADS * HEAD_DIM
Q_OFF = 3 * HYENA_WIDTH
K_OFF = Q_OFF + Q_W
V_OFF = K_OFF + KV_W
GA_OFF = V_OFF + KV_W
GB_OFF = GA_OFF + D_MODEL

LANES = 128
SUBLANES = 8
VMEM_LIMIT = 60 * 1024 * 1024
EMB_PAD = 128
MOD_ROWS = 8
HIGHEST = lax.Precision.HIGHEST
ONES_ROWS = 16
IN_TN = KV_W
PRE_SUB_COLS = 256
MERGE_TN = 256
MLP_TF = 512
QK_SUB_ROWS = 256
OUT_SUB_ROWS = 128
FFT_BLOCK = 512


def _params(*sem):
    return pltpu.CompilerParams(dimension_semantics=sem, vmem_limit_bytes=VMEM_LIMIT)


def _dot(a, b):
    return jnp.dot(a, b, preferred_element_type=F32)


def _rms(x, g):
    return x * lax.rsqrt(jnp.mean(x * x, axis=-1, keepdims=True) + EPS) * g


def _col_tiles(w, tn):
    k, n = w.shape
    return w.astype(BF16).reshape(k, n // tn, tn).transpose(1, 0, 2)


def _adaln_kernel(c_ref, w_ref, b_ref, o_ref):
    c = c_ref[...]
    s = c * jax.nn.sigmoid(c)
    o_ref[...] = _dot(s.astype(BF16), w_ref[...].astype(BF16)) + b_ref[...]


def _adaln(cin, w, b, tn=1024):
    rows, d = cin.shape
    n = w.shape[1]
    return pl.pallas_call(
        _adaln_kernel,
        grid=(n // tn,),
        in_specs=[pl.BlockSpec((rows, d), lambda j: (0, 0)),
                  pl.BlockSpec((d, tn), lambda j: (0, j)),
                  pl.BlockSpec((1, tn), lambda j: (0, j))],
        out_specs=pl.BlockSpec((rows, tn), lambda j: (0, j)),
        out_shape=jax.ShapeDtypeStruct((rows, n), F32),
        compiler_params=_params("arbitrary"),
        name="adaln",
    )(cin, w, b)


def _mod_row(i, tiles_per_mod, mod_base):
    if tiles_per_mod is None:
        return mod_base
    return mod_base + i // tiles_per_mod


def _modnorm(x, g, mod_ref, b):
    sh = mod_ref[pl.ds(b, 1), 0:D_MODEL]
    sc = mod_ref[pl.ds(b, 1), D_MODEL:2 * D_MODEL]
    return (_rms(x, g) * (1.0 + sc) + sh).astype(BF16)


def _qk_head_epilogue(acc, g, cos, sin, first, scale, store):
    for hh in range(acc.shape[1] // HEAD_DIM):
        y = _rms(acc[:, hh * HEAD_DIM:(hh + 1) * HEAD_DIM], g)
        if cos is not None:
            partner = jnp.where(first,
                                pltpu.roll(y, HEAD_DIM - AXIS_DIM // 2, 1),
                                pltpu.roll(y, AXIS_DIM // 2, 1))
            y = y * cos + partner * sin
        if scale != 1.0:
            y = y * scale
        store(hh, y.astype(BF16))


def _inproj_kernel(x_ref, mod_ref, g_ref, w_ref, qg_ref, kg_ref, cos_ref, sin_ref,
                   h_ref, u_ref, q_ref, k_ref, v_ref, *, tiles_per_mod, n_u, n_q, q_scale):
    i = pl.program_id(0)
    j = pl.program_id(1)
    tm = x_ref.shape[0]
    sub = min(tm, QK_SUB_ROWS)

    @pl.when(j == 0)
    def _():
        b = _mod_row(i, tiles_per_mod, 0)
        for r in range(0, tm, sub):
            hr = _modnorm(x_ref[r:r + sub, :], g_ref[...], mod_ref, b)
            h_ref[r:r + sub, :] = hr
            u_ref[r:r + sub, :] = _dot(hr, w_ref[0])

    @pl.when((j > 0) & (j < n_u))
    def _():
        u_ref[...] = _dot(h_ref[...], w_ref[0])

    def qk_tile(g_ref_, scale, store):
        lane = lax.broadcasted_iota(jnp.int32, (sub, HEAD_DIM), 1)
        first = (lane & (AXIS_DIM // 2)) == 0
        for r in range(0, tm, sub):
            acc = _dot(h_ref[r:r + sub, :], w_ref[0])
            _qk_head_epilogue(acc, g_ref_[...], cos_ref[r:r + sub, :], sin_ref[r:r + sub, :], first, scale,
                              functools.partial(store, r))

    @pl.when((j >= n_u) & (j < n_u + n_q))
    def _():
        def store(r, hh, y):
            q_ref[r:r + sub, hh * HEAD_DIM:(hh + 1) * HEAD_DIM] = y
        qk_tile(qg_ref, q_scale, store)

    @pl.when(j == n_u + n_q)
    def _():
        def store(r, hh, y):
            k_ref[0, r:r + sub, hh * HEAD_DIM:(hh + 1) * HEAD_DIM] = y
        qk_tile(kg_ref, 1.0, store)

    @pl.when(j == n_u + n_q + 1)
    def _():
        v_ref[0] = _dot(h_ref[...], w_ref[0]).astype(v_ref.dtype)


def _inproj(x, mod, g, w_in_t, qg, kg, cos, sin, *, seq, kv_len, q_scale, tm=1024):
    t, d = x.shape
    tn = w_in_t.shape[2]
    n_u, n_q = Q_OFF // tn, Q_W // tn
    seq_tiles = seq // tm
    row = pl.BlockSpec((tm, d), lambda i, j: (i, 0))
    gain = pl.BlockSpec((1, HEAD_DIM), lambda i, j: (0, 0))
    table = pl.BlockSpec((tm, HEAD_DIM), lambda i, j: (i % seq_tiles, 0))
    kv = pl.BlockSpec((1, tm, KV_W), lambda i, j: (i // seq_tiles, i % seq_tiles, 0))
    return pl.pallas_call(
        functools.partial(_inproj_kernel, tiles_per_mod=seq_tiles, n_u=n_u, n_q=n_q, q_scale=q_scale),
        grid=(t // tm, n_u + n_q + 2),
        in_specs=[row, pl.BlockSpec(mod.shape, lambda i, j: (0, 0)), pl.BlockSpec((1, d), lambda i, j: (0, 0)),
                  pl.BlockSpec((1, d, tn), lambda i, j: (j, 0, 0)), gain, gain, table, table],
        out_specs=[row,
                   pl.BlockSpec((tm, tn), lambda i, j: (i, jnp.minimum(j, n_u - 1))),
                   pl.BlockSpec((tm, tn), lambda i, j: (i, jnp.clip(j - n_u, 0, n_q - 1))),
                   kv, kv],
        out_shape=[jax.ShapeDtypeStruct((t, d), BF16),
                   jax.ShapeDtypeStruct((t, Q_OFF), F32),
                   jax.ShapeDtypeStruct((t, Q_W), BF16),
                   jax.ShapeDtypeStruct((t // seq, kv_len, KV_W), BF16),
                   jax.ShapeDtypeStruct((t // seq, kv_len, KV_W), BF16)],
        compiler_params=_params("arbitrary", "arbitrary"),
        name="inproj",
    )(x, mod, g, w_in_t, qg, kg, cos, sin)


def _ctx_kv_kernel(x_ref, mod_ref, g_ref, wk_ref, wv_ref, kg_ref, kin_ref, vin_ref, k_ref, v_ref, *, mod_row):
    del kin_ref, vin_ref
    hc = _modnorm(x_ref[0], g_ref[...], mod_ref, mod_row)

    def store(hh, y):
        k_ref[0, :, hh * HEAD_DIM:(hh + 1) * HEAD_DIM] = y
    _qk_head_epilogue(_dot(hc, wk_ref[0]), kg_ref[...], None, None, None, 1.0, store)
    v_ref[0] = _dot(hc, wv_ref[0]).astype(v_ref.dtype)


def _ctx_kv(ctx, mod, g, w_in_t, kg, k_all, v_all, *, seq, mod_row):
    nb, lc, d = ctx.shape
    anyspace = pl.BlockSpec(memory_space=pl.ANY)
    out = pl.BlockSpec((1, lc, KV_W), lambda b: (b, seq // lc, 0))
    return pl.pallas_call(
        functools.partial(_ctx_kv_kernel, mod_row=mod_row),
        grid=(nb,),
        in_specs=[pl.BlockSpec((1, lc, d), lambda b: (b, 0, 0)),
                  pl.BlockSpec(mod.shape, lambda b: (0, 0)), pl.BlockSpec((1, d), lambda b: (0, 0)),
                  pl.BlockSpec((1, d, KV_W), lambda b: (K_OFF // KV_W, 0, 0)),
                  pl.BlockSpec((1, d, KV_W), lambda b: (V_OFF // KV_W, 0, 0)),
                  pl.BlockSpec((1, HEAD_DIM), lambda b: (0, 0)), anyspace, anyspace],
        out_specs=[out, out],
        out_shape=[jax.ShapeDtypeStruct(k_all.shape, k_all.dtype), jax.ShapeDtypeStruct(v_all.shape, v_all.dtype)],
        input_output_aliases={6: 0, 7: 1},
        compiler_params=_params("arbitrary"),
        name="ctx_kv",
    )(ctx, mod, g, w_in_t, w_in_t, kg, k_all, v_all)


def _dot_hi(a, b):
    return jnp.dot(a, b, precision=HIGHEST, preferred_element_type=F32)


def _filter_hidden_kernel(emb_ref, w1_ref, b1_ref, w2_ref, b2_ref, w3_ref, b3_ref, fr_ref, o_ref):
    fr = fr_ref[...]
    h = jnp.sin(fr * (_dot_hi(emb_ref[0], w1_ref[...]) + b1_ref[...]))
    h = jnp.sin(fr * (_dot_hi(h, w2_ref[...]) + b2_ref[...]))
    o_ref[0] = jnp.sin(fr * (_dot_hi(h, w3_ref[...]) + b3_ref[...]))


def _filter_hidden(emb, w1, b1, w2, b2, w3, b3, fr, *, tr=256):
    sides, L, e = emb.shape
    full = lambda a: pl.BlockSpec(a.shape, lambda s, r: (0,) * a.ndim)
    return pl.pallas_call(
        _filter_hidden_kernel,
        grid=(sides, L // tr),
        in_specs=[pl.BlockSpec((1, tr, e), lambda s, r: (s, r, 0)),
                  full(w1), full(b1), full(w2), full(b2), full(w3), full(b3), full(fr)],
        out_specs=pl.BlockSpec((1, tr, FILTER_HIDDEN), lambda s, r: (s, r, 0)),
        out_shape=jax.ShapeDtypeStruct((sides, L, FILTER_HIDDEN), F32),
        compiler_params=_params("arbitrary", "arbitrary"),
        name="filter_hidden",
    )(emb, w1, b1, w2, b2, w3, b3, fr)


def _filter_kernel(hid_ref, emb_ref, w4f_ref, w4b_ref, dl_ref, k_ref, norm_ref):
    r = pl.program_id(1)
    dl = dl_ref[...]
    kf = _dot_hi(hid_ref[0], w4f_ref[...]) * jnp.exp(-emb_ref[0, :, 0:1] * dl)
    kb = _dot_hi(hid_ref[1], w4b_ref[...]) * jnp.exp(-emb_ref[1, :, 0:1] * dl)
    tr = kb.shape[0]
    row = r * tr + lax.broadcasted_iota(jnp.int32, kb.shape, 0)
    kb = jnp.where(row == 0, 0.0, kb)
    k_ref[0] = kb.astype(k_ref.dtype)
    k_ref[1] = kf.astype(k_ref.dtype)

    @pl.when(r == 0)
    def _():
        norm_ref[...] = jnp.zeros(norm_ref.shape, F32)

    norm_ref[...] += jnp.sum(jnp.abs(kf), axis=0, keepdims=True) + jnp.sum(jnp.abs(kb), axis=0, keepdims=True)


def _hyena_filter(hid, emb, w4, deltas, *, tr=512, tc=256):
    _, L, e = emb.shape
    C = deltas.shape[1]
    return pl.pallas_call(
        _filter_kernel,
        grid=(C // tc, L // tr),
        in_specs=[pl.BlockSpec((2, tr, FILTER_HIDDEN), lambda c, r: (0, r, 0)),
                  pl.BlockSpec((2, tr, e), lambda c, r: (0, r, 0)),
                  pl.BlockSpec((FILTER_HIDDEN, tc), lambda c, r: (0, c)),
                  pl.BlockSpec((FILTER_HIDDEN, tc), lambda c, r: (0, c + C // tc)),
                  pl.BlockSpec((1, tc), lambda c, r: (0, c))],
        out_specs=[pl.BlockSpec((2, tr, tc), lambda c, r: (0, r, c)),
                   pl.BlockSpec((1, tc), lambda c, r: (0, c))],
        out_shape=[jax.ShapeDtypeStruct((2, L, C), BF16), jax.ShapeDtypeStruct((1, C), F32)],
        compiler_params=_params("arbitrary", "arbitrary"),
        name="hyena_filter",
    )(hid, emb, w4, w4, deltas)


def _hyena_pre_kernel(x0m, x0p, x0n, x1m, x1p, x1n, vm, vp, vn, w0, w1, w2, b0, b1, b2, fb, cm_ref, sf_ref,
                      zr_ref, zi_ref, zb_ref, x0_ref, *, tiles_per_seq):
    pos = pl.program_id(0) % tiles_per_seq
    first = pos == 0
    last = pos == tiles_per_seq - 1
    tr, tc = x0m.shape
    sub = min(tc, PRE_SUB_COLS)
    row = lax.broadcasted_iota(jnp.int32, (tr, sub), 0)

    def conv(m_ref, p_ref, n_ref, w_ref, b_ref, cols):
        u = m_ref[:, cols]
        prev = jnp.where(first, 0.0, p_ref[SUBLANES - 1:SUBLANES, cols])
        nxt = jnp.where(last, 0.0, n_ref[0:1, cols])
        um = jnp.where(row == 0, prev, pltpu.roll(u, 1, 0))
        up = jnp.where(row == tr - 1, nxt, pltpu.roll(u, tr - 1, 0))
        w = w_ref[:, cols]
        return um * w[0:1] + u * w[1:2] + up * w[2:3] + b_ref[:, cols]

    for c in range(0, tc, sub):
        cols = slice(c, c + sub)
        z = conv(x1m, x1p, x1n, w1, b1, cols) * conv(vm, vp, vn, w2, b2, cols)
        zb16 = z.astype(BF16)
        zr_ref[0, :, cols] = _dot(cm_ref[...], zb16)
        zi_ref[0, :, cols] = _dot(sf_ref[...], zb16)
        zb_ref[:, cols] = z * fb[:, cols]
        x0_ref[:, cols] = conv(x0m, x0p, x0n, w0, b0, cols)


def _hyena_pre(u, conv_w, conv_b, fbias, cm, sf, *, seq, tc=512):
    t = u.shape[0]
    C = HYENA_WIDTH
    tr = cm.shape[0]
    cb = C // tc
    rb = tr // SUBLANES
    last_rb = t // SUBLANES - 1
    in_specs, args = [], []
    for part in range(3):
        off = part * cb
        in_specs += [pl.BlockSpec((tr, tc), lambda i, j, off=off: (i, j + off)),
                     pl.BlockSpec((SUBLANES, tc), lambda i, j, off=off: (jnp.maximum(i * rb - 1, 0), j + off)),
                     pl.BlockSpec((SUBLANES, tc), lambda i, j, off=off: (jnp.minimum((i + 1) * rb, last_rb), j + off))]
        args += [u, u, u]
    for part in range(3):
        in_specs.append(pl.BlockSpec((3, tc), lambda i, j, off=part * cb: (0, j + off)))
        args.append(conv_w)
    for part in range(3):
        in_specs.append(pl.BlockSpec((1, tc), lambda i, j, off=part * cb: (0, j + off)))
        args.append(conv_b)
    in_specs.append(pl.BlockSpec((1, tc), lambda i, j: (0, j)))
    args.append(fbias)
    in_specs += [pl.BlockSpec((tr, tr), lambda i, j: (0, 0))] * 2
    args += [cm, sf]
    out_spec = pl.BlockSpec((tr, tc), lambda i, j: (i, j))
    spec_spec = pl.BlockSpec((1, tr, tc), lambda i, j: (i, 0, j))
    return pl.pallas_call(
        functools.partial(_hyena_pre_kernel, tiles_per_seq=seq // tr),
        grid=(t // tr, cb),
        in_specs=in_specs,
        out_specs=[spec_spec, spec_spec, out_spec, out_spec],
        out_shape=[jax.ShapeDtypeStruct((t // tr, tr, C), F32),
                   jax.ShapeDtypeStruct((t // tr, tr, C), F32),
                   jax.ShapeDtypeStruct((t, C), F32),
                   jax.ShapeDtypeStruct((t, C), F32)],
        compiler_params=_params("arbitrary", "arbitrary"),
        name="hyena_pre",
    )(*args)


def _filter_spectrum_kernel(cm_ref, sf_ref, cms_ref, sfs_ref, hi_ref, lo_ref, norm_ref, gr_ref, gi_ref):
    hi = hi_ref[0]
    lo = lo_ref[0]
    gr = _dot(cm_ref[...], hi) + _dot(cms_ref[...], lo)
    gi = _dot(sf_ref[...], hi) + _dot(sfs_ref[...], lo)
    p = gr.shape[0]
    dc = lax.broadcasted_iota(jnp.int32, gr.shape, 0) == 0
    scale = jnp.where(dc, 0.5 / p, 1.0 / p) / norm_ref[...]
    gr_ref[0] = gr * scale
    gi_ref[0] = gi * scale


def _filter_spectrum(cm, sf, cms, sfs, taps, norm, *, tc=1024):
    nseg, P, C = taps.shape
    mat = pl.BlockSpec((P, P), lambda d, c: (0, 0))
    out = pl.BlockSpec((1, P, tc), lambda d, c: (d, 0, c))
    return pl.pallas_call(
        _filter_spectrum_kernel,
        grid=(nseg - 1, C // tc),
        in_specs=[mat, mat, mat, mat,
                  pl.BlockSpec((1, P, tc), lambda d, c: (d + 1, 0, c)),
                  pl.BlockSpec((1, P, tc), lambda d, c: (d, 0, c)),
                  pl.BlockSpec((1, tc), lambda d, c: (0, c))],
        out_specs=[out, out],
        out_shape=[jax.ShapeDtypeStruct((nseg - 1, P, C), F32)] * 2,
        compiler_params=_params("arbitrary", "arbitrary"),
        name="dft_filter",
    )(cm, sf, cms, sfs, taps, taps, norm)


def _conv_kernel(cm_ref, si_ref, zr_ref, zi_ref, gr_ref, gi_ref, x0_ref, zb_ref, o_ref, yr_sc, yi_sc, *, rows):
    nblk, P, tc = zr_ref.shape
    dc = lax.broadcasted_iota(jnp.int32, (rows, tc), 0) == 0

    def out_block(i, carry):
        for r in range(0, P, rows):
            rr = ii = ri = ir = None
            for j in range(nblk):
                lag = i - j + (nblk - 1)
                gr = gr_ref[lag, r:r + rows, :]
                gi = gi_ref[lag, r:r + rows, :]
                zr = zr_ref[j, r:r + rows, :]
                zi = zi_ref[j, r:r + rows, :]
                if j == 0:
                    rr, ii, ri, ir = gr * zr, gi * zi, gr * zi, gi * zr
                else:
                    rr, ii, ri, ir = rr + gr * zr, ii + gi * zi, ri + gr * zi, ir + gi * zr
            if r == 0:
                yr = jnp.where(dc, rr, rr - ii)
                yi = jnp.where(dc, ii, ri + ir)
            else:
                yr = rr - ii
                yi = ri + ir
            yr_sc[r:r + rows, :] = yr.astype(yr_sc.dtype)
            yi_sc[r:r + rows, :] = yi.astype(yi_sc.dtype)
        y = _dot(cm_ref[...], yr_sc[...]) + _dot(si_ref[...], yi_sc[...])
        t0 = pl.multiple_of(i * P, P)
        o_ref[pl.ds(t0, P), :] = (x0_ref[pl.ds(t0, P), :] * (y + zb_ref[pl.ds(t0, P), :])).astype(o_ref.dtype)
        return carry

    lax.fori_loop(0, nblk, out_block, 0)


def _conv(cm, si, zr, zi, gr, gi, x0, zb, *, seq, tc=128, rows=64):
    P = cm.shape[0]
    nblk = seq // P
    nlag = gr.shape[0]
    t, C = x0.shape
    mat = pl.BlockSpec((P, P), lambda c, b: (0, 0))
    zspec = pl.BlockSpec((nblk, P, tc), lambda c, b: (b, 0, c))
    gspec = pl.BlockSpec((nlag, P, tc), lambda c, b: (0, 0, c))
    tile = pl.BlockSpec((seq, tc), lambda c, b: (b, c))
    return pl.pallas_call(
        functools.partial(_conv_kernel, rows=rows),
        grid=(C // tc, t // seq),
        in_specs=[mat, mat, zspec, zspec, gspec, gspec, tile, tile],
        out_specs=tile,
        out_shape=jax.ShapeDtypeStruct((t, C), BF16),
        scratch_shapes=[pltpu.VMEM((P, tc), BF16), pltpu.VMEM((P, tc), BF16)],
        compiler_params=_params("arbitrary", "arbitrary"),
        name="hyena_conv",
    )(cm, si, zr, zi, gr, gi, x0, zb)


def _attn_kernel(q_ref, k_ref, v_ref, o_ref, vt_sc, *, tq, chunks):
    @pl.when(pl.program_id(2) == 0)
    def _():
        vt_sc[0:HEAD_DIM, :] = v_ref[0].T
        vt_sc[HEAD_DIM:, :] = jnp.ones((vt_sc.shape[0] - HEAD_DIM, vt_sc.shape[1]), vt_sc.dtype)

    qs = jnp.concatenate([q_ref[0, :, g * HEAD_DIM:(g + 1) * HEAD_DIM] for g in range(GROUP)], axis=0)
    n = GROUP * tq
    m = jnp.full((1, n), -jnp.inf, F32)
    acc = jnp.zeros((vt_sc.shape[0], n), F32)

    def scores(c):
        start, size = chunks[c]
        return lax.dot_general(k_ref[0, start:start + size, :], qs, (((1,), (1,)), ((), ())),
                               preferred_element_type=F32)

    st_next = scores(0)
    for c, (start, size) in enumerate(chunks):
        st = st_next
        if c + 1 < len(chunks):
            st_next = scores(c + 1)
        m_new = jnp.maximum(m, jnp.max(st, axis=0, keepdims=True))
        alpha = jnp.exp2(m - m_new)
        p = jnp.exp2(st - m_new).astype(BF16)
        acc = alpha * acc + _dot(vt_sc[:, start:start + size], p)
        m = m_new
    out = acc[0:HEAD_DIM] * (1.0 / acc[HEAD_DIM:HEAD_DIM + 1])
    for g in range(GROUP):
        o_ref[0, :, g * HEAD_DIM:(g + 1) * HEAD_DIM] = out[:, g * tq:(g + 1) * tq].T.astype(o_ref.dtype)


def _attention(q, k, v, *, tq=256, tk=512):
    B, L, _ = q.shape
    S = k.shape[1]
    gw = GROUP * HEAD_DIM
    kv_spec = pl.BlockSpec((1, S, HEAD_DIM), lambda b, h, i: (b, 0, h))
    q_spec = pl.BlockSpec((1, tq, gw), lambda b, h, i: (b, i, h))
    chunks = tuple((c, min(tk, S - c)) for c in range(0, S, tk))
    return pl.pallas_call(
        functools.partial(_attn_kernel, tq=tq, chunks=chunks),
        grid=(B, N_KV_HEADS, L // tq),
        in_specs=[q_spec, kv_spec, kv_spec],
        out_specs=q_spec,
        out_shape=jax.ShapeDtypeStruct((B, L, Q_W), BF16),
        scratch_shapes=[pltpu.VMEM((HEAD_DIM + ONES_ROWS, S), BF16)],
        compiler_params=_params("arbitrary", "arbitrary", "arbitrary"),
        name="attention",
    )(q, k, v)


def _merge_kernel(h_ref, ya_ref, yb_ref, wga_ref, wgb_ref, wba_ref, wbb_ref, o_ref):
    h = h_ref[...]
    ga = jax.nn.sigmoid(_dot(h, wga_ref[0]))
    gb = jax.nn.sigmoid(_dot(h, wgb_ref[0]))
    a = _dot(ya_ref[...], wba_ref[0])
    b = _dot(yb_ref[...], wbb_ref[0])
    o_ref[...] = (ga * a + gb * b).astype(o_ref.dtype)


def _merge(h, ya, yb, wg, wba, wbb, *, tm=1024):
    t, d = h.shape
    tn = wg.shape[2]
    row = lambda w: pl.BlockSpec((tm, w), lambda i, j: (i, 0))
    col = lambda kdim, off=0: pl.BlockSpec((1, kdim, tn), lambda i, j: (j + off, 0, 0))
    return pl.pallas_call(
        _merge_kernel,
        grid=(t // tm, d // tn),
        in_specs=[row(d), row(ya.shape[1]), row(yb.shape[1]),
                  col(d), col(d, d // tn), col(ya.shape[1]), col(yb.shape[1])],
        out_specs=pl.BlockSpec((tm, tn), lambda i, j: (i, j)),
        out_shape=jax.ShapeDtypeStruct((t, d), BF16),
        compiler_params=_params("arbitrary", "arbitrary"),
        name="merge",
    )(h, ya, yb, wg, wg, wba, wbb)


def _outproj_kernel(m_ref, wo_ref, x_ref, mod_ref, g_ref, x1_ref, h2_ref, *, tiles_per_mod):
    b = _mod_row(pl.program_id(0), tiles_per_mod, 0)
    gt1 = mod_ref[pl.ds(b, 1), 2 * D_MODEL:3 * D_MODEL]
    sh2 = mod_ref[pl.ds(b, 1), 3 * D_MODEL:4 * D_MODEL]
    sc2 = mod_ref[pl.ds(b, 1), 4 * D_MODEL:5 * D_MODEL]
    tm = m_ref.shape[0]
    sub = min(tm, OUT_SUB_ROWS)
    for r in range(0, tm, sub):
        mix = _dot(m_ref[r:r + sub, :], wo_ref[...])
        x1 = x_ref[r:r + sub, :] + gt1 * _rms(mix, g_ref[1:2, :])
        x1_ref[r:r + sub, :] = x1
        h2_ref[r:r + sub, :] = (_rms(x1, g_ref[2:3, :]) * (1.0 + sc2) + sh2).astype(h2_ref.dtype)


def _outproj(merged, wo, x, mod, gains, *, seq, tm=512):
    t, d = x.shape
    row = pl.BlockSpec((tm, d), lambda i: (i, 0))
    return pl.pallas_call(
        functools.partial(_outproj_kernel, tiles_per_mod=seq // tm),
        grid=(t // tm,),
        in_specs=[row, pl.BlockSpec((d, d), lambda i: (0, 0)), row,
                  pl.BlockSpec(mod.shape, lambda i: (0, 0)),
                  pl.BlockSpec(gains.shape, lambda i: (0, 0))],
        out_specs=[row, row],
        out_shape=[jax.ShapeDtypeStruct((t, d), F32), jax.ShapeDtypeStruct((t, d), BF16)],
        compiler_params=_params("arbitrary"),
        name="outproj",
    )(merged, wo, x, mod, gains)


def _mlp_kernel(h2_ref, w1_ref, w2_ref, x1_ref, mod_ref, g_ref, o_ref, *, tiles_per_mod):
    j = pl.program_id(1)

    @pl.when(j == 0)
    def _():
        o_ref[...] = jnp.zeros(o_ref.shape, F32)

    last = pl.num_programs(1) - 1

    def partial_sum(rows):
        hid = jnp.maximum(_dot(h2_ref[rows, :], w1_ref[0]), 0.0)
        return o_ref[rows, :] + _dot((hid * hid).astype(BF16), w2_ref[...])

    @pl.when(j < last)
    def _():
        o_ref[...] = partial_sum(slice(None))

    @pl.when(j == last)
    def _():
        b = _mod_row(pl.program_id(0), tiles_per_mod, 0)
        gt2 = mod_ref[pl.ds(b, 1), 5 * D_MODEL:6 * D_MODEL]
        tm = o_ref.shape[0]
        sub = min(tm, OUT_SUB_ROWS)
        for r in range(0, tm, sub):
            rows = slice(r, r + sub)
            o_ref[rows, :] = x1_ref[rows, :] + gt2 * _rms(partial_sum(rows), g_ref[3:4, :])


def _mlp(h2, w1, w2, x1, mod, gains, *, seq, tm=1024):
    t, d = x1.shape
    f = w2.shape[0]
    tf = w1.shape[2]
    row = pl.BlockSpec((tm, d), lambda i, j: (i, 0))
    return pl.pallas_call(
        functools.partial(_mlp_kernel, tiles_per_mod=seq // tm),
        grid=(t // tm, f // tf),
        in_specs=[row, pl.BlockSpec((1, d, tf), lambda i, j: (j, 0, 0)), pl.BlockSpec((tf, d), lambda i, j: (j, 0)),
                  row, pl.BlockSpec(mod.shape, lambda i, j: (0, 0)),
                  pl.BlockSpec(gains.shape, lambda i, j: (0, 0))],
        out_specs=row,
        out_shape=jax.ShapeDtypeStruct((t, d), F32),
        compiler_params=_params("arbitrary", "arbitrary"),
        name="mlp",
    )(h2, w1, w2, x1, mod, gains)


def _rope_tables(seq):
    rows = jnp.repeat(jnp.arange(seq // GRID_W), GRID_W)
    cols = jnp.tile(jnp.arange(GRID_W), seq // GRID_W)
    inv = ROPE_THETA ** (-jnp.arange(0, AXIS_DIM, 2, dtype=F32) / AXIS_DIM)
    ar = rows[:, None] * inv
    ac = cols[:, None] * inv
    cos = jnp.concatenate([jnp.cos(ar), jnp.cos(ar), jnp.cos(ac), jnp.cos(ac)], axis=-1)
    sin = jnp.concatenate([-jnp.sin(ar), jnp.sin(ar), -jnp.sin(ac), jnp.sin(ac)], axis=-1)
    return cos, sin


def _filter_embedding(seq):
    t = jnp.linspace(0.0, 1.0, seq, dtype=F32)[:, None]
    wpos = 2.0 * math.pi * jnp.arange(seq, dtype=F32)[:, None] / seq
    bands = jnp.linspace(1e-4, FILTER_BANDS - 1, FILTER_BANDS, dtype=F32)
    emb = jnp.concatenate([t, jnp.cos(bands * wpos), -jnp.sin(bands * wpos)], axis=-1)
    emb = jnp.pad(emb, ((0, 0), (0, EMB_PAD - FILTER_EMB)))
    emb_b = jnp.concatenate([emb[:1], emb[:0:-1]], axis=0)
    return jnp.stack([emb, emb_b])


def _dft_tables(seq):
    idx = jnp.arange(seq, dtype=jnp.int32)
    phase = (idx[:, None] * idx[None, :]) & (2 * seq - 1)
    ang = phase.astype(F32) * (math.pi / seq)
    alt = (1 - 2 * (idx & 1)).astype(F32)
    cm = jnp.cos(ang)
    sn = jnp.sin(ang)
    sf = jnp.where(idx[:, None] == 0, alt[None, :], sn)
    si = jnp.where(idx[None, :] == 0, alt[:, None], sn)
    cms = alt[:, None] * cm
    sfs = alt[:, None] * sf
    return tuple(t.astype(BF16) for t in (cm, sf, si, cms, sfs))


def kernel(x, c, ctx, c_ctx, w_ada, b_ada, norm_gains, w_in, conv_w, conv_b, filt_w1, filt_b1, filt_w2, filt_b2, filt_w3, filt_b3, filt_w4, filt_freq, filt_bias, qk_gains, w_branch_a, w_branch_b, w_out, w_ff1, w_ff2):
    B, L, D = x.shape
    T = B * L
    C = HYENA_WIDTH
    lyr = 0
    gains = norm_gains[lyr]
    w_in_t = _col_tiles(w_in[lyr][:, :GA_OFF], IN_TN)
    w_gate_t = _col_tiles(w_in[lyr][:, GA_OFF:], MERGE_TN)

    cin = jnp.zeros((MOD_ROWS, D), F32).at[:B].set(c).at[B].set(c_ctx)
    mod = _adaln(cin, w_ada[lyr], b_ada[lyr][None])

    xf = x.reshape(T, D)

    cos, sin = _rope_tables(L)
    qg = qk_gains[lyr, 0][None]
    kg = qk_gains[lyr, 1][None]
    q_scale = HEAD_DIM ** -0.5 * math.log2(math.e)
    h, u, q, k_all, v_all = _inproj(xf, mod, gains[0:1], w_in_t, qg, kg, cos, sin, seq=L, kv_len=L + ctx.shape[1],
                                    q_scale=q_scale, tm=min(1024, L))
    k_all, v_all = _ctx_kv(ctx, mod, gains[0:1], w_in_t, kg, k_all, v_all, seq=L, mod_row=B)

    emb = _filter_embedding(L)
    w1p = jnp.pad(filt_w1[lyr], ((0, EMB_PAD - FILTER_EMB), (0, 0)))
    deltas = jnp.abs(jnp.linspace(MIN_DECAY, MAX_DECAY, C, dtype=F32))[None]
    hid = _filter_hidden(emb, w1p, filt_b1[lyr][None], filt_w2[lyr], filt_b2[lyr][None],
                         filt_w3[lyr], filt_b3[lyr][None], filt_freq[lyr][None])
    filt, norm = _hyena_filter(hid, emb, filt_w4[lyr], deltas)
    P = FFT_BLOCK
    cm, sf, si, cms, sfs = _dft_tables(P)
    gr, gi = _filter_spectrum(cm, sf, cms, sfs, filt.reshape(2 * L // P, P, C), norm)
    zr, zi, zb, x0 = _hyena_pre(u, conv_w[lyr], conv_b[lyr][None], filt_bias[lyr][None], cm, sf, seq=L)
    ya = _conv(cm, si, zr, zi, gr, gi, x0, zb, seq=L)

    yb = _attention(q.reshape(B, L, Q_W), k_all, v_all, tq=512, tk=512).reshape(T, Q_W)

    merged = _merge(h, ya, yb, w_gate_t, _col_tiles(w_branch_a[lyr], MERGE_TN), _col_tiles(w_branch_b[lyr], MERGE_TN))
    x1, h2 = _outproj(merged, w_out[lyr].astype(BF16), xf, mod, gains, seq=L)
    out = _mlp(h2, _col_tiles(w_ff1[lyr], MLP_TF), w_ff2[lyr].astype(BF16), x1, mod, gains, seq=L)
    return out.reshape(B, L, D)
```

```python
import functools
import math

import jax
import jax.numpy as jnp
from jax import lax
from jax.experimental import pallas as pl
from jax.experimental.pallas import tpu as pltpu

F32 = jnp.float32
BF16 = jnp.bfloat16

D_MODEL = 2048
CTX_LEN = 256
GRID_W = 64
N_HEADS = 16
HEAD_DIM = 128
N_KV_HEADS = 4
GROUP = N_HEADS // N_KV_HEADS
AXIS_DIM = HEAD_DIM // 2
ROPE_THETA = 10000.0
HYENA_WIDTH = D_MODEL // 2
FILTER_HIDDEN = 64
FILTER_EMB = 17
FILTER_BANDS = (FILTER_EMB - 1) // 2
DECAY_TARGET = 1e-2
MIN_DECAY = math.log(DECAY_TARGET) / 1.5
MAX_DECAY = math.log(DECAY_TARGET) / 0.3
D_FF = 4 * D_MODEL
EPS = 1e-6
Q_W = N_HEADS * HEAD_DIM
KV_W = N_KV_HEADS * HEAD_DIM
Q_OFF = 3 * HYENA_WIDTH
K_OFF = Q_OFF + Q_W
V_OFF = K_OFF + KV_W
GA_OFF = V_OFF + KV_W
GB_OFF = GA_OFF + D_MODEL

LANES = 128
SUBLANES = 8
VMEM_LIMIT = 60 * 1024 * 1024
EMB_PAD = 128
MOD_ROWS = 8
HIGHEST = lax.Precision.HIGHEST
ONES_ROWS = 16
IN_TN = KV_W
PRE_SUB_COLS = 256
MERGE_TN = 512
MLP_TF = 512
QK_SUB_ROWS = 256
OUT_SUB_ROWS = 128
FFT_BLOCK = 512


def _params(*sem):
    return pltpu.CompilerParams(dimension_semantics=sem, vmem_limit_bytes=VMEM_LIMIT)


def _dot(a, b):
    return jnp.dot(a, b, preferred_element_type=F32)


def _rms(x, g):
    return x * lax.rsqrt(jnp.mean(x * x, axis=-1, keepdims=True) + EPS) * g


def _adaln_kernel(c_ref, w_ref, b_ref, o_ref):
    c = c_ref[...]
    s = c * jax.nn.sigmoid(c)
    o_ref[...] = _dot(s.astype(BF16), w_ref[...].astype(BF16)) + b_ref[...]


def _adaln(cin, w, b, tn=1024):
    rows, d = cin.shape
    n = w.shape[1]
    return pl.pallas_call(
        _adaln_kernel,
        grid=(n // tn,),
        in_specs=[pl.BlockSpec((rows, d), lambda j: (0, 0)),
                  pl.BlockSpec((d, tn), lambda j: (0, j)),
                  pl.BlockSpec((1, tn), lambda j: (0, j))],
        out_specs=pl.BlockSpec((rows, tn), lambda j: (0, j)),
        out_shape=jax.ShapeDtypeStruct((rows, n), F32),
        compiler_params=_params("arbitrary"),
        name="adaln",
    )(cin, w, b)


def _mod_row(i, tiles_per_mod, mod_base):
    if tiles_per_mod is None:
        return mod_base
    return mod_base + i // tiles_per_mod


def _modnorm(x, g, mod_ref, b):
    sh = mod_ref[pl.ds(b, 1), 0:D_MODEL]
    sc = mod_ref[pl.ds(b, 1), D_MODEL:2 * D_MODEL]
    return (_rms(x, g) * (1.0 + sc) + sh).astype(BF16)


def _qk_head_epilogue(acc, g, cos, sin, first, scale, store):
    for hh in range(acc.shape[1] // HEAD_DIM):
        y = _rms(acc[:, hh * HEAD_DIM:(hh + 1) * HEAD_DIM], g)
        if cos is not None:
            partner = jnp.where(first,
                                pltpu.roll(y, HEAD_DIM - AXIS_DIM // 2, 1),
                                pltpu.roll(y, AXIS_DIM // 2, 1))
            y = y * cos + partner * sin
        if scale != 1.0:
            y = y * scale
        store(hh, y.astype(BF16))


def _inproj_kernel(x_ref, mod_ref, g_ref, w_ref, qg_ref, kg_ref, cos_ref, sin_ref,
                   h_ref, u_ref, q_ref, k_ref, v_ref, *, tiles_per_mod, q_scale):
    tm = x_ref.shape[0]
    sub = min(tm, QK_SUB_ROWS)
    tn = IN_TN
    b = _mod_row(pl.program_id(0), tiles_per_mod, 0)
    lane = lax.broadcasted_iota(jnp.int32, (sub, HEAD_DIM), 1)
    first = (lane & (AXIS_DIM // 2)) == 0

    def proj(hr, col):
        return _dot(hr, w_ref[:, col:col + tn])

    for r in range(0, tm, sub):
        rows = slice(r, r + sub)
        hr = _modnorm(x_ref[rows, :], g_ref[...], mod_ref, b)
        h_ref[rows, :] = hr
        for col in range(0, Q_OFF, tn):
            u_ref[rows, col:col + tn] = proj(hr, col)
        cos, sin = cos_ref[rows, :], sin_ref[rows, :]
        for col in range(0, Q_W, tn):
            def store_q(hh, y, col=col):
                q_ref[rows, col + hh * HEAD_DIM:col + (hh + 1) * HEAD_DIM] = y
            _qk_head_epilogue(proj(hr, Q_OFF + col), qg_ref[...], cos, sin, first, q_scale, store_q)

        def store_k(hh, y):
            k_ref[0, rows, hh * HEAD_DIM:(hh + 1) * HEAD_DIM] = y
        _qk_head_epilogue(proj(hr, K_OFF), kg_ref[...], cos, sin, first, 1.0, store_k)
        v_ref[0, rows, :] = proj(hr, V_OFF).astype(v_ref.dtype)


def _inproj(x, mod, g, w_in_b, qg, kg, cos, sin, *, seq, kv_len, q_scale, tm=512):
    t, d = x.shape
    seq_tiles = seq // tm
    row = lambda w: pl.BlockSpec((tm, w), lambda i: (i, 0))
    gain = pl.BlockSpec((1, HEAD_DIM), lambda i: (0, 0))
    table = pl.BlockSpec((tm, HEAD_DIM), lambda i: (i % seq_tiles, 0))
    kv = pl.BlockSpec((1, tm, KV_W), lambda i: (i // seq_tiles, i % seq_tiles, 0))
    return pl.pallas_call(
        functools.partial(_inproj_kernel, tiles_per_mod=seq_tiles, q_scale=q_scale),
        grid=(t // tm,),
        in_specs=[row(d), pl.BlockSpec(mod.shape, lambda i: (0, 0)), pl.BlockSpec((1, d), lambda i: (0, 0)),
                  pl.BlockSpec((d, GA_OFF), lambda i: (0, 0), pipeline_mode=pl.Buffered(1)),
                  gain, gain, table, table],
        out_specs=[row(d), row(Q_OFF), row(Q_W), kv, kv],
        out_shape=[jax.ShapeDtypeStruct((t, d), BF16),
                   jax.ShapeDtypeStruct((t, Q_OFF), F32),
                   jax.ShapeDtypeStruct((t, Q_W), BF16),
                   jax.ShapeDtypeStruct((t // seq, kv_len, KV_W), BF16),
                   jax.ShapeDtypeStruct((t // seq, kv_len, KV_W), BF16)],
        compiler_params=_params("arbitrary"),
        name="inproj",
    )(x, mod, g, w_in_b, qg, kg, cos, sin)


def _ctx_kv_kernel(x_ref, mod_ref, g_ref, wk_ref, wv_ref, kg_ref, kin_ref, vin_ref, k_ref, v_ref, *, mod_row):
    del kin_ref, vin_ref
    hc = _modnorm(x_ref[0], g_ref[...], mod_ref, mod_row)

    def store(hh, y):
        k_ref[0, :, hh * HEAD_DIM:(hh + 1) * HEAD_DIM] = y
    _qk_head_epilogue(_dot(hc, wk_ref[...]), kg_ref[...], None, None, None, 1.0, store)
    v_ref[0] = _dot(hc, wv_ref[...]).astype(v_ref.dtype)


def _ctx_kv(ctx, mod, g, w_in_b, kg, k_all, v_all, *, seq, mod_row):
    nb, lc, d = ctx.shape
    anyspace = pl.BlockSpec(memory_space=pl.ANY)
    out = pl.BlockSpec((1, lc, KV_W), lambda b: (b, seq // lc, 0))
    return pl.pallas_call(
        functools.partial(_ctx_kv_kernel, mod_row=mod_row),
        grid=(nb,),
        in_specs=[pl.BlockSpec((1, lc, d), lambda b: (b, 0, 0)),
                  pl.BlockSpec(mod.shape, lambda b: (0, 0)), pl.BlockSpec((1, d), lambda b: (0, 0)),
                  pl.BlockSpec((d, KV_W), lambda b: (0, K_OFF // KV_W)),
                  pl.BlockSpec((d, KV_W), lambda b: (0, V_OFF // KV_W)),
                  pl.BlockSpec((1, HEAD_DIM), lambda b: (0, 0)), anyspace, anyspace],
        out_specs=[out, out],
        out_shape=[jax.ShapeDtypeStruct(k_all.shape, k_all.dtype), jax.ShapeDtypeStruct(v_all.shape, v_all.dtype)],
        input_output_aliases={6: 0, 7: 1},
        compiler_params=_params("arbitrary"),
        name="ctx_kv",
    )(ctx, mod, g, w_in_b, w_in_b, kg, k_all, v_all)


def _dot_hi(a, b):
    return jnp.dot(a, b, precision=HIGHEST, preferred_element_type=F32)


def _filter_hidden_kernel(emb_ref, w1_ref, b1_ref, w2_ref, b2_ref, w3_ref, b3_ref, fr_ref, o_ref):
    fr = fr_ref[...]
    h = jnp.sin(fr * (_dot_hi(emb_ref[0], w1_ref[...]) + b1_ref[...]))
    h = jnp.sin(fr * (_dot_hi(h, w2_ref[...]) + b2_ref[...]))
    o_ref[0] = jnp.sin(fr * (_dot_hi(h, w3_ref[...]) + b3_ref[...]))


def _filter_hidden(emb, w1, b1, w2, b2, w3, b3, fr, *, tr=256):
    sides, L, e = emb.shape
    full = lambda a: pl.BlockSpec(a.shape, lambda s, r: (0,) * a.ndim)
    return pl.pallas_call(
        _filter_hidden_kernel,
        grid=(sides, L // tr),
        in_specs=[pl.BlockSpec((1, tr, e), lambda s, r: (s, r, 0)),
                  full(w1), full(b1), full(w2), full(b2), full(w3), full(b3), full(fr)],
        out_specs=pl.BlockSpec((1, tr, FILTER_HIDDEN), lambda s, r: (s, r, 0)),
        out_shape=jax.ShapeDtypeStruct((sides, L, FILTER_HIDDEN), F32),
        compiler_params=_params("arbitrary", "arbitrary"),
        name="filter_hidden",
    )(emb, w1, b1, w2, b2, w3, b3, fr)


def _filter_kernel(hid_ref, emb_ref, w4f_ref, w4b_ref, dl_ref, k_ref, norm_ref):
    r = pl.program_id(1)
    dl = dl_ref[...]
    kf = _dot_hi(hid_ref[0], w4f_ref[...]) * jnp.exp(-emb_ref[0, :, 0:1] * dl)
    kb = _dot_hi(hid_ref[1], w4b_ref[...]) * jnp.exp(-emb_ref[1, :, 0:1] * dl)
    tr = kb.shape[0]
    row = r * tr + lax.broadcasted_iota(jnp.int32, kb.shape, 0)
    kb = jnp.where(row == 0, 0.0, kb)
    k_ref[0] = kb.astype(k_ref.dtype)
    k_ref[1] = kf.astype(k_ref.dtype)

    @pl.when(r == 0)
    def _():
        norm_ref[...] = jnp.zeros(norm_ref.shape, F32)

    norm_ref[...] += jnp.sum(jnp.abs(kf), axis=0, keepdims=True) + jnp.sum(jnp.abs(kb), axis=0, keepdims=True)


def _hyena_filter(hid, emb, w4, deltas, *, tr=512, tc=256):
    _, L, e = emb.shape
    C = deltas.shape[1]
    return pl.pallas_call(
        _filter_kernel,
        grid=(C // tc, L // tr),
        in_specs=[pl.BlockSpec((2, tr, FILTER_HIDDEN), lambda c, r: (0, r, 0)),
                  pl.BlockSpec((2, tr, e), lambda c, r: (0, r, 0)),
                  pl.BlockSpec((FILTER_HIDDEN, tc), lambda c, r: (0, c)),
                  pl.BlockSpec((FILTER_HIDDEN, tc), lambda c, r: (0, c + C // tc)),
                  pl.BlockSpec((1, tc), lambda c, r: (0, c))],
        out_specs=[pl.BlockSpec((2, tr, tc), lambda c, r: (0, r, c)),
                   pl.BlockSpec((1, tc), lambda c, r: (0, c))],
        out_shape=[jax.ShapeDtypeStruct((2, L, C), BF16), jax.ShapeDtypeStruct((1, C), F32)],
        compiler_params=_params("arbitrary", "arbitrary"),
        name="hyena_filter",
    )(hid, emb, w4, w4, deltas)


def _hyena_pre_kernel(x0m, x0p, x0n, x1m, x1p, x1n, vm, vp, vn, w0, w1, w2, b0, b1, b2, fb, cm_ref, sf_ref,
                      zr_ref, zi_ref, zb_ref, x0_ref, *, tiles_per_seq):
    pos = pl.program_id(0) % tiles_per_seq
    first = pos == 0
    last = pos == tiles_per_seq - 1
    tr, tc = x0m.shape
    sub = min(tc, PRE_SUB_COLS)
    row = lax.broadcasted_iota(jnp.int32, (tr, sub), 0)

    def conv(m_ref, p_ref, n_ref, w_ref, b_ref, cols):
        u = m_ref[:, cols]
        prev = jnp.where(first, 0.0, p_ref[SUBLANES - 1:SUBLANES, cols])
        nxt = jnp.where(last, 0.0, n_ref[0:1, cols])
        um = jnp.where(row == 0, prev, pltpu.roll(u, 1, 0))
        up = jnp.where(row == tr - 1, nxt, pltpu.roll(u, tr - 1, 0))
        w = w_ref[:, cols]
        return um * w[0:1] + u * w[1:2] + up * w[2:3] + b_ref[:, cols]

    for c in range(0, tc, sub):
        cols = slice(c, c + sub)
        z = conv(x1m, x1p, x1n, w1, b1, cols) * conv(vm, vp, vn, w2, b2, cols)
        zb16 = z.astype(BF16)
        zr_ref[0, :, cols] = _dot(cm_ref[...], zb16)
        zi_ref[0, :, cols] = _dot(sf_ref[...], zb16)
        zb_ref[:, cols] = z * fb[:, cols]
        x0_ref[:, cols] = conv(x0m, x0p, x0n, w0, b0, cols)


def _hyena_pre(u, conv_w, conv_b, fbias, cm, sf, *, seq, tc=512):
    t = u.shape[0]
    C = HYENA_WIDTH
    tr = cm.shape[0]
    cb = C // tc
    rb = tr // SUBLANES
    last_rb = t // SUBLANES - 1
    in_specs, args = [], []
    for part in range(3):
        off = part * cb
        in_specs += [pl.BlockSpec((tr, tc), lambda i, j, off=off: (i, j + off)),
                     pl.BlockSpec((SUBLANES, tc), lambda i, j, off=off: (jnp.maximum(i * rb - 1, 0), j + off)),
                     pl.BlockSpec((SUBLANES, tc), lambda i, j, off=off: (jnp.minimum((i + 1) * rb, last_rb), j + off))]
        args += [u, u, u]
    for part in range(3):
        in_specs.append(pl.BlockSpec((3, tc), lambda i, j, off=part * cb: (0, j + off)))
        args.append(conv_w)
    for part in range(3):
        in_specs.append(pl.BlockSpec((1, tc), lambda i, j, off=part * cb: (0, j + off)))
        args.append(conv_b)
    in_specs.append(pl.BlockSpec((1, tc), lambda i, j: (0, j)))
    args.append(fbias)
    in_specs += [pl.BlockSpec((tr, tr), lambda i, j: (0, 0))] * 2
    args += [cm, sf]
    out_spec = pl.BlockSpec((tr, tc), lambda i, j: (i, j))
    spec_spec = pl.BlockSpec((1, tr, tc), lambda i, j: (i, 0, j))
    return pl.pallas_call(
        functools.partial(_hyena_pre_kernel, tiles_per_seq=seq // tr),
        grid=(t // tr, cb),
        in_specs=in_specs,
        out_specs=[spec_spec, spec_spec, out_spec, out_spec],
        out_shape=[jax.ShapeDtypeStruct((t // tr, tr, C), F32),
                   jax.ShapeDtypeStruct((t // tr, tr, C), F32),
                   jax.ShapeDtypeStruct((t, C), F32),
                   jax.ShapeDtypeStruct((t, C), F32)],
        compiler_params=_params("arbitrary", "arbitrary"),
        name="hyena_pre",
    )(*args)


def _filter_spectrum_kernel(cm_ref, sf_ref, cms_ref, sfs_ref, hi_ref, lo_ref, norm_ref, gr_ref, gi_ref):
    hi = hi_ref[0]
    lo = lo_ref[0]
    gr = _dot(cm_ref[...], hi) + _dot(cms_ref[...], lo)
    gi = _dot(sf_ref[...], hi) + _dot(sfs_ref[...], lo)
    p = gr.shape[0]
    dc = lax.broadcasted_iota(jnp.int32, gr.shape, 0) == 0
    scale = jnp.where(dc, 0.5 / p, 1.0 / p) / norm_ref[...]
    gr_ref[0] = gr * scale
    gi_ref[0] = gi * scale


def _filter_spectrum(cm, sf, cms, sfs, taps, norm, *, tc=1024):
    nseg, P, C = taps.shape
    mat = pl.BlockSpec((P, P), lambda d, c: (0, 0))
    out = pl.BlockSpec((1, P, tc), lambda d, c: (d, 0, c))
    return pl.pallas_call(
        _filter_spectrum_kernel,
        grid=(nseg - 1, C // tc),
        in_specs=[mat, mat, mat, mat,
                  pl.BlockSpec((1, P, tc), lambda d, c: (d + 1, 0, c)),
                  pl.BlockSpec((1, P, tc), lambda d, c: (d, 0, c)),
                  pl.BlockSpec((1, tc), lambda d, c: (0, c))],
        out_specs=[out, out],
        out_shape=[jax.ShapeDtypeStruct((nseg - 1, P, C), F32)] * 2,
        compiler_params=_params("arbitrary", "arbitrary"),
        name="dft_filter",
    )(cm, sf, cms, sfs, taps, taps, norm)


def _conv_kernel(cm_ref, si_ref, zr_ref, zi_ref, gr_ref, gi_ref, x0_ref, zb_ref, o_ref, yr_sc, yi_sc, *, rows):
    nblk, P, tc = zr_ref.shape
    dc = lax.broadcasted_iota(jnp.int32, (rows, tc), 0) == 0

    def out_block(i, carry):
        for r in range(0, P, rows):
            rr = ii = ri = ir = None
            for j in range(nblk):
                lag = i - j + (nblk - 1)
                gr = gr_ref[lag, r:r + rows, :]
                gi = gi_ref[lag, r:r + rows, :]
                zr = zr_ref[j, r:r + rows, :]
                zi = zi_ref[j, r:r + rows, :]
                if j == 0:
                    rr, ii, ri, ir = gr * zr, gi * zi, gr * zi, gi * zr
                else:
                    rr, ii, ri, ir = rr + gr * zr, ii + gi * zi, ri + gr * zi, ir + gi * zr
            if r == 0:
                yr = jnp.where(dc, rr, rr - ii)
                yi = jnp.where(dc, ii, ri + ir)
            else:
                yr = rr - ii
                yi = ri + ir
            yr_sc[r:r + rows, :] = yr.astype(yr_sc.dtype)
            yi_sc[r:r + rows, :] = yi.astype(yi_sc.dtype)
        y = _dot(cm_ref[...], yr_sc[...]) + _dot(si_ref[...], yi_sc[...])
        t0 = pl.multiple_of(i * P, P)
        o_ref[pl.ds(t0, P), :] = (x0_ref[pl.ds(t0, P), :] * (y + zb_ref[pl.ds(t0, P), :])).astype(o_ref.dtype)
        return carry

    lax.fori_loop(0, nblk, out_block, 0)


def _conv(cm, si, zr, zi, gr, gi, x0, zb, *, seq, tc=128, rows=64):
    P = cm.shape[0]
    nblk = seq // P
    nlag = gr.shape[0]
    t, C = x0.shape
    mat = pl.BlockSpec((P, P), lambda c, b: (0, 0))
    zspec = pl.BlockSpec((nblk, P, tc), lambda c, b: (b, 0, c))
    gspec = pl.BlockSpec((nlag, P, tc), lambda c, b: (0, 0, c))
    tile = pl.BlockSpec((seq, tc), lambda c, b: (b, c))
    return pl.pallas_call(
        functools.partial(_conv_kernel, rows=rows),
        grid=(C // tc, t // seq),
        in_specs=[mat, mat, zspec, zspec, gspec, gspec, tile, tile],
        out_specs=tile,
        out_shape=jax.ShapeDtypeStruct((t, C), BF16),
        scratch_shapes=[pltpu.VMEM((P, tc), BF16), pltpu.VMEM((P, tc), BF16)],
        compiler_params=_params("arbitrary", "arbitrary"),
        name="hyena_conv",
    )(cm, si, zr, zi, gr, gi, x0, zb)


def _attn_kernel(q_ref, k_ref, v_ref, o_ref, vt_sc, *, tq, chunks):
    @pl.when(pl.program_id(2) == 0)
    def _():
        vt_sc[0:HEAD_DIM, :] = v_ref[0].T
        vt_sc[HEAD_DIM:, :] = jnp.ones((vt_sc.shape[0] - HEAD_DIM, vt_sc.shape[1]), vt_sc.dtype)

    qs = jnp.concatenate([q_ref[0, :, g * HEAD_DIM:(g + 1) * HEAD_DIM] for g in range(GROUP)], axis=0)
    n = GROUP * tq
    m = jnp.full((1, n), -jnp.inf, F32)
    acc = jnp.zeros((vt_sc.shape[0], n), F32)

    def scores(c):
        start, size = chunks[c]
        return lax.dot_general(k_ref[0, start:start + size, :], qs, (((1,), (1,)), ((), ())),
                               preferred_element_type=F32)

    st_next = scores(0)
    for c, (start, size) in enumerate(chunks):
        st = st_next
        if c + 1 < len(chunks):
            st_next = scores(c + 1)
        m_new = jnp.maximum(m, jnp.max(st, axis=0, keepdims=True))
        alpha = jnp.exp2(m - m_new)
        p = jnp.exp2(st - m_new).astype(BF16)
        acc = alpha * acc + _dot(vt_sc[:, start:start + size], p)
        m = m_new
    out = acc[0:HEAD_DIM] * (1.0 / acc[HEAD_DIM:HEAD_DIM + 1])
    for g in range(GROUP):
        o_ref[0, :, g * HEAD_DIM:(g + 1) * HEAD_DIM] = out[:, g * tq:(g + 1) * tq].T.astype(o_ref.dtype)


def _attention(q, k, v, *, tq=256, tk=512):
    B, L, _ = q.shape
    S = k.shape[1]
    gw = GROUP * HEAD_DIM
    kv_spec = pl.BlockSpec((1, S, HEAD_DIM), lambda b, h, i: (b, 0, h))
    q_spec = pl.BlockSpec((1, tq, gw), lambda b, h, i: (b, i, h))
    chunks = tuple((c, min(tk, S - c)) for c in range(0, S, tk))
    return pl.pallas_call(
        functools.partial(_attn_kernel, tq=tq, chunks=chunks),
        grid=(B, N_KV_HEADS, L // tq),
        in_specs=[q_spec, kv_spec, kv_spec],
        out_specs=q_spec,
        out_shape=jax.ShapeDtypeStruct((B, L, Q_W), BF16),
        scratch_shapes=[pltpu.VMEM((HEAD_DIM + ONES_ROWS, S), BF16)],
        compiler_params=_params("arbitrary", "arbitrary", "arbitrary"),
        name="attention",
    )(q, k, v)


def _merge_kernel(h_ref, ya_ref, yb_ref, wga_ref, wgb_ref, wba_ref, wbb_ref, o_ref):
    h = h_ref[...]
    ya = ya_ref[...]
    yb = yb_ref[...]
    tn = MERGE_TN
    for col in range(0, o_ref.shape[1], tn):
        cols = slice(col, col + tn)
        ga = jax.nn.sigmoid(_dot(h, wga_ref[:, cols]))
        gb = jax.nn.sigmoid(_dot(h, wgb_ref[:, cols]))
        o_ref[:, cols] = (ga * _dot(ya, wba_ref[:, cols]) + gb * _dot(yb, wbb_ref[:, cols])).astype(o_ref.dtype)


def _merge(h, ya, yb, w_in_b, wba, wbb, *, tm=512):
    t, d = h.shape
    row = lambda w: pl.BlockSpec((tm, w), lambda i: (i, 0))
    resident = lambda kdim, blk=0: pl.BlockSpec((kdim, d), lambda i: (0, blk), pipeline_mode=pl.Buffered(1))
    return pl.pallas_call(
        _merge_kernel,
        grid=(t // tm,),
        in_specs=[row(d), row(ya.shape[1]), row(yb.shape[1]),
                  resident(d, GA_OFF // d), resident(d, GB_OFF // d), resident(ya.shape[1]), resident(yb.shape[1])],
        out_specs=row(d),
        out_shape=jax.ShapeDtypeStruct((t, d), BF16),
        compiler_params=_params("arbitrary"),
        name="merge",
    )(h, ya, yb, w_in_b, w_in_b, wba, wbb)


def _outproj_kernel(m_ref, wo_ref, x_ref, mod_ref, g_ref, x1_ref, h2_ref, *, tiles_per_mod):
    b = _mod_row(pl.program_id(0), tiles_per_mod, 0)
    gt1 = mod_ref[pl.ds(b, 1), 2 * D_MODEL:3 * D_MODEL]
    sh2 = mod_ref[pl.ds(b, 1), 3 * D_MODEL:4 * D_MODEL]
    sc2 = mod_ref[pl.ds(b, 1), 4 * D_MODEL:5 * D_MODEL]
    tm = m_ref.shape[0]
    sub = min(tm, OUT_SUB_ROWS)
    for r in range(0, tm, sub):
        mix = _dot(m_ref[r:r + sub, :], wo_ref[...])
        x1 = x_ref[r:r + sub, :] + gt1 * _rms(mix, g_ref[1:2, :])
        x1_ref[r:r + sub, :] = x1
        h2_ref[r:r + sub, :] = (_rms(x1, g_ref[2:3, :]) * (1.0 + sc2) + sh2).astype(h2_ref.dtype)


def _outproj(merged, wo, x, mod, gains, *, seq, tm=512):
    t, d = x.shape
    row = pl.BlockSpec((tm, d), lambda i: (i, 0))
    return pl.pallas_call(
        functools.partial(_outproj_kernel, tiles_per_mod=seq // tm),
        grid=(t // tm,),
        in_specs=[row, pl.BlockSpec((d, d), lambda i: (0, 0)), row,
                  pl.BlockSpec(mod.shape, lambda i: (0, 0)),
                  pl.BlockSpec(gains.shape, lambda i: (0, 0))],
        out_specs=[row, row],
        out_shape=[jax.ShapeDtypeStruct((t, d), F32), jax.ShapeDtypeStruct((t, d), BF16)],
        compiler_params=_params("arbitrary"),
        name="outproj",
    )(merged, wo, x, mod, gains)


def _mlp_kernel(h2_ref, w1_ref, w2_ref, x1_ref, mod_ref, g_ref, o_ref, *, tiles_per_mod):
    j = pl.program_id(1)

    @pl.when(j == 0)
    def _():
        o_ref[...] = jnp.zeros(o_ref.shape, F32)

    last = pl.num_programs(1) - 1

    def partial_sum(rows):
        hid = jnp.maximum(_dot(h2_ref[rows, :], w1_ref[...]), 0.0)
        return o_ref[rows, :] + _dot((hid * hid).astype(BF16), w2_ref[...])

    @pl.when(j < last)
    def _():
        o_ref[...] = partial_sum(slice(None))

    @pl.when(j == last)
    def _():
        b = _mod_row(pl.program_id(0), tiles_per_mod, 0)
        gt2 = mod_ref[pl.ds(b, 1), 5 * D_MODEL:6 * D_MODEL]
        tm = o_ref.shape[0]
        sub = min(tm, OUT_SUB_ROWS)
        for r in range(0, tm, sub):
            rows = slice(r, r + sub)
            o_ref[rows, :] = x1_ref[rows, :] + gt2 * _rms(partial_sum(rows), g_ref[3:4, :])


def _mlp(h2, w1, w2, x1, mod, gains, *, seq, tm=1024, tf=MLP_TF):
    t, d = x1.shape
    f = w1.shape[1]
    row = pl.BlockSpec((tm, d), lambda i, j: (i, 0))
    return pl.pallas_call(
        functools.partial(_mlp_kernel, tiles_per_mod=seq // tm),
        grid=(t // tm, f // tf),
        in_specs=[row, pl.BlockSpec((d, tf), lambda i, j: (0, j)), pl.BlockSpec((tf, d), lambda i, j: (j, 0)),
                  row, pl.BlockSpec(mod.shape, lambda i, j: (0, 0)),
                  pl.BlockSpec(gains.shape, lambda i, j: (0, 0))],
        out_specs=row,
        out_shape=jax.ShapeDtypeStruct((t, d), F32),
        compiler_params=_params("arbitrary", "arbitrary"),
        name="mlp",
    )(h2, w1, w2, x1, mod, gains)


def _rope_tables(seq):
    rows = jnp.repeat(jnp.arange(seq // GRID_W), GRID_W)
    cols = jnp.tile(jnp.arange(GRID_W), seq // GRID_W)
    inv = ROPE_THETA ** (-jnp.arange(0, AXIS_DIM, 2, dtype=F32) / AXIS_DIM)
    ar = rows[:, None] * inv
    ac = cols[:, None] * inv
    cos = jnp.concatenate([jnp.cos(ar), jnp.cos(ar), jnp.cos(ac), jnp.cos(ac)], axis=-1)
    sin = jnp.concatenate([-jnp.sin(ar), jnp.sin(ar), -jnp.sin(ac), jnp.sin(ac)], axis=-1)
    return cos, sin


def _filter_embedding(seq):
    t = jnp.linspace(0.0, 1.0, seq, dtype=F32)[:, None]
    wpos = 2.0 * math.pi * jnp.arange(seq, dtype=F32)[:, None] / seq
    bands = jnp.linspace(1e-4, FILTER_BANDS - 1, FILTER_BANDS, dtype=F32)
    emb = jnp.concatenate([t, jnp.cos(bands * wpos), -jnp.sin(bands * wpos)], axis=-1)
    emb = jnp.pad(emb, ((0, 0), (0, EMB_PAD - FILTER_EMB)))
    emb_b = jnp.concatenate([emb[:1], emb[:0:-1]], axis=0)
    return jnp.stack([emb, emb_b])


def _dft_tables(seq):
    idx = jnp.arange(seq, dtype=jnp.int32)
    phase = (idx[:, None] * idx[None, :]) & (2 * seq - 1)
    ang = phase.astype(F32) * (math.pi / seq)
    alt = (1 - 2 * (idx & 1)).astype(F32)
    cm = jnp.cos(ang)
    sn = jnp.sin(ang)
    sf = jnp.where(idx[:, None] == 0, alt[None, :], sn)
    si = jnp.where(idx[None, :] == 0, alt[:, None], sn)
    cms = alt[:, None] * cm
    sfs = alt[:, None] * sf
    return tuple(t.astype(BF16) for t in (cm, sf, si, cms, sfs))


def kernel(x, c, ctx, c_ctx, w_ada, b_ada, norm_gains, w_in, conv_w, conv_b, filt_w1, filt_b1, filt_w2, filt_b2, filt_w3, filt_b3, filt_w4, filt_freq, filt_bias, qk_gains, w_branch_a, w_branch_b, w_out, w_ff1, w_ff2):
    B, L, D = x.shape
    T = B * L
    C = HYENA_WIDTH
    lyr = 0
    gains = norm_gains[lyr]
    w_in_b = w_in[lyr].astype(BF16)

    cin = jnp.zeros((MOD_ROWS, D), F32).at[:B].set(c).at[B].set(c_ctx)
    mod = _adaln(cin, w_ada[lyr], b_ada[lyr][None])

    xf = x.reshape(T, D)

    cos, sin = _rope_tables(L)
    qg = qk_gains[lyr, 0][None]
    kg = qk_gains[lyr, 1][None]
    q_scale = HEAD_DIM ** -0.5 * math.log2(math.e)
    h, u, q, k_all, v_all = _inproj(xf, mod, gains[0:1], w_in_b, qg, kg, cos, sin, seq=L, kv_len=L + ctx.shape[1],
                                    q_scale=q_scale)
    k_all, v_all = _ctx_kv(ctx, mod, gains[0:1], w_in_b, kg, k_all, v_all, seq=L, mod_row=B)

    emb = _filter_embedding(L)
    w1p = jnp.pad(filt_w1[lyr], ((0, EMB_PAD - FILTER_EMB), (0, 0)))
    deltas = jnp.abs(jnp.linspace(MIN_DECAY, MAX_DECAY, C, dtype=F32))[None]
    hid = _filter_hidden(emb, w1p, filt_b1[lyr][None], filt_w2[lyr], filt_b2[lyr][None],
                         filt_w3[lyr], filt_b3[lyr][None], filt_freq[lyr][None])
    filt, norm = _hyena_filter(hid, emb, filt_w4[lyr], deltas)
    P = FFT_BLOCK
    cm, sf, si, cms, sfs = _dft_tables(P)
    gr, gi = _filter_spectrum(cm, sf, cms, sfs, filt.reshape(2 * L // P, P, C), norm)
    zr, zi, zb, x0 = _hyena_pre(u, conv_w[lyr], conv_b[lyr][None], filt_bias[lyr][None], cm, sf, seq=L)
    ya = _conv(cm, si, zr, zi, gr, gi, x0, zb, seq=L)

    yb = _attention(q.reshape(B, L, Q_W), k_all, v_all, tq=512, tk=512).reshape(T, Q_W)

    merged = _merge(h, ya, yb, w_in_b, w_branch_a[lyr].astype(BF16), w_branch_b[lyr].astype(BF16))
    x1, h2 = _outproj(merged, w_out[lyr].astype(BF16), xf, mod, gains, seq=L)
    out = _mlp(h2, w_ff1[lyr].astype(BF16), w_ff2[lyr].astype(BF16), x1, mod, gains, seq=L)
    return out.reshape(B, L, D)
```

```python
import functools
import math

import jax
import jax.numpy as jnp
from jax import lax
from jax.experimental import pallas as pl
from jax.experimental.pallas import tpu as pltpu

F32 = jnp.float32
BF16 = jnp.bfloat16

D_MODEL = 2048
CTX_LEN = 256
GRID_W = 64
N_HEADS = 16
HEAD_DIM = 128
N_KV_HEADS = 4
GROUP = N_HEADS // N_KV_HEADS
AXIS_DIM = HEAD_DIM // 2
ROPE_THETA = 10000.0
HYENA_WIDTH = D_MODEL // 2
FILTER_HIDDEN = 64
FILTER_EMB = 17
FILTER_BANDS = (FILTER_EMB - 1) // 2
DECAY_TARGET = 1e-2
MIN_DECAY = math.log(DECAY_TARGET) / 1.5
MAX_DECAY = math.log(DECAY_TARGET) / 0.3
D_FF = 4 * D_MODEL
EPS = 1e-6
Q_W = N_HEADS * HEAD_DIM
KV_W = N_KV_HEADS * HEAD_DIM
Q_OFF = 3 * HYENA_WIDTH
K_OFF = Q_OFF + Q_W
V_OFF = K_OFF + KV_W
GA_OFF = V_OFF + KV_W
GB_OFF = GA_OFF + D_MODEL

LANES = 128
SUBLANES = 8
VMEM_LIMIT = 60 * 1024 * 1024
EMB_PAD = 128
MOD_ROWS = 8
HIGHEST = lax.Precision.HIGHEST
ONES_ROWS = 16
IN_TN = KV_W
PRE_SUB_COLS = 256
MERGE_TN = 512
MLP_TF = 512
QK_SUB_ROWS = 256
OUT_SUB_ROWS = 128
FFT_BLOCK = 512


def _params(*sem):
    return pltpu.CompilerParams(dimension_semantics=sem, vmem_limit_bytes=VMEM_LIMIT)


def _dot(a, b):
    return jnp.dot(a, b, preferred_element_type=F32)


def _rms(x, g):
    return x * lax.rsqrt(jnp.mean(x * x, axis=-1, keepdims=True) + EPS) * g


def _adaln_kernel(c_ref, w_ref, b_ref, o_ref):
    c = c_ref[...]
    s = c * jax.nn.sigmoid(c)
    o_ref[...] = _dot(s.astype(BF16), w_ref[...].astype(BF16)) + b_ref[...]


def _adaln(cin, w, b, tn=1024):
    rows, d = cin.shape
    n = w.shape[1]
    return pl.pallas_call(
        _adaln_kernel,
        grid=(n // tn,),
        in_specs=[pl.BlockSpec((rows, d), lambda j: (0, 0)),
                  pl.BlockSpec((d, tn), lambda j: (0, j)),
                  pl.BlockSpec((1, tn), lambda j: (0, j))],
        out_specs=pl.BlockSpec((rows, tn), lambda j: (0, j)),
        out_shape=jax.ShapeDtypeStruct((rows, n), F32),
        compiler_params=_params("arbitrary"),
        name="adaln",
    )(cin, w, b)


def _mod_row(i, tiles_per_mod, mod_base):
    if tiles_per_mod is None:
        return mod_base
    return mod_base + i // tiles_per_mod


def _modnorm(x, g, mod_ref, b):
    sh = mod_ref[pl.ds(b, 1), 0:D_MODEL]
    sc = mod_ref[pl.ds(b, 1), D_MODEL:2 * D_MODEL]
    return (_rms(x, g) * (1.0 + sc) + sh).astype(BF16)


def _qk_head_epilogue(acc, g, cos, sin, first, scale, store):
    for hh in range(acc.shape[1] // HEAD_DIM):
        y = _rms(acc[:, hh * HEAD_DIM:(hh + 1) * HEAD_DIM], g)
        if cos is not None:
            partner = jnp.where(first,
                                pltpu.roll(y, HEAD_DIM - AXIS_DIM // 2, 1),
                                pltpu.roll(y, AXIS_DIM // 2, 1))
            y = y * cos + partner * sin
        if scale != 1.0:
            y = y * scale
        store(hh, y.astype(BF16))


def _inproj_kernel(x_ref, mod_ref, g_ref, w_ref, qg_ref, kg_ref, cos_ref, sin_ref, kin_ref, vin_ref,
                   h_ref, u_ref, q_ref, k_ref, v_ref, *, tiles_per_mod, q_scale):
    del kin_ref, vin_ref
    tm = x_ref.shape[0]
    sub = min(tm, QK_SUB_ROWS)
    tn = IN_TN
    b = _mod_row(pl.program_id(0), tiles_per_mod, 0)
    lane = lax.broadcasted_iota(jnp.int32, (sub, HEAD_DIM), 1)
    first = (lane & (AXIS_DIM // 2)) == 0

    def proj(hr, col):
        return _dot(hr, w_ref[:, col:col + tn])

    for r in range(0, tm, sub):
        rows = slice(r, r + sub)
        hr = _modnorm(x_ref[rows, :], g_ref[...], mod_ref, b)
        h_ref[rows, :] = hr
        for col in range(0, Q_OFF, tn):
            u_ref[rows, col:col + tn] = proj(hr, col)
        cos, sin = cos_ref[rows, :], sin_ref[rows, :]
        for col in range(0, Q_W, tn):
            def store_q(hh, y, col=col):
                q_ref[rows, col + hh * HEAD_DIM:col + (hh + 1) * HEAD_DIM] = y
            _qk_head_epilogue(proj(hr, Q_OFF + col), qg_ref[...], cos, sin, first, q_scale, store_q)

        def store_k(hh, y):
            k_ref[0, rows, hh * HEAD_DIM:(hh + 1) * HEAD_DIM] = y
        _qk_head_epilogue(proj(hr, K_OFF), kg_ref[...], cos, sin, first, 1.0, store_k)
        v_ref[0, rows, :] = proj(hr, V_OFF).astype(v_ref.dtype)


def _inproj(x, mod, g, w_in_b, qg, kg, cos, sin, k_ctx, v_ctx, *, seq, q_scale, tm=512):
    t, d = x.shape
    seq_tiles = seq // tm
    anyspace = pl.BlockSpec(memory_space=pl.ANY)
    row = lambda w: pl.BlockSpec((tm, w), lambda i: (i, 0))
    gain = pl.BlockSpec((1, HEAD_DIM), lambda i: (0, 0))
    table = pl.BlockSpec((tm, HEAD_DIM), lambda i: (i % seq_tiles, 0))
    kv = pl.BlockSpec((1, tm, KV_W), lambda i: (i // seq_tiles, i % seq_tiles, 0))
    return pl.pallas_call(
        functools.partial(_inproj_kernel, tiles_per_mod=seq_tiles, q_scale=q_scale),
        grid=(t // tm,),
        in_specs=[row(d), pl.BlockSpec(mod.shape, lambda i: (0, 0)), pl.BlockSpec((1, d), lambda i: (0, 0)),
                  pl.BlockSpec((d, GA_OFF), lambda i: (0, 0), pipeline_mode=pl.Buffered(1)),
                  gain, gain, table, table, anyspace, anyspace],
        out_specs=[row(d), row(Q_OFF), row(Q_W), kv, kv],
        out_shape=[jax.ShapeDtypeStruct((t, d), BF16),
                   jax.ShapeDtypeStruct((t, Q_OFF), F32),
                   jax.ShapeDtypeStruct((t, Q_W), BF16),
                   jax.ShapeDtypeStruct(k_ctx.shape, k_ctx.dtype),
                   jax.ShapeDtypeStruct(v_ctx.shape, v_ctx.dtype)],
        input_output_aliases={8: 3, 9: 4},
        compiler_params=_params("arbitrary"),
        name="inproj",
    )(x, mod, g, w_in_b, qg, kg, cos, sin, k_ctx, v_ctx)


def _ctx_kv_kernel(x_ref, mod_ref, g_ref, wk_ref, wv_ref, kg_ref, k_ref, v_ref, *, seq, mod_row):
    k_ref[0, 0:seq, :] = jnp.zeros((seq, KV_W), k_ref.dtype)
    v_ref[0, 0:seq, :] = jnp.zeros((seq, KV_W), v_ref.dtype)
    hc = _modnorm(x_ref[0], g_ref[...], mod_ref, mod_row)

    def store(hh, y):
        k_ref[0, seq:, hh * HEAD_DIM:(hh + 1) * HEAD_DIM] = y
    _qk_head_epilogue(_dot(hc, wk_ref[...]), kg_ref[...], None, None, None, 1.0, store)
    v_ref[0, seq:, :] = _dot(hc, wv_ref[...]).astype(v_ref.dtype)


def _ctx_kv(ctx, mod, g, w_in_b, kg, *, seq, mod_row):
    nb, lc, d = ctx.shape
    out = pl.BlockSpec((1, seq + lc, KV_W), lambda b: (b, 0, 0))
    shape = jax.ShapeDtypeStruct((nb, seq + lc, KV_W), BF16)
    return pl.pallas_call(
        functools.partial(_ctx_kv_kernel, seq=seq, mod_row=mod_row),
        grid=(nb,),
        in_specs=[pl.BlockSpec((1, lc, d), lambda b: (b, 0, 0)),
                  pl.BlockSpec(mod.shape, lambda b: (0, 0)), pl.BlockSpec((1, d), lambda b: (0, 0)),
                  pl.BlockSpec((d, KV_W), lambda b: (0, K_OFF // KV_W)),
                  pl.BlockSpec((d, KV_W), lambda b: (0, V_OFF // KV_W)),
                  pl.BlockSpec((1, HEAD_DIM), lambda b: (0, 0))],
        out_specs=[out, out],
        out_shape=[shape, shape],
        compiler_params=_params("arbitrary"),
        name="ctx_kv",
    )(ctx, mod, g, w_in_b, w_in_b, kg)


def _dot_hi(a, b):
    return jnp.dot(a, b, precision=HIGHEST, preferred_element_type=F32)


def _filter_hidden_kernel(emb_ref, w1_ref, b1_ref, w2_ref, b2_ref, w3_ref, b3_ref, fr_ref, o_ref):
    fr = fr_ref[...]
    h = jnp.sin(fr * (_dot_hi(emb_ref[0], w1_ref[...]) + b1_ref[...]))
    h = jnp.sin(fr * (_dot_hi(h, w2_ref[...]) + b2_ref[...]))
    o_ref[0] = jnp.sin(fr * (_dot_hi(h, w3_ref[...]) + b3_ref[...]))


def _filter_hidden(emb, w1, b1, w2, b2, w3, b3, fr, *, tr=256):
    sides, rows, e = emb.shape
    width = w3.shape[1]
    full = lambda a: pl.BlockSpec(a.shape, lambda s, r: (0,) * a.ndim)
    return pl.pallas_call(
        _filter_hidden_kernel,
        grid=(sides, rows // tr),
        in_specs=[pl.BlockSpec((1, tr, e), lambda s, r: (s, r, 0)),
                  full(w1), full(b1), full(w2), full(b2), full(w3), full(b3), full(fr)],
        out_specs=pl.BlockSpec((1, tr, width), lambda s, r: (s, r, 0)),
        out_shape=jax.ShapeDtypeStruct((sides, rows, width), F32),
        compiler_params=_params("arbitrary", "arbitrary"),
        name="filter_hidden",
    )(emb, w1, b1, w2, b2, w3, b3, fr)


def _filter_kernel(hid_ref, emb_ref, w4f_ref, w4b_ref, dl_ref, k_ref, norm_ref):
    r = pl.program_id(1)
    dl = dl_ref[...]
    kf = _dot_hi(hid_ref[0], w4f_ref[0]) * jnp.exp(-emb_ref[0, :, 0:1] * dl)
    kb = _dot_hi(hid_ref[1], w4b_ref[0]) * jnp.exp(-emb_ref[1, :, 0:1] * dl)
    tr = kb.shape[0]
    row = r * tr + lax.broadcasted_iota(jnp.int32, kb.shape, 0)
    kb = jnp.where(row == 0, 0.0, kb)
    k_ref[0] = kb.astype(k_ref.dtype)
    k_ref[1] = kf.astype(k_ref.dtype)

    @pl.when(r == 0)
    def _():
        norm_ref[...] = jnp.zeros(norm_ref.shape, F32)

    norm_ref[...] += jnp.sum(jnp.abs(kf), axis=0, keepdims=True) + jnp.sum(jnp.abs(kb), axis=0, keepdims=True)


def _hyena_filter(hid, emb, w4, deltas, *, tr=512, tc=256):
    _, L, e = emb.shape
    C = deltas.shape[1]
    half_tiles = L // tr // 2
    width = hid.shape[2]
    return pl.pallas_call(
        _filter_kernel,
        grid=(C // tc, L // tr),
        in_specs=[pl.BlockSpec((2, tr, width), lambda c, r: (0, r % half_tiles, 0)),
                  pl.BlockSpec((2, tr, e), lambda c, r: (0, r, 0)),
                  pl.BlockSpec((1, width, tc), lambda c, r: (r // half_tiles, 0, c)),
                  pl.BlockSpec((1, width, tc), lambda c, r: (r // half_tiles, 0, c + C // tc)),
                  pl.BlockSpec((1, tc), lambda c, r: (0, c))],
        out_specs=[pl.BlockSpec((2, tr, tc), lambda c, r: (0, r, c)),
                   pl.BlockSpec((1, tc), lambda c, r: (0, c))],
        out_shape=[jax.ShapeDtypeStruct((2, L, C), BF16), jax.ShapeDtypeStruct((1, C), F32)],
        compiler_params=_params("arbitrary", "arbitrary"),
        name="hyena_filter",
    )(hid, emb, w4, w4, deltas)


def _hyena_pre_kernel(x0m, x0p, x0n, x1m, x1p, x1n, vm, vp, vn, w0, w1, w2, b0, b1, b2, fb, cm_ref, sf_ref,
                      zr_ref, zi_ref, zb_ref, x0_ref, *, tiles_per_seq):
    pos = pl.program_id(0) % tiles_per_seq
    first = pos == 0
    last = pos == tiles_per_seq - 1
    tr, tc = x0m.shape
    sub = min(tc, PRE_SUB_COLS)
    row = lax.broadcasted_iota(jnp.int32, (tr, sub), 0)

    def conv(m_ref, p_ref, n_ref, w_ref, b_ref, cols):
        u = m_ref[:, cols]
        prev = jnp.where(first, 0.0, p_ref[SUBLANES - 1:SUBLANES, cols])
        nxt = jnp.where(last, 0.0, n_ref[0:1, cols])
        um = jnp.where(row == 0, prev, pltpu.roll(u, 1, 0))
        up = jnp.where(row == tr - 1, nxt, pltpu.roll(u, tr - 1, 0))
        w = w_ref[:, cols]
        return um * w[0:1] + u * w[1:2] + up * w[2:3] + b_ref[:, cols]

    for c in range(0, tc, sub):
        cols = slice(c, c + sub)
        z = conv(x1m, x1p, x1n, w1, b1, cols) * conv(vm, vp, vn, w2, b2, cols)
        zb16 = z.astype(BF16)
        zr_ref[0, :, cols] = _dot(cm_ref[...], zb16).astype(zr_ref.dtype)
        zi_ref[0, :, cols] = _dot(sf_ref[...], zb16).astype(zi_ref.dtype)
        zb_ref[:, cols] = (z * fb[:, cols]).astype(zb_ref.dtype)
        x0_ref[:, cols] = conv(x0m, x0p, x0n, w0, b0, cols).astype(x0_ref.dtype)


def _hyena_pre(u, conv_w, conv_b, fbias, cm, sf, *, seq, tc=512):
    t = u.shape[0]
    C = HYENA_WIDTH
    tr = cm.shape[0]
    cb = C // tc
    rb = tr // SUBLANES
    last_rb = t // SUBLANES - 1
    in_specs, args = [], []
    for part in range(3):
        off = part * cb
        in_specs += [pl.BlockSpec((tr, tc), lambda i, j, off=off: (i, j + off)),
                     pl.BlockSpec((SUBLANES, tc), lambda i, j, off=off: (jnp.maximum(i * rb - 1, 0), j + off)),
                     pl.BlockSpec((SUBLANES, tc), lambda i, j, off=off: (jnp.minimum((i + 1) * rb, last_rb), j + off))]
        args += [u, u, u]
    for part in range(3):
        in_specs.append(pl.BlockSpec((3, tc), lambda i, j, off=part * cb: (0, j + off)))
        args.append(conv_w)
    for part in range(3):
        in_specs.append(pl.BlockSpec((1, tc), lambda i, j, off=part * cb: (0, j + off)))
        args.append(conv_b)
    in_specs.append(pl.BlockSpec((1, tc), lambda i, j: (0, j)))
    args.append(fbias)
    in_specs += [pl.BlockSpec((tr, tr), lambda i, j: (0, 0))] * 2
    args += [cm, sf]
    out_spec = pl.BlockSpec((tr, tc), lambda i, j: (i, j))
    spec_spec = pl.BlockSpec((1, tr, tc), lambda i, j: (i, 0, j))
    return pl.pallas_call(
        functools.partial(_hyena_pre_kernel, tiles_per_seq=seq // tr),
        grid=(t // tr, cb),
        in_specs=in_specs,
        out_specs=[spec_spec, spec_spec, out_spec, out_spec],
        out_shape=[jax.ShapeDtypeStruct((t // tr, tr, C), BF16),
                   jax.ShapeDtypeStruct((t // tr, tr, C), BF16),
                   jax.ShapeDtypeStruct((t, C), BF16),
                   jax.ShapeDtypeStruct((t, C), BF16)],
        compiler_params=_params("arbitrary", "arbitrary"),
        name="hyena_pre",
    )(*args)


def _filter_spectrum_kernel(cm_ref, sf_ref, cms_ref, sfs_ref, hi_ref, lo_ref, norm_ref, gr_ref, gi_ref):
    hi = hi_ref[0]
    lo = lo_ref[0]
    gr = _dot(cm_ref[...], hi) + _dot(cms_ref[...], lo)
    gi = _dot(sf_ref[...], hi) + _dot(sfs_ref[...], lo)
    p = gr.shape[0]
    dc = lax.broadcasted_iota(jnp.int32, gr.shape, 0) == 0
    scale = jnp.where(dc, 0.5 / p, 1.0 / p) / norm_ref[...]
    gr_ref[0] = gr * scale
    gi_ref[0] = gi * scale


def _filter_spectrum(cm, sf, cms, sfs, taps, norm, *, tc=1024):
    nseg, P, C = taps.shape
    mat = pl.BlockSpec((P, P), lambda d, c: (0, 0))
    out = pl.BlockSpec((1, P, tc), lambda d, c: (d, 0, c))
    return pl.pallas_call(
        _filter_spectrum_kernel,
        grid=(nseg - 1, C // tc),
        in_specs=[mat, mat, mat, mat,
                  pl.BlockSpec((1, P, tc), lambda d, c: (d + 1, 0, c)),
                  pl.BlockSpec((1, P, tc), lambda d, c: (d, 0, c)),
                  pl.BlockSpec((1, tc), lambda d, c: (0, c))],
        out_specs=[out, out],
        out_shape=[jax.ShapeDtypeStruct((nseg - 1, P, C), F32)] * 2,
        compiler_params=_params("arbitrary", "arbitrary"),
        name="dft_filter",
    )(cm, sf, cms, sfs, taps, taps, norm)


def _conv_kernel(cm_ref, si_ref, zr_ref, zi_ref, gr_ref, gi_ref, x0_ref, zb_ref, o_ref, yr_sc, yi_sc, *, rows):
    nblk, P, tc = zr_ref.shape
    dc = lax.broadcasted_iota(jnp.int32, (rows, tc), 0) == 0

    def out_block(i, carry):
        for r in range(0, P, rows):
            rr = ii = ri = ir = None
            for j in range(nblk):
                lag = i - j + (nblk - 1)
                gr = gr_ref[lag, r:r + rows, :]
                gi = gi_ref[lag, r:r + rows, :]
                zr = zr_ref[j, r:r + rows, :].astype(F32)
                zi = zi_ref[j, r:r + rows, :].astype(F32)
                if j == 0:
                    rr, ii, ri, ir = gr * zr, gi * zi, gr * zi, gi * zr
                else:
                    rr, ii, ri, ir = rr + gr * zr, ii + gi * zi, ri + gr * zi, ir + gi * zr
            if r == 0:
                yr = jnp.where(dc, rr, rr - ii)
                yi = jnp.where(dc, ii, ri + ir)
            else:
                yr = rr - ii
                yi = ri + ir
            yr_sc[r:r + rows, :] = yr.astype(yr_sc.dtype)
            yi_sc[r:r + rows, :] = yi.astype(yi_sc.dtype)
        y = _dot(cm_ref[...], yr_sc[...]) + _dot(si_ref[...], yi_sc[...])
        t0 = pl.multiple_of(i * P, P)
        x0 = x0_ref[pl.ds(t0, P), :].astype(F32)
        o_ref[pl.ds(t0, P), :] = (x0 * (y + zb_ref[pl.ds(t0, P), :].astype(F32))).astype(o_ref.dtype)
        return carry

    lax.fori_loop(0, nblk, out_block, 0)


def _conv(cm, si, zr, zi, gr, gi, x0, zb, *, seq, tc=128, rows=64):
    P = cm.shape[0]
    nblk = seq // P
    nlag = gr.shape[0]
    t, C = x0.shape
    mat = pl.BlockSpec((P, P), lambda c, b: (0, 0))
    zspec = pl.BlockSpec((nblk, P, tc), lambda c, b: (b, 0, c))
    gspec = pl.BlockSpec((nlag, P, tc), lambda c, b: (0, 0, c))
    tile = pl.BlockSpec((seq, tc), lambda c, b: (b, c))
    return pl.pallas_call(
        functools.partial(_conv_kernel, rows=rows),
        grid=(C // tc, t // seq),
        in_specs=[mat, mat, zspec, zspec, gspec, gspec, tile, tile],
        out_specs=tile,
        out_shape=jax.ShapeDtypeStruct((t, C), BF16),
        scratch_shapes=[pltpu.VMEM((P, tc), BF16), pltpu.VMEM((P, tc), BF16)],
        compiler_params=_params("arbitrary", "arbitrary"),
        name="hyena_conv",
    )(cm, si, zr, zi, gr, gi, x0, zb)


def _attn_kernel(q_ref, k_ref, v_ref, o_ref, vt_sc, *, tq, chunks):
    @pl.when(pl.program_id(2) == 0)
    def _():
        vt_sc[0:HEAD_DIM, :] = v_ref[0].T
        vt_sc[HEAD_DIM:, :] = jnp.ones((vt_sc.shape[0] - HEAD_DIM, vt_sc.shape[1]), vt_sc.dtype)

    qs = jnp.concatenate([q_ref[0, :, g * HEAD_DIM:(g + 1) * HEAD_DIM] for g in range(GROUP)], axis=0)
    n = GROUP * tq
    m = jnp.full((1, n), -jnp.inf, F32)
    acc = jnp.zeros((vt_sc.shape[0], n), F32)

    def scores(c):
        start, size = chunks[c]
        return lax.dot_general(k_ref[0, start:start + size, :], qs, (((1,), (1,)), ((), ())),
                               preferred_element_type=F32)

    st_next = scores(0)
    for c, (start, size) in enumerate(chunks):
        st = st_next
        if c + 1 < len(chunks):
            st_next = scores(c + 1)
        m_new = jnp.maximum(m, jnp.max(st, axis=0, keepdims=True))
        alpha = jnp.exp2(m - m_new)
        p = jnp.exp2(st - m_new).astype(BF16)
        acc = alpha * acc + _dot(vt_sc[:, start:start + size], p)
        m = m_new
    out = acc[0:HEAD_DIM] * (1.0 / acc[HEAD_DIM:HEAD_DIM + 1])
    for g in range(GROUP):
        o_ref[0, :, g * HEAD_DIM:(g + 1) * HEAD_DIM] = out[:, g * tq:(g + 1) * tq].T.astype(o_ref.dtype)


def _attention(q, k, v, *, tq=256, tk=512):
    B, L, _ = q.shape
    S = k.shape[1]
    gw = GROUP * HEAD_DIM
    kv_spec = pl.BlockSpec((1, S, HEAD_DIM), lambda b, h, i: (b, 0, h))
    q_spec = pl.BlockSpec((1, tq, gw), lambda b, h, i: (b, i, h))
    chunks = tuple((c, min(tk, S - c)) for c in range(0, S, tk))
    return pl.pallas_call(
        functools.partial(_attn_kernel, tq=tq, chunks=chunks),
        grid=(B, N_KV_HEADS, L // tq),
        in_specs=[q_spec, kv_spec, kv_spec],
        out_specs=q_spec,
        out_shape=jax.ShapeDtypeStruct((B, L, Q_W), BF16),
        scratch_shapes=[pltpu.VMEM((HEAD_DIM + ONES_ROWS, S), BF16)],
        compiler_params=_params("arbitrary", "arbitrary", "arbitrary"),
        name="attention",
    )(q, k, v)


def _merge_kernel(h_ref, ya_ref, yb_ref, wga_ref, wgb_ref, wba_ref, wbb_ref, o_ref):
    h = h_ref[...]
    ya = ya_ref[...]
    yb = yb_ref[...]
    tn = MERGE_TN
    for col in range(0, o_ref.shape[1], tn):
        cols = slice(col, col + tn)
        ga = jax.nn.sigmoid(_dot(h, wga_ref[:, cols]))
        gb = jax.nn.sigmoid(_dot(h, wgb_ref[:, cols]))
        o_ref[:, cols] = (ga * _dot(ya, wba_ref[:, cols]) + gb * _dot(yb, wbb_ref[:, cols])).astype(o_ref.dtype)


def _merge(h, ya, yb, w_in_b, wba, wbb, *, tm=512):
    t, d = h.shape
    row = lambda w: pl.BlockSpec((tm, w), lambda i: (i, 0))
    resident = lambda kdim, blk=0: pl.BlockSpec((kdim, d), lambda i: (0, blk), pipeline_mode=pl.Buffered(1))
    return pl.pallas_call(
        _merge_kernel,
        grid=(t // tm,),
        in_specs=[row(d), row(ya.shape[1]), row(yb.shape[1]),
                  resident(d, GA_OFF // d), resident(d, GB_OFF // d), resident(ya.shape[1]), resident(yb.shape[1])],
        out_specs=row(d),
        out_shape=jax.ShapeDtypeStruct((t, d), BF16),
        compiler_params=_params("arbitrary"),
        name="merge",
    )(h, ya, yb, w_in_b, w_in_b, wba, wbb)


def _outproj_kernel(m_ref, wo_ref, x_ref, mod_ref, g_ref, x1_ref, h2_ref, *, tiles_per_mod):
    b = _mod_row(pl.program_id(0), tiles_per_mod, 0)
    gt1 = mod_ref[pl.ds(b, 1), 2 * D_MODEL:3 * D_MODEL]
    sh2 = mod_ref[pl.ds(b, 1), 3 * D_MODEL:4 * D_MODEL]
    sc2 = mod_ref[pl.ds(b, 1), 4 * D_MODEL:5 * D_MODEL]
    tm = m_ref.shape[0]
    sub = min(tm, OUT_SUB_ROWS)
    for r in range(0, tm, sub):
        mix = _dot(m_ref[r:r + sub, :], wo_ref[...])
        x1 = x_ref[r:r + sub, :] + gt1 * _rms(mix, g_ref[1:2, :])
        x1_ref[r:r + sub, :] = x1
        h2_ref[r:r + sub, :] = (_rms(x1, g_ref[2:3, :]) * (1.0 + sc2) + sh2).astype(h2_ref.dtype)


def _outproj(merged, wo, x, mod, gains, *, seq, tm=512):
    t, d = x.shape
    row = pl.BlockSpec((tm, d), lambda i: (i, 0))
    return pl.pallas_call(
        functools.partial(_outproj_kernel, tiles_per_mod=seq // tm),
        grid=(t // tm,),
        in_specs=[row, pl.BlockSpec((d, d), lambda i: (0, 0)), row,
                  pl.BlockSpec(mod.shape, lambda i: (0, 0)),
                  pl.BlockSpec(gains.shape, lambda i: (0, 0))],
        out_specs=[row, row],
        out_shape=[jax.ShapeDtypeStruct((t, d), F32), jax.ShapeDtypeStruct((t, d), BF16)],
        compiler_params=_params("arbitrary"),
        name="outproj",
    )(merged, wo, x, mod, gains)


def _mlp_kernel(h2_ref, w1_ref, w2_ref, x1_ref, mod_ref, g_ref, o_ref, *, tiles_per_mod):
    j = pl.program_id(1)

    @pl.when(j == 0)
    def _():
        o_ref[...] = jnp.zeros(o_ref.shape, F32)

    last = pl.num_programs(1) - 1

    def partial_sum(rows):
        hid = jnp.maximum(_dot(h2_ref[rows, :], w1_ref[...]), 0.0)
        return o_ref[rows, :] + _dot((hid * hid).astype(BF16), w2_ref[...])

    @pl.when(j < last)
    def _():
        o_ref[...] = partial_sum(slice(None))

    @pl.when(j == last)
    def _():
        b = _mod_row(pl.program_id(0), tiles_per_mod, 0)
        gt2 = mod_ref[pl.ds(b, 1), 5 * D_MODEL:6 * D_MODEL]
        tm = o_ref.shape[0]
        sub = min(tm, OUT_SUB_ROWS)
        for r in range(0, tm, sub):
            rows = slice(r, r + sub)
            o_ref[rows, :] = x1_ref[rows, :] + gt2 * _rms(partial_sum(rows), g_ref[3:4, :])


def _mlp(h2, w1, w2, x1, mod, gains, *, seq, tm=1024, tf=MLP_TF):
    t, d = x1.shape
    f = w1.shape[1]
    row = pl.BlockSpec((tm, d), lambda i, j: (i, 0))
    return pl.pallas_call(
        functools.partial(_mlp_kernel, tiles_per_mod=seq // tm),
        grid=(t // tm, f // tf),
        in_specs=[row, pl.BlockSpec((d, tf), lambda i, j: (0, j)), pl.BlockSpec((tf, d), lambda i, j: (j, 0)),
                  row, pl.BlockSpec(mod.shape, lambda i, j: (0, 0)),
                  pl.BlockSpec(gains.shape, lambda i, j: (0, 0))],
        out_specs=row,
        out_shape=jax.ShapeDtypeStruct((t, d), F32),
        compiler_params=_params("arbitrary", "arbitrary"),
        name="mlp",
    )(h2, w1, w2, x1, mod, gains)


def _rope_tables(seq):
    rows = jnp.repeat(jnp.arange(seq // GRID_W), GRID_W)
    cols = jnp.tile(jnp.arange(GRID_W), seq // GRID_W)
    inv = ROPE_THETA ** (-jnp.arange(0, AXIS_DIM, 2, dtype=F32) / AXIS_DIM)
    ar = rows[:, None] * inv
    ac = cols[:, None] * inv
    cos = jnp.concatenate([jnp.cos(ar), jnp.cos(ar), jnp.cos(ac), jnp.cos(ac)], axis=-1)
    sin = jnp.concatenate([-jnp.sin(ar), jnp.sin(ar), -jnp.sin(ac), jnp.sin(ac)], axis=-1)
    return cos, sin


def _filter_embedding(seq):
    t = jnp.linspace(0.0, 1.0, seq, dtype=F32)[:, None]
    wpos = 2.0 * math.pi * jnp.arange(seq, dtype=F32)[:, None] / seq
    bands = jnp.linspace(1e-4, FILTER_BANDS - 1, FILTER_BANDS, dtype=F32)
    emb = jnp.concatenate([t, jnp.cos(bands * wpos), -jnp.sin(bands * wpos)], axis=-1)
    emb = jnp.pad(emb, ((0, 0), (0, EMB_PAD - FILTER_EMB)))
    emb_b = jnp.concatenate([emb[:1], emb[:0:-1]], axis=0)
    return jnp.stack([emb, emb_b])


def _block_diag2(w):
    z = jnp.zeros_like(w)
    return jnp.concatenate([jnp.concatenate([w, z], axis=1), jnp.concatenate([z, w], axis=1)], axis=0)


def _dft_tables(seq):
    idx = jnp.arange(seq, dtype=jnp.int32)
    phase = (idx[:, None] * idx[None, :]) & (2 * seq - 1)
    ang = phase.astype(F32) * (math.pi / seq)
    alt = (1 - 2 * (idx & 1)).astype(F32)
    cm = jnp.cos(ang)
    sn = jnp.sin(ang)
    sf = jnp.where(idx[:, None] == 0, alt[None, :], sn)
    si = jnp.where(idx[None, :] == 0, alt[:, None], sn)
    cms = alt[:, None] * cm
    sfs = alt[:, None] * sf
    return tuple(t.astype(BF16) for t in (cm, sf, si, cms, sfs))


def kernel(x, c, ctx, c_ctx, w_ada, b_ada, norm_gains, w_in, conv_w, conv_b, filt_w1, filt_b1, filt_w2, filt_b2, filt_w3, filt_b3, filt_w4, filt_freq, filt_bias, qk_gains, w_branch_a, w_branch_b, w_out, w_ff1, w_ff2):
    B, L, D = x.shape
    T = B * L
    C = HYENA_WIDTH
    lyr = 0
    gains = norm_gains[lyr]
    w_in_b = w_in[lyr].astype(BF16)

    cin = jnp.zeros((MOD_ROWS, D), F32).at[:B].set(c).at[B].set(c_ctx)
    mod = _adaln(cin, w_ada[lyr], b_ada[lyr][None])

    xf = x.reshape(T, D)

    cos, sin = _rope_tables(L)
    qg = qk_gains[lyr, 0][None]
    kg = qk_gains[lyr, 1][None]
    q_scale = HEAD_DIM ** -0.5 * math.log2(math.e)
    k_ctx, v_ctx = _ctx_kv(ctx, mod, gains[0:1], w_in_b, kg, seq=L, mod_row=B)
    h, u, q, k_all, v_all = _inproj(xf, mod, gains[0:1], w_in_b, qg, kg, cos, sin, k_ctx, v_ctx, seq=L,
                                    q_scale=q_scale)

    emb = _filter_embedding(L)
    w1p = jnp.pad(filt_w1[lyr], ((0, EMB_PAD - FILTER_EMB), (0, 0)))
    deltas = jnp.abs(jnp.linspace(MIN_DECAY, MAX_DECAY, C, dtype=F32))[None]
    emb2 = jnp.concatenate([emb[:, :L // 2], emb[:, L // 2:]], axis=-1)
    pair = lambda v: jnp.tile(v, 2)[None]
    w4 = filt_w4[lyr]
    w4_halves = jnp.stack([jnp.concatenate([w4, jnp.zeros_like(w4)]), jnp.concatenate([jnp.zeros_like(w4), w4])])
    hid = _filter_hidden(emb2, _block_diag2(w1p), pair(filt_b1[lyr]), _block_diag2(filt_w2[lyr]), pair(filt_b2[lyr]),
                         _block_diag2(filt_w3[lyr]), pair(filt_b3[lyr]), pair(filt_freq[lyr]))
    filt, norm = _hyena_filter(hid, emb, w4_halves, deltas)
    P = FFT_BLOCK
    cm, sf, si, cms, sfs = _dft_tables(P)
    gr, gi = _filter_spectrum(cm, sf, cms, sfs, filt.reshape(2 * L // P, P, C), norm)
    zr, zi, zb, x0 = _hyena_pre(u, conv_w[lyr], conv_b[lyr][None], filt_bias[lyr][None], cm, sf, seq=L)
    ya = _conv(cm, si, zr, zi, gr, gi, x0, zb, seq=L)

    yb = _attention(q.reshape(B, L, Q_W), k_all, v_all, tq=512, tk=512).reshape(T, Q_W)

    merged = _merge(h, ya, yb, w_in_b, w_branch_a[lyr].astype(BF16), w_branch_b[lyr].astype(BF16))
    x1, h2 = _outproj(merged, w_out[lyr].astype(BF16), xf, mod, gains, seq=L)
    out = _mlp(h2, w_ff1[lyr].astype(BF16), w_ff2[lyr].astype(BF16), x1, mod, gains, seq=L)
    return out.reshape(B, L, D)
```

```python
import functools
import math

import jax
import jax.numpy as jnp
from jax import lax
from jax.experimental import pallas as pl
from jax.experimental.pallas import tpu as pltpu

F32 = jnp.float32
BF16 = jnp.bfloat16

D_MODEL = 2048
CTX_LEN = 256
GRID_W = 64
N_HEADS = 16
HEAD_DIM = 128
N_KV_HEADS = 4
GROUP = N_HEADS // N_KV_HEADS
AXIS_DIM = HEAD_DIM // 2
ROPE_THETA = 10000.0
HYENA_WIDTH = D_MODEL // 2
FILTER_HIDDEN = 64
FILTER_EMB = 17
FILTER_BANDS = (FILTER_EMB - 1) // 2
DECAY_TARGET = 1e-2
MIN_DECAY = math.log(DECAY_TARGET) / 1.5
MAX_DECAY = math.log(DECAY_TARGET) / 0.3
D_FF = 4 * D_MODEL
EPS = 1e-6
Q_W = N_HEADS * HEAD_DIM
KV_W = N_KV_HEADS * HEAD_DIM
Q_OFF = 3 * HYENA_WIDTH
K_OFF = Q_OFF + Q_W
V_OFF = K_OFF + KV_W
GA_OFF = V_OFF + KV_W
GB_OFF = GA_OFF + D_MODEL

LANES = 128
SUBLANES = 8
VMEM_LIMIT = 60 * 1024 * 1024
EMB_PAD = 128
MOD_ROWS = 8
HIGHEST = lax.Precision.HIGHEST
ONES_ROWS = 16
IN_TN = KV_W
PRE_SUB_COLS = 256
MERGE_TN = 512
MLP_TF = 512
QK_SUB_ROWS = 256
OUT_SUB_ROWS = 128
FFT_BLOCK = 512


def _params(*sem):
    return pltpu.CompilerParams(dimension_semantics=sem, vmem_limit_bytes=VMEM_LIMIT)


def _dot(a, b):
    return jnp.dot(a, b, preferred_element_type=F32)


def _rms(x, g):
    return x * lax.rsqrt(jnp.mean(x * x, axis=-1, keepdims=True) + EPS) * g


def _adaln_kernel(c_ref, w_ref, b_ref, o_ref):
    c = c_ref[...]
    s = c * jax.nn.sigmoid(c)
    o_ref[...] = _dot(s.astype(BF16), w_ref[...].astype(BF16)) + b_ref[...]


def _adaln(cin, w, b, tn=1024):
    rows, d = cin.shape
    n = w.shape[1]
    return pl.pallas_call(
        _adaln_kernel,
        grid=(n // tn,),
        in_specs=[pl.BlockSpec((rows, d), lambda j: (0, 0)),
                  pl.BlockSpec((d, tn), lambda j: (0, j)),
                  pl.BlockSpec((1, tn), lambda j: (0, j))],
        out_specs=pl.BlockSpec((rows, tn), lambda j: (0, j)),
        out_shape=jax.ShapeDtypeStruct((rows, n), F32),
        compiler_params=_params("arbitrary"),
        name="adaln",
    )(cin, w, b)


def _mod_row(i, tiles_per_mod, mod_base):
    if tiles_per_mod is None:
        return mod_base
    return mod_base + i // tiles_per_mod


def _modnorm(x, g, mod_ref, b):
    sh = mod_ref[pl.ds(b, 1), 0:D_MODEL]
    sc = mod_ref[pl.ds(b, 1), D_MODEL:2 * D_MODEL]
    return (_rms(x, g) * (1.0 + sc) + sh).astype(BF16)


def _qk_head_epilogue(acc, g, cos, sin, first, scale, store):
    for hh in range(acc.shape[1] // HEAD_DIM):
        y = _rms(acc[:, hh * HEAD_DIM:(hh + 1) * HEAD_DIM], g)
        if cos is not None:
            partner = jnp.where(first,
                                pltpu.roll(y, HEAD_DIM - AXIS_DIM // 2, 1),
                                pltpu.roll(y, AXIS_DIM // 2, 1))
            y = y * cos + partner * sin
        if scale != 1.0:
            y = y * scale
        store(hh, y.astype(BF16))


def _inproj_kernel(x_ref, mod_ref, g_ref, w_ref, qg_ref, kg_ref, cos_ref, sin_ref, kin_ref, vin_ref,
                   h_ref, u_ref, q_ref, k_ref, v_ref, *, tiles_per_mod, q_scale):
    del kin_ref, vin_ref
    tm = x_ref.shape[0]
    sub = min(tm, QK_SUB_ROWS)
    tn = IN_TN
    b = _mod_row(pl.program_id(0), tiles_per_mod, 0)
    lane = lax.broadcasted_iota(jnp.int32, (sub, HEAD_DIM), 1)
    first = (lane & (AXIS_DIM // 2)) == 0

    def proj(hr, col):
        return _dot(hr, w_ref[:, col:col + tn])

    for r in range(0, tm, sub):
        rows = slice(r, r + sub)
        hr = _modnorm(x_ref[rows, :], g_ref[...], mod_ref, b)
        h_ref[rows, :] = hr
        for col in range(0, Q_OFF, tn):
            u_ref[rows, col:col + tn] = proj(hr, col)
        cos, sin = cos_ref[rows, :], sin_ref[rows, :]
        for col in range(0, Q_W, tn):
            def store_q(hh, y, col=col):
                q_ref[rows, col + hh * HEAD_DIM:col + (hh + 1) * HEAD_DIM] = y
            _qk_head_epilogue(proj(hr, Q_OFF + col), qg_ref[...], cos, sin, first, q_scale, store_q)

        def store_k(hh, y):
            k_ref[0, rows, hh * HEAD_DIM:(hh + 1) * HEAD_DIM] = y
        _qk_head_epilogue(proj(hr, K_OFF), kg_ref[...], cos, sin, first, 1.0, store_k)
        v_ref[0, rows, :] = proj(hr, V_OFF).astype(v_ref.dtype)


def _inproj(x, mod, g, w_in_b, qg, kg, cos, sin, k_ctx, v_ctx, *, seq, q_scale, tm=512):
    t, d = x.shape
    seq_tiles = seq // tm
    anyspace = pl.BlockSpec(memory_space=pl.ANY)
    row = lambda w: pl.BlockSpec((tm, w), lambda i: (i, 0))
    gain = pl.BlockSpec((1, HEAD_DIM), lambda i: (0, 0))
    table = pl.BlockSpec((tm, HEAD_DIM), lambda i: (i % seq_tiles, 0))
    kv = pl.BlockSpec((1, tm, KV_W), lambda i: (i // seq_tiles, i % seq_tiles, 0))
    return pl.pallas_call(
        functools.partial(_inproj_kernel, tiles_per_mod=seq_tiles, q_scale=q_scale),
        grid=(t // tm,),
        in_specs=[row(d), pl.BlockSpec(mod.shape, lambda i: (0, 0)), pl.BlockSpec((1, d), lambda i: (0, 0)),
                  pl.BlockSpec((d, GA_OFF), lambda i: (0, 0), pipeline_mode=pl.Buffered(1)),
                  gain, gain, table, table, anyspace, anyspace],
        out_specs=[row(d), row(Q_OFF), row(Q_W), kv, kv],
        out_shape=[jax.ShapeDtypeStruct((t, d), BF16),
                   jax.ShapeDtypeStruct((t, Q_OFF), F32),
                   jax.ShapeDtypeStruct((t, Q_W), BF16),
                   jax.ShapeDtypeStruct(k_ctx.shape, k_ctx.dtype),
                   jax.ShapeDtypeStruct(v_ctx.shape, v_ctx.dtype)],
        input_output_aliases={8: 3, 9: 4},
        compiler_params=_params("arbitrary"),
        name="inproj",
    )(x, mod, g, w_in_b, qg, kg, cos, sin, k_ctx, v_ctx)


def _ctx_kv_kernel(x_ref, mod_ref, g_ref, wk_ref, wv_ref, kg_ref, k_ref, v_ref, *, seq, mod_row):
    k_ref[0, 0:seq, :] = jnp.zeros((seq, KV_W), k_ref.dtype)
    v_ref[0, 0:seq, :] = jnp.zeros((seq, KV_W), v_ref.dtype)
    hc = _modnorm(x_ref[0], g_ref[...], mod_ref, mod_row)

    def store(hh, y):
        k_ref[0, seq:, hh * HEAD_DIM:(hh + 1) * HEAD_DIM] = y
    _qk_head_epilogue(_dot(hc, wk_ref[...]), kg_ref[...], None, None, None, 1.0, store)
    v_ref[0, seq:, :] = _dot(hc, wv_ref[...]).astype(v_ref.dtype)


def _ctx_kv(ctx, mod, g, w_in_b, kg, *, seq, mod_row):
    nb, lc, d = ctx.shape
    out = pl.BlockSpec((1, seq + lc, KV_W), lambda b: (b, 0, 0))
    shape = jax.ShapeDtypeStruct((nb, seq + lc, KV_W), BF16)
    return pl.pallas_call(
        functools.partial(_ctx_kv_kernel, seq=seq, mod_row=mod_row),
        grid=(nb,),
        in_specs=[pl.BlockSpec((1, lc, d), lambda b: (b, 0, 0)),
                  pl.BlockSpec(mod.shape, lambda b: (0, 0)), pl.BlockSpec((1, d), lambda b: (0, 0)),
                  pl.BlockSpec((d, KV_W), lambda b: (0, K_OFF // KV_W)),
                  pl.BlockSpec((d, KV_W), lambda b: (0, V_OFF // KV_W)),
                  pl.BlockSpec((1, HEAD_DIM), lambda b: (0, 0))],
        out_specs=[out, out],
        out_shape=[shape, shape],
        compiler_params=_params("arbitrary"),
        name="ctx_kv",
    )(ctx, mod, g, w_in_b, w_in_b, kg)


def _dot_hi(a, b):
    return jnp.dot(a, b, precision=HIGHEST, preferred_element_type=F32)


def _filter_hidden_kernel(emb_ref, w1_ref, b1_ref, w2_ref, b2_ref, w3_ref, b3_ref, fr_ref, o_ref):
    fr = fr_ref[...]
    h = jnp.sin(fr * (_dot_hi(emb_ref[0], w1_ref[...]) + b1_ref[...]))
    h = jnp.sin(fr * (_dot_hi(h, w2_ref[...]) + b2_ref[...]))
    o_ref[0] = jnp.sin(fr * (_dot_hi(h, w3_ref[...]) + b3_ref[...]))


def _filter_hidden(emb, w1, b1, w2, b2, w3, b3, fr, *, tr=256):
    sides, rows, e = emb.shape
    width = w3.shape[1]
    full = lambda a: pl.BlockSpec(a.shape, lambda s, r: (0,) * a.ndim)
    return pl.pallas_call(
        _filter_hidden_kernel,
        grid=(sides, rows // tr),
        in_specs=[pl.BlockSpec((1, tr, e), lambda s, r: (s, r, 0)),
                  full(w1), full(b1), full(w2), full(b2), full(w3), full(b3), full(fr)],
        out_specs=pl.BlockSpec((1, tr, width), lambda s, r: (s, r, 0)),
        out_shape=jax.ShapeDtypeStruct((sides, rows, width), F32),
        compiler_params=_params("arbitrary", "arbitrary"),
        name="filter_hidden",
    )(emb, w1, b1, w2, b2, w3, b3, fr)


def _filter_kernel(hid_ref, emb_ref, w4f_ref, w4b_ref, dl_ref, k_ref, norm_ref):
    r = pl.program_id(1)
    dl = dl_ref[...]
    kf = _dot_hi(hid_ref[0], w4f_ref[0]) * jnp.exp(-emb_ref[0, :, 0:1] * dl)
    kb = _dot_hi(hid_ref[1], w4b_ref[0]) * jnp.exp(-emb_ref[1, :, 0:1] * dl)
    tr = kb.shape[0]
    row = r * tr + lax.broadcasted_iota(jnp.int32, kb.shape, 0)
    kb = jnp.where(row == 0, 0.0, kb)
    k_ref[0] = kb.astype(k_ref.dtype)
    k_ref[1] = kf.astype(k_ref.dtype)

    @pl.when(r == 0)
    def _():
        norm_ref[...] = jnp.zeros(norm_ref.shape, F32)

    norm_ref[...] += jnp.sum(jnp.abs(kf), axis=0, keepdims=True) + jnp.sum(jnp.abs(kb), axis=0, keepdims=True)


def _hyena_filter(hid, emb, w4, deltas, *, tr=512, tc=256):
    _, L, e = emb.shape
    C = deltas.shape[1]
    half_tiles = L // tr // 2
    width = hid.shape[2]
    return pl.pallas_call(
        _filter_kernel,
        grid=(C // tc, L // tr),
        in_specs=[pl.BlockSpec((2, tr, width), lambda c, r: (0, r % half_tiles, 0)),
                  pl.BlockSpec((2, tr, e), lambda c, r: (0, r, 0)),
                  pl.BlockSpec((1, width, tc), lambda c, r: (r // half_tiles, 0, c)),
                  pl.BlockSpec((1, width, tc), lambda c, r: (r // half_tiles, 0, c + C // tc)),
                  pl.BlockSpec((1, tc), lambda c, r: (0, c))],
        out_specs=[pl.BlockSpec((2, tr, tc), lambda c, r: (0, r, c)),
                   pl.BlockSpec((1, tc), lambda c, r: (0, c))],
        out_shape=[jax.ShapeDtypeStruct((2, L, C), BF16), jax.ShapeDtypeStruct((1, C), F32)],
        compiler_params=_params("arbitrary", "arbitrary"),
        name="hyena_filter",
    )(hid, emb, w4, w4, deltas)


def _hyena_pre_kernel(x0m, x0p, x0n, x1m, x1p, x1n, vm, vp, vn, w0, w1, w2, b0, b1, b2, fb, cm_ref, sf_ref,
                      zr_ref, zi_ref, zb_ref, x0_ref, *, tiles_per_seq):
    pos = pl.program_id(0) % tiles_per_seq
    first = pos == 0
    last = pos == tiles_per_seq - 1
    tr, tc = x0m.shape
    sub = min(tc, PRE_SUB_COLS)
    row = lax.broadcasted_iota(jnp.int32, (tr, sub), 0)

    def conv(m_ref, p_ref, n_ref, w_ref, b_ref, cols):
        u = m_ref[:, cols]
        prev = jnp.where(first, 0.0, p_ref[SUBLANES - 1:SUBLANES, cols])
        nxt = jnp.where(last, 0.0, n_ref[0:1, cols])
        um = jnp.where(row == 0, prev, pltpu.roll(u, 1, 0))
        up = jnp.where(row == tr - 1, nxt, pltpu.roll(u, tr - 1, 0))
        w = w_ref[:, cols]
        return um * w[0:1] + u * w[1:2] + up * w[2:3] + b_ref[:, cols]

    for c in range(0, tc, sub):
        cols = slice(c, c + sub)
        z = conv(x1m, x1p, x1n, w1, b1, cols) * conv(vm, vp, vn, w2, b2, cols)
        zb16 = z.astype(BF16)
        zr_ref[0, :, cols] = _dot(cm_ref[...], zb16).astype(zr_ref.dtype)
        zi_ref[0, :, cols] = _dot(sf_ref[...], zb16).astype(zi_ref.dtype)
        zb_ref[:, cols] = (z * fb[:, cols]).astype(zb_ref.dtype)
        x0_ref[:, cols] = conv(x0m, x0p, x0n, w0, b0, cols).astype(x0_ref.dtype)


def _hyena_pre(u, conv_w, conv_b, fbias, cm, sf, *, seq, tc=512):
    t = u.shape[0]
    C = HYENA_WIDTH
    tr = cm.shape[0]
    cb = C // tc
    rb = tr // SUBLANES
    last_rb = t // SUBLANES - 1
    in_specs, args = [], []
    for part in range(3):
        off = part * cb
        in_specs += [pl.BlockSpec((tr, tc), lambda i, j, off=off: (i, j + off)),
                     pl.BlockSpec((SUBLANES, tc), lambda i, j, off=off: (jnp.maximum(i * rb - 1, 0), j + off)),
                     pl.BlockSpec((SUBLANES, tc), lambda i, j, off=off: (jnp.minimum((i + 1) * rb, last_rb), j + off))]
        args += [u, u, u]
    for part in range(3):
        in_specs.append(pl.BlockSpec((3, tc), lambda i, j, off=part * cb: (0, j + off)))
        args.append(conv_w)
    for part in range(3):
        in_specs.append(pl.BlockSpec((1, tc), lambda i, j, off=part * cb: (0, j + off)))
        args.append(conv_b)
    in_specs.append(pl.BlockSpec((1, tc), lambda i, j: (0, j)))
    args.append(fbias)
    in_specs += [pl.BlockSpec((tr, tr), lambda i, j: (0, 0))] * 2
    args += [cm, sf]
    out_spec = pl.BlockSpec((tr, tc), lambda i, j: (i, j))
    spec_spec = pl.BlockSpec((1, tr, tc), lambda i, j: (i, 0, j))
    return pl.pallas_call(
        functools.partial(_hyena_pre_kernel, tiles_per_seq=seq // tr),
        grid=(t // tr, cb),
        in_specs=in_specs,
        out_specs=[spec_spec, spec_spec, out_spec, out_spec],
        out_shape=[jax.ShapeDtypeStruct((t // tr, tr, C), BF16),
                   jax.ShapeDtypeStruct((t // tr, tr, C), BF16),
                   jax.ShapeDtypeStruct((t, C), BF16),
                   jax.ShapeDtypeStruct((t, C), BF16)],
        compiler_params=_params("arbitrary", "arbitrary"),
        name="hyena_pre",
    )(*args)


def _filter_spectrum_kernel(cm_ref, sf_ref, cms_ref, sfs_ref, hi_ref, lo_ref, norm_ref, gr_ref, gi_ref):
    hi = hi_ref[0]
    lo = lo_ref[0]
    gr = _dot(cm_ref[...], hi) + _dot(cms_ref[...], lo)
    gi = _dot(sf_ref[...], hi) + _dot(sfs_ref[...], lo)
    p = gr.shape[0]
    dc = lax.broadcasted_iota(jnp.int32, gr.shape, 0) == 0
    scale = jnp.where(dc, 0.5 / p, 1.0 / p) / norm_ref[...]
    gr_ref[0] = gr * scale
    gi_ref[0] = gi * scale


def _filter_spectrum(cm, sf, cms, sfs, taps, norm, *, tc=1024):
    nseg, P, C = taps.shape
    mat = pl.BlockSpec((P, P), lambda d, c: (0, 0))
    out = pl.BlockSpec((1, P, tc), lambda d, c: (d, 0, c))
    return pl.pallas_call(
        _filter_spectrum_kernel,
        grid=(nseg - 1, C // tc),
        in_specs=[mat, mat, mat, mat,
                  pl.BlockSpec((1, P, tc), lambda d, c: (d + 1, 0, c)),
                  pl.BlockSpec((1, P, tc), lambda d, c: (d, 0, c)),
                  pl.BlockSpec((1, tc), lambda d, c: (0, c))],
        out_specs=[out, out],
        out_shape=[jax.ShapeDtypeStruct((nseg - 1, P, C), F32)] * 2,
        compiler_params=_params("arbitrary", "arbitrary"),
        name="dft_filter",
    )(cm, sf, cms, sfs, taps, taps, norm)


def _conv_kernel(cm_ref, si_ref, zr_ref, zi_ref, gr_ref, gi_ref, x0_ref, zb_ref, o_ref, yr_sc, yi_sc, *, rows):
    nblk, P, tc = zr_ref.shape
    dc = lax.broadcasted_iota(jnp.int32, (rows, tc), 0) == 0

    def out_block(i, carry):
        for r in range(0, P, rows):
            rr = ii = ri = ir = None
            for j in range(nblk):
                lag = i - j + (nblk - 1)
                gr = gr_ref[lag, r:r + rows, :]
                gi = gi_ref[lag, r:r + rows, :]
                zr = zr_ref[j, r:r + rows, :].astype(F32)
                zi = zi_ref[j, r:r + rows, :].astype(F32)
                if j == 0:
                    rr, ii, ri, ir = gr * zr, gi * zi, gr * zi, gi * zr
                else:
                    rr, ii, ri, ir = rr + gr * zr, ii + gi * zi, ri + gr * zi, ir + gi * zr
            if r == 0:
                yr = jnp.where(dc, rr, rr - ii)
                yi = jnp.where(dc, ii, ri + ir)
            else:
                yr = rr - ii
                yi = ri + ir
            yr_sc[r:r + rows, :] = yr.astype(yr_sc.dtype)
            yi_sc[r:r + rows, :] = yi.astype(yi_sc.dtype)
        y = _dot(cm_ref[...], yr_sc[...]) + _dot(si_ref[...], yi_sc[...])
        t0 = pl.multiple_of(i * P, P)
        x0 = x0_ref[pl.ds(t0, P), :].astype(F32)
        o_ref[pl.ds(t0, P), :] = (x0 * (y + zb_ref[pl.ds(t0, P), :].astype(F32))).astype(o_ref.dtype)
        return carry

    lax.fori_loop(0, nblk, out_block, 0)


def _conv(cm, si, zr, zi, gr, gi, x0, zb, *, seq, tc=128, rows=64):
    P = cm.shape[0]
    nblk = seq // P
    nlag = gr.shape[0]
    t, C = x0.shape
    mat = pl.BlockSpec((P, P), lambda c, b: (0, 0))
    zspec = pl.BlockSpec((nblk, P, tc), lambda c, b: (b, 0, c))
    gspec = pl.BlockSpec((nlag, P, tc), lambda c, b: (0, 0, c))
    tile = pl.BlockSpec((seq, tc), lambda c, b: (b, c))
    return pl.pallas_call(
        functools.partial(_conv_kernel, rows=rows),
        grid=(C // tc, t // seq),
        in_specs=[mat, mat, zspec, zspec, gspec, gspec, tile, tile],
        out_specs=tile,
        out_shape=jax.ShapeDtypeStruct((t, C), BF16),
        scratch_shapes=[pltpu.VMEM((P, tc), BF16), pltpu.VMEM((P, tc), BF16)],
        compiler_params=_params("arbitrary", "arbitrary"),
        name="hyena_conv",
    )(cm, si, zr, zi, gr, gi, x0, zb)


def _attn_kernel(q_ref, k_ref, v_ref, o_ref, vt_sc, *, tq, chunks):
    @pl.when(pl.program_id(2) == 0)
    def _():
        vt_sc[0:HEAD_DIM, :] = v_ref[0].T
        vt_sc[HEAD_DIM:, :] = jnp.ones((vt_sc.shape[0] - HEAD_DIM, vt_sc.shape[1]), vt_sc.dtype)

    qs = jnp.concatenate([q_ref[0, :, g * HEAD_DIM:(g + 1) * HEAD_DIM] for g in range(GROUP)], axis=0)
    n = GROUP * tq
    m = jnp.full((1, n), -jnp.inf, F32)
    acc = jnp.zeros((vt_sc.shape[0], n), F32)

    def scores(c):
        start, size = chunks[c]
        return lax.dot_general(k_ref[0, start:start + size, :], qs, (((1,), (1,)), ((), ())),
                               preferred_element_type=F32)

    st_next = scores(0)
    for c, (start, size) in enumerate(chunks):
        st = st_next
        if c + 1 < len(chunks):
            st_next = scores(c + 1)
        m_new = jnp.maximum(m, jnp.max(st, axis=0, keepdims=True))
        alpha = jnp.exp2(m - m_new)
        p = jnp.exp2(st - m_new).astype(BF16)
        acc = alpha * acc + _dot(vt_sc[:, start:start + size], p)
        m = m_new
    out = acc[0:HEAD_DIM] * (1.0 / acc[HEAD_DIM:HEAD_DIM + 1])
    for g in range(GROUP):
        o_ref[0, :, g * HEAD_DIM:(g + 1) * HEAD_DIM] = out[:, g * tq:(g + 1) * tq].T.astype(o_ref.dtype)


def _attention(q, k, v, *, tq=256, tk=512):
    B, L, _ = q.shape
    S = k.shape[1]
    gw = GROUP * HEAD_DIM
    kv_spec = pl.BlockSpec((1, S, HEAD_DIM), lambda b, h, i: (b, 0, h))
    q_spec = pl.BlockSpec((1, tq, gw), lambda b, h, i: (b, i, h))
    chunks = tuple((c, min(tk, S - c)) for c in range(0, S, tk))
    return pl.pallas_call(
        functools.partial(_attn_kernel, tq=tq, chunks=chunks),
        grid=(B, N_KV_HEADS, L // tq),
        in_specs=[q_spec, kv_spec, kv_spec],
        out_specs=q_spec,
        out_shape=jax.ShapeDtypeStruct((B, L, Q_W), BF16),
        scratch_shapes=[pltpu.VMEM((HEAD_DIM + ONES_ROWS, S), BF16)],
        compiler_params=_params("arbitrary", "arbitrary", "arbitrary"),
        name="attention",
    )(q, k, v)


def _merge_kernel(h_ref, ya_ref, yb_ref, wga_ref, wgb_ref, wba_ref, wbb_ref, o_ref):
    h = h_ref[...]
    ya = ya_ref[...]
    yb = yb_ref[...]
    tn = MERGE_TN
    for col in range(0, o_ref.shape[1], tn):
        cols = slice(col, col + tn)
        ga = jax.nn.sigmoid(_dot(h, wga_ref[:, cols]))
        gb = jax.nn.sigmoid(_dot(h, wgb_ref[:, cols]))
        o_ref[:, cols] = (ga * _dot(ya, wba_ref[:, cols]) + gb * _dot(yb, wbb_ref[:, cols])).astype(o_ref.dtype)


def _merge(h, ya, yb, w_in_b, wba, wbb, *, tm=512):
    t, d = h.shape
    row = lambda w: pl.BlockSpec((tm, w), lambda i: (i, 0))
    resident = lambda kdim, blk=0: pl.BlockSpec((kdim, d), lambda i: (0, blk), pipeline_mode=pl.Buffered(1))
    return pl.pallas_call(
        _merge_kernel,
        grid=(t // tm,),
        in_specs=[row(d), row(ya.shape[1]), row(yb.shape[1]),
                  resident(d, GA_OFF // d), resident(d, GB_OFF // d), resident(ya.shape[1]), resident(yb.shape[1])],
        out_specs=row(d),
        out_shape=jax.ShapeDtypeStruct((t, d), BF16),
        compiler_params=_params("arbitrary"),
        name="merge",
    )(h, ya, yb, w_in_b, w_in_b, wba, wbb)


def _outproj_kernel(m_ref, wo_ref, x_ref, mod_ref, g_ref, x1_ref, h2_ref, *, tiles_per_mod):
    b = _mod_row(pl.program_id(0), tiles_per_mod, 0)
    gt1 = mod_ref[pl.ds(b, 1), 2 * D_MODEL:3 * D_MODEL]
    sh2 = mod_ref[pl.ds(b, 1), 3 * D_MODEL:4 * D_MODEL]
    sc2 = mod_ref[pl.ds(b, 1), 4 * D_MODEL:5 * D_MODEL]
    tm = m_ref.shape[0]
    sub = min(tm, OUT_SUB_ROWS)
    for r in range(0, tm, sub):
        mix = _dot(m_ref[r:r + sub, :], wo_ref[...])
        x1 = x_ref[r:r + sub, :] + gt1 * _rms(mix, g_ref[1:2, :])
        x1_ref[r:r + sub, :] = x1
        h2_ref[r:r + sub, :] = (_rms(x1, g_ref[2:3, :]) * (1.0 + sc2) + sh2).astype(h2_ref.dtype)


def _outproj(merged, wo, x, mod, gains, *, seq, tm=512):
    t, d = x.shape
    row = pl.BlockSpec((tm, d), lambda i: (i, 0))
    return pl.pallas_call(
        functools.partial(_outproj_kernel, tiles_per_mod=seq // tm),
        grid=(t // tm,),
        in_specs=[row, pl.BlockSpec((d, d), lambda i: (0, 0)), row,
                  pl.BlockSpec(mod.shape, lambda i: (0, 0)),
                  pl.BlockSpec(gains.shape, lambda i: (0, 0))],
        out_specs=[row, row],
        out_shape=[jax.ShapeDtypeStruct((t, d), F32), jax.ShapeDtypeStruct((t, d), BF16)],
        compiler_params=_params("arbitrary"),
        name="outproj",
    )(merged, wo, x, mod, gains)


def _mlp_kernel(h2_ref, w1_hbm, w2_hbm, x1_ref, mod_ref, g_ref, o_ref, w1_buf, w2_buf, sem, *, tiles_per_mod, tf):
    i = pl.program_id(0)
    n = w2_hbm.shape[0] // tf

    def tile_copies(j, slot):
        start = j * tf if isinstance(j, int) else pl.multiple_of(j * tf, tf)
        return (pltpu.make_async_copy(w1_hbm.at[:, pl.ds(start, tf)], w1_buf.at[slot], sem.at[0, slot]),
                pltpu.make_async_copy(w2_hbm.at[pl.ds(start, tf), :], w2_buf.at[slot], sem.at[1, slot]))

    def start(j, slot):
        for c in tile_copies(j, slot):
            c.start()

    def wait(j, slot):
        for c in tile_copies(j, slot):
            c.wait()

    def product(rows, slot):
        hid = jnp.maximum(_dot(h2_ref[rows, :], w1_buf[slot]), 0.0)
        return _dot((hid * hid).astype(BF16), w2_buf[slot])

    @pl.when(i == 0)
    def _():
        start(0, 0)

    wait(0, 0)
    start(1, 1)
    o_ref[...] = product(slice(None), 0)

    def body(j, carry):
        slot = lax.rem(j, 2)
        wait(j, slot)
        start(j + 1, 1 - slot)
        o_ref[...] = o_ref[...] + product(slice(None), slot)
        return carry

    lax.fori_loop(1, n - 1, body, 0)

    last_slot = (n - 1) % 2
    wait(n - 1, last_slot)

    @pl.when(i + 1 < pl.num_programs(0))
    def _():
        start(0, 0)

    b = _mod_row(i, tiles_per_mod, 0)
    gt2 = mod_ref[pl.ds(b, 1), 5 * D_MODEL:6 * D_MODEL]
    tm = o_ref.shape[0]
    sub = min(tm, OUT_SUB_ROWS)
    for r in range(0, tm, sub):
        rows = slice(r, r + sub)
        acc = o_ref[rows, :] + product(rows, last_slot)
        o_ref[rows, :] = x1_ref[rows, :] + gt2 * _rms(acc, g_ref[3:4, :])


def _mlp(h2, w1, w2, x1, mod, gains, *, seq, tm=1024, tf=MLP_TF):
    t, d = x1.shape
    f = w1.shape[1]
    assert f // tf >= 3 and (f // tf) % 2 == 0, "ring schedule expects an even number (>= 4) of d_ff tiles"
    row = pl.BlockSpec((tm, d), lambda i: (i, 0))
    anyspace = pl.BlockSpec(memory_space=pl.ANY)
    return pl.pallas_call(
        functools.partial(_mlp_kernel, tiles_per_mod=seq // tm, tf=tf),
        grid=(t // tm,),
        in_specs=[row, anyspace, anyspace, row, pl.BlockSpec(mod.shape, lambda i: (0, 0)),
                  pl.BlockSpec(gains.shape, lambda i: (0, 0))],
        out_specs=row,
        out_shape=jax.ShapeDtypeStruct((t, d), F32),
        scratch_shapes=[pltpu.VMEM((2, d, tf), BF16), pltpu.VMEM((2, tf, d), BF16),
                        pltpu.SemaphoreType.DMA((2, 2))],
        compiler_params=_params("arbitrary"),
        name="mlp",
    )(h2, w1, w2, x1, mod, gains)


def _rope_tables(seq):
    rows = jnp.repeat(jnp.arange(seq // GRID_W), GRID_W)
    cols = jnp.tile(jnp.arange(GRID_W), seq // GRID_W)
    inv = ROPE_THETA ** (-jnp.arange(0, AXIS_DIM, 2, dtype=F32) / AXIS_DIM)
    ar = rows[:, None] * inv
    ac = cols[:, None] * inv
    cos = jnp.concatenate([jnp.cos(ar), jnp.cos(ar), jnp.cos(ac), jnp.cos(ac)], axis=-1)
    sin = jnp.concatenate([-jnp.sin(ar), jnp.sin(ar), -jnp.sin(ac), jnp.sin(ac)], axis=-1)
    return cos, sin


def _filter_embedding(seq):
    t = jnp.linspace(0.0, 1.0, seq, dtype=F32)[:, None]
    wpos = 2.0 * math.pi * jnp.arange(seq, dtype=F32)[:, None] / seq
    bands = jnp.linspace(1e-4, FILTER_BANDS - 1, FILTER_BANDS, dtype=F32)
    emb = jnp.concatenate([t, jnp.cos(bands * wpos), -jnp.sin(bands * wpos)], axis=-1)
    emb = jnp.pad(emb, ((0, 0), (0, EMB_PAD - FILTER_EMB)))
    emb_b = jnp.concatenate([emb[:1], emb[:0:-1]], axis=0)
    return jnp.stack([emb, emb_b])


def _block_diag2(w):
    z = jnp.zeros_like(w)
    return jnp.concatenate([jnp.concatenate([w, z], axis=1), jnp.concatenate([z, w], axis=1)], axis=0)


def _dft_tables(seq):
    idx = jnp.arange(seq, dtype=jnp.int32)
    phase = (idx[:, None] * idx[None, :]) & (2 * seq - 1)
    ang = phase.astype(F32) * (math.pi / seq)
    alt = (1 - 2 * (idx & 1)).astype(F32)
    cm = jnp.cos(ang)
    sn = jnp.sin(ang)
    sf = jnp.where(idx[:, None] == 0, alt[None, :], sn)
    si = jnp.where(idx[None, :] == 0, alt[:, None], sn)
    cms = alt[:, None] * cm
    sfs = alt[:, None] * sf
    return tuple(t.astype(BF16) for t in (cm, sf, si, cms, sfs))


def kernel(x, c, ctx, c_ctx, w_ada, b_ada, norm_gains, w_in, conv_w, conv_b, filt_w1, filt_b1, filt_w2, filt_b2, filt_w3, filt_b3, filt_w4, filt_freq, filt_bias, qk_gains, w_branch_a, w_branch_b, w_out, w_ff1, w_ff2):
    B, L, D = x.shape
    T = B * L
    C = HYENA_WIDTH
    lyr = 0
    gains = norm_gains[lyr]
    w_in_b = w_in[lyr].astype(BF16)

    cin = jnp.zeros((MOD_ROWS, D), F32).at[:B].set(c).at[B].set(c_ctx)
    mod = _adaln(cin, w_ada[lyr], b_ada[lyr][None])

    xf = x.reshape(T, D)

    cos, sin = _rope_tables(L)
    qg = qk_gains[lyr, 0][None]
    kg = qk_gains[lyr, 1][None]
    q_scale = HEAD_DIM ** -0.5 * math.log2(math.e)
    k_ctx, v_ctx = _ctx_kv(ctx, mod, gains[0:1], w_in_b, kg, seq=L, mod_row=B)
    h, u, q, k_all, v_all = _inproj(xf, mod, gains[0:1], w_in_b, qg, kg, cos, sin, k_ctx, v_ctx, seq=L,
                                    q_scale=q_scale)

    emb = _filter_embedding(L)
    w1p = jnp.pad(filt_w1[lyr], ((0, EMB_PAD - FILTER_EMB), (0, 0)))
    deltas = jnp.abs(jnp.linspace(MIN_DECAY, MAX_DECAY, C, dtype=F32))[None]
    emb2 = jnp.concatenate([emb[:, :L // 2], emb[:, L // 2:]], axis=-1)
    pair = lambda v: jnp.tile(v, 2)[None]
    w4 = filt_w4[lyr]
    w4_halves = jnp.stack([jnp.concatenate([w4, jnp.zeros_like(w4)]), jnp.concatenate([jnp.zeros_like(w4), w4])])
    hid = _filter_hidden(emb2, _block_diag2(w1p), pair(filt_b1[lyr]), _block_diag2(filt_w2[lyr]), pair(filt_b2[lyr]),
                         _block_diag2(filt_w3[lyr]), pair(filt_b3[lyr]), pair(filt_freq[lyr]))
    filt, norm = _hyena_filter(hid, emb, w4_halves, deltas)
    P = FFT_BLOCK
    cm, sf, si, cms, sfs = _dft_tables(P)
    gr, gi = _filter_spectrum(cm, sf, cms, sfs, filt.reshape(2 * L // P, P, C), norm)
    zr, zi, zb, x0 = _hyena_pre(u, conv_w[lyr], conv_b[lyr][None], filt_bias[lyr][None], cm, sf, seq=L)
    ya = _conv(cm, si, zr, zi, gr, gi, x0, zb, seq=L)

    yb = _attention(q.reshape(B, L, Q_W), k_all, v_all, tq=512, tk=512).reshape(T, Q_W)

    merged = _merge(h, ya, yb, w_in_b, w_branch_a[lyr].astype(BF16), w_branch_b[lyr].astype(BF16))
    x1, h2 = _outproj(merged, w_out[lyr].astype(BF16), xf, mod, gains, seq=L)
    out = _mlp(h2, w_ff1[lyr].astype(BF16), w_ff2[lyr].astype(BF16), x1, mod, gains, seq=L)
    return out.reshape(B, L, D)
```

```python
import functools
import math

import jax
import jax.numpy as jnp
from jax import lax
from jax.experimental import pallas as pl
from jax.experimental.pallas import tpu as pltpu

F32 = jnp.float32
BF16 = jnp.bfloat16

D_MODEL = 2048
CTX_LEN = 256
GRID_W = 64
N_HEADS = 16
HEAD_DIM = 128
N_KV_HEADS = 4
GROUP = N_HEADS // N_KV_HEADS
AXIS_DIM = HEAD_DIM // 2
ROPE_THETA = 10000.0
HYENA_WIDTH = D_MODEL // 2
FILTER_EMB = 17
FILTER_BANDS = (FILTER_EMB - 1) // 2
DECAY_TARGET = 1e-2
MIN_DECAY = math.log(DECAY_TARGET) / 1.5
MAX_DECAY = math.log(DECAY_TARGET) / 0.3
D_FF = 4 * D_MODEL
EPS = 1e-6
Q_W = N_HEADS * HEAD_DIM
KV_W = N_KV_HEADS * HEAD_DIM
Q_OFF = 3 * HYENA_WIDTH
K_OFF = Q_OFF + Q_W
V_OFF = K_OFF + KV_W
GA_OFF = V_OFF + KV_W
GB_OFF = GA_OFF + D_MODEL

SUBLANES = 8
VMEM_LIMIT = 60 * 1024 * 1024
EMB_PAD = 128
MOD_ROWS = 8
HIGHEST = lax.Precision.HIGHEST
ONES_ROWS = 16
IN_TN = KV_W
PRE_SUB_COLS = 256
MERGE_TN = 512
MLP_TF = 512
QK_SUB_ROWS = 256
OUT_SUB_ROWS = 128
FFT_BLOCK = 512


def _params(*sem):
    return pltpu.CompilerParams(dimension_semantics=sem, vmem_limit_bytes=VMEM_LIMIT)


def _dot(a, b):
    return jnp.dot(a, b, preferred_element_type=F32)


def _rms(x, g):
    return x * lax.rsqrt(jnp.mean(x * x, axis=-1, keepdims=True) + EPS) * g


def _adaln_kernel(c_ref, w_ref, b_ref, o_ref):
    c = c_ref[...]
    s = c * jax.nn.sigmoid(c)
    o_ref[...] = _dot(s.astype(BF16), w_ref[...].astype(BF16)) + b_ref[...]


def _adaln(cin, w, b, tn=1024):
    rows, d = cin.shape
    n = w.shape[1]
    return pl.pallas_call(
        _adaln_kernel,
        grid=(n // tn,),
        in_specs=[pl.BlockSpec((rows, d), lambda j: (0, 0)),
                  pl.BlockSpec((d, tn), lambda j: (0, j)),
                  pl.BlockSpec((1, tn), lambda j: (0, j))],
        out_specs=pl.BlockSpec((rows, tn), lambda j: (0, j)),
        out_shape=jax.ShapeDtypeStruct((rows, n), F32),
        compiler_params=_params("arbitrary"),
        name="adaln",
    )(cin, w, b)


def _mod_row(i, tiles_per_mod):
    return i // tiles_per_mod


def _modnorm(x, g, mod_ref, b):
    sh = mod_ref[pl.ds(b, 1), 0:D_MODEL]
    sc = mod_ref[pl.ds(b, 1), D_MODEL:2 * D_MODEL]
    return (_rms(x, g) * (1.0 + sc) + sh).astype(BF16)


def _qk_head_epilogue(acc, g, cos, sin, first, scale, store):
    for hh in range(acc.shape[1] // HEAD_DIM):
        y = _rms(acc[:, hh * HEAD_DIM:(hh + 1) * HEAD_DIM], g)
        if cos is not None:
            partner = jnp.where(first,
                                pltpu.roll(y, HEAD_DIM - AXIS_DIM // 2, 1),
                                pltpu.roll(y, AXIS_DIM // 2, 1))
            y = y * cos + partner * sin
        if scale != 1.0:
            y = y * scale
        store(hh, y.astype(BF16))


def _inproj_kernel(x_ref, mod_ref, g_ref, w_ref, qg_ref, kg_ref, cos_ref, sin_ref, kin_ref, vin_ref,
                   h_ref, u_ref, q_ref, k_ref, v_ref, *, tiles_per_mod, q_scale):
    del kin_ref, vin_ref
    tm = x_ref.shape[0]
    sub = min(tm, QK_SUB_ROWS)
    tn = IN_TN
    b = _mod_row(pl.program_id(0), tiles_per_mod)
    lane = lax.broadcasted_iota(jnp.int32, (sub, HEAD_DIM), 1)
    first = (lane & (AXIS_DIM // 2)) == 0

    def proj(hr, col):
        return _dot(hr, w_ref[:, col:col + tn])

    for r in range(0, tm, sub):
        rows = slice(r, r + sub)
        hr = _modnorm(x_ref[rows, :], g_ref[...], mod_ref, b)
        h_ref[rows, :] = hr
        for col in range(0, Q_OFF, tn):
            u_ref[rows, col:col + tn] = proj(hr, col)
        cos, sin = cos_ref[rows, :], sin_ref[rows, :]
        for col in range(0, Q_W, tn):
            def store_q(hh, y, col=col):
                q_ref[rows, col + hh * HEAD_DIM:col + (hh + 1) * HEAD_DIM] = y
            _qk_head_epilogue(proj(hr, Q_OFF + col), qg_ref[...], cos, sin, first, q_scale, store_q)

        def store_k(hh, y):
            k_ref[0, rows, hh * HEAD_DIM:(hh + 1) * HEAD_DIM] = y
        _qk_head_epilogue(proj(hr, K_OFF), kg_ref[...], cos, sin, first, 1.0, store_k)
        v_ref[0, rows, :] = proj(hr, V_OFF).astype(v_ref.dtype)


def _inproj(x, mod, g, w_in_b, qg, kg, cos, sin, k_ctx, v_ctx, *, seq, q_scale, tm=512):
    t, d = x.shape
    seq_tiles = seq // tm
    anyspace = pl.BlockSpec(memory_space=pl.ANY)
    row = lambda w: pl.BlockSpec((tm, w), lambda i: (i, 0))
    gain = pl.BlockSpec((1, HEAD_DIM), lambda i: (0, 0))
    table = pl.BlockSpec((tm, HEAD_DIM), lambda i: (i % seq_tiles, 0))
    kv = pl.BlockSpec((1, tm, KV_W), lambda i: (i // seq_tiles, i % seq_tiles, 0))
    return pl.pallas_call(
        functools.partial(_inproj_kernel, tiles_per_mod=seq_tiles, q_scale=q_scale),
        grid=(t // tm,),
        in_specs=[row(d), pl.BlockSpec(mod.shape, lambda i: (0, 0)), pl.BlockSpec((1, d), lambda i: (0, 0)),
                  pl.BlockSpec((d, GA_OFF), lambda i: (0, 0), pipeline_mode=pl.Buffered(1)),
                  gain, gain, table, table, anyspace, anyspace],
        out_specs=[row(d), row(Q_OFF), row(Q_W), kv, kv],
        out_shape=[jax.ShapeDtypeStruct((t, d), BF16),
                   jax.ShapeDtypeStruct((t, Q_OFF), F32),
                   jax.ShapeDtypeStruct((t, Q_W), BF16),
                   jax.ShapeDtypeStruct(k_ctx.shape, k_ctx.dtype),
                   jax.ShapeDtypeStruct(v_ctx.shape, v_ctx.dtype)],
        input_output_aliases={8: 3, 9: 4},
        compiler_params=_params("arbitrary"),
        name="inproj",
    )(x, mod, g, w_in_b, qg, kg, cos, sin, k_ctx, v_ctx)


def _ctx_kv_kernel(x_ref, mod_ref, g_ref, wk_ref, wv_ref, kg_ref, k_ref, v_ref, *, seq, mod_row):
    k_ref[0, 0:seq, :] = jnp.zeros((seq, KV_W), k_ref.dtype)
    v_ref[0, 0:seq, :] = jnp.zeros((seq, KV_W), v_ref.dtype)
    hc = _modnorm(x_ref[0], g_ref[...], mod_ref, mod_row)

    def store(hh, y):
        k_ref[0, seq:, hh * HEAD_DIM:(hh + 1) * HEAD_DIM] = y
    _qk_head_epilogue(_dot(hc, wk_ref[...]), kg_ref[...], None, None, None, 1.0, store)
    v_ref[0, seq:, :] = _dot(hc, wv_ref[...]).astype(v_ref.dtype)


def _ctx_kv(ctx, mod, g, w_in_b, kg, *, seq, mod_row):
    nb, lc, d = ctx.shape
    out = pl.BlockSpec((1, seq + lc, KV_W), lambda b: (b, 0, 0))
    shape = jax.ShapeDtypeStruct((nb, seq + lc, KV_W), BF16)
    return pl.pallas_call(
        functools.partial(_ctx_kv_kernel, seq=seq, mod_row=mod_row),
        grid=(nb,),
        in_specs=[pl.BlockSpec((1, lc, d), lambda b: (b, 0, 0)),
                  pl.BlockSpec(mod.shape, lambda b: (0, 0)), pl.BlockSpec((1, d), lambda b: (0, 0)),
                  pl.BlockSpec((d, KV_W), lambda b: (0, K_OFF // KV_W)),
                  pl.BlockSpec((d, KV_W), lambda b: (0, V_OFF // KV_W)),
                  pl.BlockSpec((1, HEAD_DIM), lambda b: (0, 0))],
        out_specs=[out, out],
        out_shape=[shape, shape],
        compiler_params=_params("arbitrary"),
        name="ctx_kv",
    )(ctx, mod, g, w_in_b, w_in_b, kg)


def _dot_hi(a, b):
    return jnp.dot(a, b, precision=HIGHEST, preferred_element_type=F32)


def _filter_hidden_kernel(emb_ref, w1_ref, b1_ref, w2_ref, b2_ref, w3_ref, b3_ref, fr_ref, o_ref):
    fr = fr_ref[...]
    h = jnp.sin(fr * (_dot_hi(emb_ref[0], w1_ref[...]) + b1_ref[...]))
    h = jnp.sin(fr * (_dot_hi(h, w2_ref[...]) + b2_ref[...]))
    o_ref[0] = jnp.sin(fr * (_dot_hi(h, w3_ref[...]) + b3_ref[...]))


def _filter_hidden(emb, w1, b1, w2, b2, w3, b3, fr, *, tr=256):
    sides, rows, e = emb.shape
    width = w3.shape[1]
    full = lambda a: pl.BlockSpec(a.shape, lambda s, r: (0,) * a.ndim)
    return pl.pallas_call(
        _filter_hidden_kernel,
        grid=(sides, rows // tr),
        in_specs=[pl.BlockSpec((1, tr, e), lambda s, r: (s, r, 0)),
                  full(w1), full(b1), full(w2), full(b2), full(w3), full(b3), full(fr)],
        out_specs=pl.BlockSpec((1, tr, width), lambda s, r: (s, r, 0)),
        out_shape=jax.ShapeDtypeStruct((sides, rows, width), F32),
        compiler_params=_params("arbitrary", "arbitrary"),
        name="filter_hidden",
    )(emb, w1, b1, w2, b2, w3, b3, fr)


def _filter_kernel(hid_ref, emb_ref, w4f_ref, w4b_ref, dl_ref, k_ref, norm_ref):
    r = pl.program_id(1)
    dl = dl_ref[...]
    kf = _dot_hi(hid_ref[0], w4f_ref[0]) * jnp.exp(-emb_ref[0, :, 0:1] * dl)
    kb = _dot_hi(hid_ref[1], w4b_ref[0]) * jnp.exp(-emb_ref[1, :, 0:1] * dl)
    tr = kb.shape[0]
    row = r * tr + lax.broadcasted_iota(jnp.int32, kb.shape, 0)
    kb = jnp.where(row == 0, 0.0, kb)
    k_ref[0] = kb.astype(k_ref.dtype)
    k_ref[1] = kf.astype(k_ref.dtype)

    @pl.when(r == 0)
    def _():
        norm_ref[...] = jnp.zeros(norm_ref.shape, F32)

    norm_ref[...] += jnp.sum(jnp.abs(kf), axis=0, keepdims=True) + jnp.sum(jnp.abs(kb), axis=0, keepdims=True)


def _hyena_filter(hid, emb, w4, deltas, *, tr=512, tc=256):
    _, L, e = emb.shape
    C = deltas.shape[1]
    half_tiles = L // tr // 2
    width = hid.shape[2]
    return pl.pallas_call(
        _filter_kernel,
        grid=(C // tc, L // tr),
        in_specs=[pl.BlockSpec((2, tr, width), lambda c, r: (0, r % half_tiles, 0)),
                  pl.BlockSpec((2, tr, e), lambda c, r: (0, r, 0)),
                  pl.BlockSpec((1, width, tc), lambda c, r: (r // half_tiles, 0, c)),
                  pl.BlockSpec((1, width, tc), lambda c, r: (r // half_tiles, 0, c + C // tc)),
                  pl.BlockSpec((1, tc), lambda c, r: (0, c))],
        out_specs=[pl.BlockSpec((2, tr, tc), lambda c, r: (0, r, c)),
                   pl.BlockSpec((1, tc), lambda c, r: (0, c))],
        out_shape=[jax.ShapeDtypeStruct((2, L, C), BF16), jax.ShapeDtypeStruct((1, C), F32)],
        compiler_params=_params("arbitrary", "arbitrary"),
        name="hyena_filter",
    )(hid, emb, w4, w4, deltas)


def _hyena_pre_kernel(x0m, x0p, x0n, x1m, x1p, x1n, vm, vp, vn, w0, w1, w2, b0, b1, b2, fb, cm_ref, sf_ref,
                      zr_ref, zi_ref, zb_ref, x0_ref, *, tiles_per_seq):
    pos = pl.program_id(0) % tiles_per_seq
    first = pos == 0
    last = pos == tiles_per_seq - 1
    tr, tc = x0m.shape
    sub = min(tc, PRE_SUB_COLS)
    row = lax.broadcasted_iota(jnp.int32, (tr, sub), 0)

    def conv(m_ref, p_ref, n_ref, w_ref, b_ref, cols):
        u = m_ref[:, cols]
        prev = jnp.where(first, 0.0, p_ref[SUBLANES - 1:SUBLANES, cols])
        nxt = jnp.where(last, 0.0, n_ref[0:1, cols])
        um = jnp.where(row == 0, prev, pltpu.roll(u, 1, 0))
        up = jnp.where(row == tr - 1, nxt, pltpu.roll(u, tr - 1, 0))
        w = w_ref[:, cols]
        return um * w[0:1] + u * w[1:2] + up * w[2:3] + b_ref[:, cols]

    for c in range(0, tc, sub):
        cols = slice(c, c + sub)
        z = conv(x1m, x1p, x1n, w1, b1, cols) * conv(vm, vp, vn, w2, b2, cols)
        zb16 = z.astype(BF16)
        zr_ref[0, :, cols] = _dot(cm_ref[...], zb16).astype(zr_ref.dtype)
        zi_ref[0, :, cols] = _dot(sf_ref[...], zb16).astype(zi_ref.dtype)
        zb_ref[:, cols] = (z * fb[:, cols]).astype(zb_ref.dtype)
        x0_ref[:, cols] = conv(x0m, x0p, x0n, w0, b0, cols).astype(x0_ref.dtype)


def _hyena_pre(u, conv_w, conv_b, fbias, cm, sf, *, seq, tc=512):
    t = u.shape[0]
    C = HYENA_WIDTH
    tr = cm.shape[0]
    cb = C // tc
    rb = tr // SUBLANES
    last_rb = t // SUBLANES - 1
    in_specs, args = [], []
    for part in range(3):
        off = part * cb
        in_specs += [pl.BlockSpec((tr, tc), lambda i, j, off=off: (i, j + off)),
                     pl.BlockSpec((SUBLANES, tc), lambda i, j, off=off: (jnp.maximum(i * rb - 1, 0), j + off)),
                     pl.BlockSpec((SUBLANES, tc), lambda i, j, off=off: (jnp.minimum((i + 1) * rb, last_rb), j + off))]
        args += [u, u, u]
    for part in range(3):
        in_specs.append(pl.BlockSpec((3, tc), lambda i, j, off=part * cb: (0, j + off)))
        args.append(conv_w)
    for part in range(3):
        in_specs.append(pl.BlockSpec((1, tc), lambda i, j, off=part * cb: (0, j + off)))
        args.append(conv_b)
    in_specs.append(pl.BlockSpec((1, tc), lambda i, j: (0, j)))
    args.append(fbias)
    in_specs += [pl.BlockSpec((tr, tr), lambda i, j: (0, 0))] * 2
    args += [cm, sf]
    out_spec = pl.BlockSpec((tr, tc), lambda i, j: (i, j))
    spec_spec = pl.BlockSpec((1, tr, tc), lambda i, j: (i, 0, j))
    return pl.pallas_call(
        functools.partial(_hyena_pre_kernel, tiles_per_seq=seq // tr),
        grid=(t // tr, cb),
        in_specs=in_specs,
        out_specs=[spec_spec, spec_spec, out_spec, out_spec],
        out_shape=[jax.ShapeDtypeStruct((t // tr, tr, C), BF16),
                   jax.ShapeDtypeStruct((t // tr, tr, C), BF16),
                   jax.ShapeDtypeStruct((t, C), BF16),
                   jax.ShapeDtypeStruct((t, C), BF16)],
        compiler_params=_params("arbitrary", "arbitrary"),
        name="hyena_pre",
    )(*args)


def _filter_spectrum_kernel(cm_ref, sf_ref, cms_ref, sfs_ref, hi_ref, lo_ref, norm_ref, gr_ref, gi_ref):
    hi = hi_ref[0]
    lo = lo_ref[0]
    gr = _dot(cm_ref[...], hi) + _dot(cms_ref[...], lo)
    gi = _dot(sf_ref[...], hi) + _dot(sfs_ref[...], lo)
    p = gr.shape[0]
    dc = lax.broadcasted_iota(jnp.int32, gr.shape, 0) == 0
    scale = jnp.where(dc, 0.5 / p, 1.0 / p) / norm_ref[...]
    gr_ref[0] = gr * scale
    gi_ref[0] = gi * scale


def _filter_spectrum(cm, sf, cms, sfs, taps, norm, *, tc=1024):
    nseg, P, C = taps.shape
    mat = pl.BlockSpec((P, P), lambda d, c: (0, 0))
    out = pl.BlockSpec((1, P, tc), lambda d, c: (d, 0, c))
    return pl.pallas_call(
        _filter_spectrum_kernel,
        grid=(nseg - 1, C // tc),
        in_specs=[mat, mat, mat, mat,
                  pl.BlockSpec((1, P, tc), lambda d, c: (d + 1, 0, c)),
                  pl.BlockSpec((1, P, tc), lambda d, c: (d, 0, c)),
                  pl.BlockSpec((1, tc), lambda d, c: (0, c))],
        out_specs=[out, out],
        out_shape=[jax.ShapeDtypeStruct((nseg - 1, P, C), F32)] * 2,
        compiler_params=_params("arbitrary", "arbitrary"),
        name="dft_filter",
    )(cm, sf, cms, sfs, taps, taps, norm)


def _conv_kernel(cm_ref, si_ref, zr_ref, zi_ref, gr_ref, gi_ref, x0_ref, zb_ref, o_ref, yr_sc, yi_sc, *, rows):
    nblk, P, tc = zr_ref.shape
    dc = lax.broadcasted_iota(jnp.int32, (rows, tc), 0) == 0

    def out_block(i, carry):
        for r in range(0, P, rows):
            rr = ii = ri = ir = None
            for j in range(nblk):
                lag = i - j + (nblk - 1)
                gr = gr_ref[lag, r:r + rows, :]
                gi = gi_ref[lag, r:r + rows, :]
                zr = zr_ref[j, r:r + rows, :].astype(F32)
                zi = zi_ref[j, r:r + rows, :].astype(F32)
                if j == 0:
                    rr, ii, ri, ir = gr * zr, gi * zi, gr * zi, gi * zr
                else:
                    rr, ii, ri, ir = rr + gr * zr, ii + gi * zi, ri + gr * zi, ir + gi * zr
            if r == 0:
                yr = jnp.where(dc, rr, rr - ii)
                yi = jnp.where(dc, ii, ri + ir)
            else:
                yr = rr - ii
                yi = ri + ir
            yr_sc[r:r + rows, :] = yr.astype(yr_sc.dtype)
            yi_sc[r:r + rows, :] = yi.astype(yi_sc.dtype)
        y = _dot(cm_ref[...], yr_sc[...]) + _dot(si_ref[...], yi_sc[...])
        t0 = pl.multiple_of(i * P, P)
        x0 = x0_ref[pl.ds(t0, P), :].astype(F32)
        o_ref[pl.ds(t0, P), :] = (x0 * (y + zb_ref[pl.ds(t0, P), :].astype(F32))).astype(o_ref.dtype)
        return carry

    lax.fori_loop(0, nblk, out_block, 0)


def _conv(cm, si, zr, zi, gr, gi, x0, zb, *, seq, tc=256, rows=32):
    P = cm.shape[0]
    nblk = seq // P
    nlag = gr.shape[0]
    t, C = x0.shape
    mat = pl.BlockSpec((P, P), lambda c, b: (0, 0))
    zspec = pl.BlockSpec((nblk, P, tc), lambda c, b: (b, 0, c))
    gspec = pl.BlockSpec((nlag, P, tc), lambda c, b: (0, 0, c))
    tile = pl.BlockSpec((seq, tc), lambda c, b: (b, c))
    return pl.pallas_call(
        functools.partial(_conv_kernel, rows=rows),
        grid=(C // tc, t // seq),
        in_specs=[mat, mat, zspec, zspec, gspec, gspec, tile, tile],
        out_specs=tile,
        out_shape=jax.ShapeDtypeStruct((t, C), BF16),
        scratch_shapes=[pltpu.VMEM((P, tc), BF16), pltpu.VMEM((P, tc), BF16)],
        compiler_params=_params("arbitrary", "arbitrary"),
        name="hyena_conv",
    )(cm, si, zr, zi, gr, gi, x0, zb)


def _attn_kernel(q_ref, k_ref, v_ref, o_ref, vt_sc, *, tq, chunks):
    @pl.when(pl.program_id(2) == 0)
    def _():
        vt_sc[0:HEAD_DIM, :] = v_ref[0].T
        vt_sc[HEAD_DIM:, :] = jnp.ones((vt_sc.shape[0] - HEAD_DIM, vt_sc.shape[1]), vt_sc.dtype)

    qs = jnp.concatenate([q_ref[0, :, g * HEAD_DIM:(g + 1) * HEAD_DIM] for g in range(GROUP)], axis=0)
    n = GROUP * tq
    m = jnp.full((1, n), -jnp.inf, F32)
    acc = jnp.zeros((vt_sc.shape[0], n), F32)

    def scores(c):
        start, size = chunks[c]
        return lax.dot_general(k_ref[0, start:start + size, :], qs, (((1,), (1,)), ((), ())),
                               preferred_element_type=F32)

    st_next = scores(0)
    for c, (start, size) in enumerate(chunks):
        st = st_next
        if c + 1 < len(chunks):
            st_next = scores(c + 1)
        m_new = jnp.maximum(m, jnp.max(st, axis=0, keepdims=True))
        alpha = jnp.exp2(m - m_new)
        p = jnp.exp2(st - m_new).astype(BF16)
        acc = alpha * acc + _dot(vt_sc[:, start:start + size], p)
        m = m_new
    out = acc[0:HEAD_DIM] * (1.0 / acc[HEAD_DIM:HEAD_DIM + 1])
    for g in range(GROUP):
        o_ref[0, :, g * HEAD_DIM:(g + 1) * HEAD_DIM] = out[:, g * tq:(g + 1) * tq].T.astype(o_ref.dtype)


def _attention(q, k, v, *, tq=512, tk=512):
    B, L, _ = q.shape
    S = k.shape[1]
    gw = GROUP * HEAD_DIM
    kv_spec = pl.BlockSpec((1, S, HEAD_DIM), lambda b, h, i: (b, 0, h))
    q_spec = pl.BlockSpec((1, tq, gw), lambda b, h, i: (b, i, h))
    chunks = tuple((c, min(tk, S - c)) for c in range(0, S, tk))
    return pl.pallas_call(
        functools.partial(_attn_kernel, tq=tq, chunks=chunks),
        grid=(B, N_KV_HEADS, L // tq),
        in_specs=[q_spec, kv_spec, kv_spec],
        out_specs=q_spec,
        out_shape=jax.ShapeDtypeStruct((B, L, Q_W), BF16),
        scratch_shapes=[pltpu.VMEM((HEAD_DIM + ONES_ROWS, S), BF16)],
        compiler_params=_params("arbitrary", "arbitrary", "arbitrary"),
        name="attention",
    )(q, k, v)


def _merge_kernel(h_ref, ya_ref, yb_ref, wga_ref, wgb_ref, wba_ref, wbb_ref, o_ref):
    h = h_ref[...]
    ya = ya_ref[...]
    yb = yb_ref[...]
    tn = MERGE_TN
    for col in range(0, o_ref.shape[1], tn):
        cols = slice(col, col + tn)
        ga = jax.nn.sigmoid(_dot(h, wga_ref[:, cols]))
        gb = jax.nn.sigmoid(_dot(h, wgb_ref[:, cols]))
        o_ref[:, cols] = (ga * _dot(ya, wba_ref[:, cols]) + gb * _dot(yb, wbb_ref[:, cols])).astype(o_ref.dtype)


def _merge(h, ya, yb, w_in_b, wba, wbb, *, tm=512):
    t, d = h.shape
    row = lambda w: pl.BlockSpec((tm, w), lambda i: (i, 0))
    resident = lambda kdim, blk=0: pl.BlockSpec((kdim, d), lambda i: (0, blk), pipeline_mode=pl.Buffered(1))
    return pl.pallas_call(
        _merge_kernel,
        grid=(t // tm,),
        in_specs=[row(d), row(ya.shape[1]), row(yb.shape[1]),
                  resident(d, GA_OFF // d), resident(d, GB_OFF // d), resident(ya.shape[1]), resident(yb.shape[1])],
        out_specs=row(d),
        out_shape=jax.ShapeDtypeStruct((t, d), BF16),
        compiler_params=_params("arbitrary"),
        name="merge",
    )(h, ya, yb, w_in_b, w_in_b, wba, wbb)


def _outproj_kernel(m_ref, wo_ref, x_ref, mod_ref, g_ref, x1_ref, h2_ref, *, tiles_per_mod):
    b = _mod_row(pl.program_id(0), tiles_per_mod)
    gt1 = mod_ref[pl.ds(b, 1), 2 * D_MODEL:3 * D_MODEL]
    sh2 = mod_ref[pl.ds(b, 1), 3 * D_MODEL:4 * D_MODEL]
    sc2 = mod_ref[pl.ds(b, 1), 4 * D_MODEL:5 * D_MODEL]
    tm = m_ref.shape[0]
    sub = min(tm, OUT_SUB_ROWS)
    for r in range(0, tm, sub):
        mix = _dot(m_ref[r:r + sub, :], wo_ref[...])
        x1 = x_ref[r:r + sub, :] + gt1 * _rms(mix, g_ref[1:2, :])
        x1_ref[r:r + sub, :] = x1
        h2_ref[r:r + sub, :] = (_rms(x1, g_ref[2:3, :]) * (1.0 + sc2) + sh2).astype(h2_ref.dtype)


def _outproj(merged, wo, x, mod, gains, *, seq, tm=512):
    t, d = x.shape
    row = pl.BlockSpec((tm, d), lambda i: (i, 0))
    return pl.pallas_call(
        functools.partial(_outproj_kernel, tiles_per_mod=seq // tm),
        grid=(t // tm,),
        in_specs=[row, pl.BlockSpec((d, d), lambda i: (0, 0)), row,
                  pl.BlockSpec(mod.shape, lambda i: (0, 0)),
                  pl.BlockSpec(gains.shape, lambda i: (0, 0))],
        out_specs=[row, row],
        out_shape=[jax.ShapeDtypeStruct((t, d), F32), jax.ShapeDtypeStruct((t, d), BF16)],
        compiler_params=_params("arbitrary"),
        name="outproj",
    )(merged, wo, x, mod, gains)


def _mlp_kernel(h2_ref, w1_ref, w2_ref, x1_ref, mod_ref, g_ref, o_ref, *, tiles_per_mod):
    j = pl.program_id(1)

    @pl.when(j == 0)
    def _():
        o_ref[...] = jnp.zeros(o_ref.shape, F32)

    last = pl.num_programs(1) - 1

    def partial_sum(rows):
        hid = jnp.maximum(_dot(h2_ref[rows, :], w1_ref[...]), 0.0)
        return o_ref[rows, :] + _dot((hid * hid).astype(BF16), w2_ref[...])

    @pl.when(j < last)
    def _():
        o_ref[...] = partial_sum(slice(None))

    @pl.when(j == last)
    def _():
        b = _mod_row(pl.program_id(0), tiles_per_mod)
        gt2 = mod_ref[pl.ds(b, 1), 5 * D_MODEL:6 * D_MODEL]
        tm = o_ref.shape[0]
        sub = min(tm, OUT_SUB_ROWS)
        for r in range(0, tm, sub):
            rows = slice(r, r + sub)
            o_ref[rows, :] = x1_ref[rows, :] + gt2 * _rms(partial_sum(rows), g_ref[3:4, :])


def _mlp(h2, w1, w2, x1, mod, gains, *, seq, tm=1024, tf=MLP_TF):
    t, d = x1.shape
    f = w1.shape[1]
    row = pl.BlockSpec((tm, d), lambda i, j: (i, 0))
    return pl.pallas_call(
        functools.partial(_mlp_kernel, tiles_per_mod=seq // tm),
        grid=(t // tm, f // tf),
        in_specs=[row, pl.BlockSpec((d, tf), lambda i, j: (0, j)), pl.BlockSpec((tf, d), lambda i, j: (j, 0)),
                  row, pl.BlockSpec(mod.shape, lambda i, j: (0, 0)),
                  pl.BlockSpec(gains.shape, lambda i, j: (0, 0))],
        out_specs=row,
        out_shape=jax.ShapeDtypeStruct((t, d), F32),
        compiler_params=_params("arbitrary", "arbitrary"),
        name="mlp",
    )(h2, w1, w2, x1, mod, gains)


def _rope_tables(seq):
    rows = jnp.repeat(jnp.arange(seq // GRID_W), GRID_W)
    cols = jnp.tile(jnp.arange(GRID_W), seq // GRID_W)
    inv = ROPE_THETA ** (-jnp.arange(0, AXIS_DIM, 2, dtype=F32) / AXIS_DIM)
    ar = rows[:, None] * inv
    ac = cols[:, None] * inv
    cos = jnp.concatenate([jnp.cos(ar), jnp.cos(ar), jnp.cos(ac), jnp.cos(ac)], axis=-1)
    sin = jnp.concatenate([-jnp.sin(ar), jnp.sin(ar), -jnp.sin(ac), jnp.sin(ac)], axis=-1)
    return cos, sin


def _filter_embedding(seq):
    t = jnp.linspace(0.0, 1.0, seq, dtype=F32)[:, None]
    wpos = 2.0 * math.pi * jnp.arange(seq, dtype=F32)[:, None] / seq
    bands = jnp.linspace(1e-4, FILTER_BANDS - 1, FILTER_BANDS, dtype=F32)
    emb = jnp.concatenate([t, jnp.cos(bands * wpos), -jnp.sin(bands * wpos)], axis=-1)
    emb = jnp.pad(emb, ((0, 0), (0, EMB_PAD - FILTER_EMB)))
    emb_b = jnp.concatenate([emb[:1], emb[:0:-1]], axis=0)
    return jnp.stack([emb, emb_b])


def _block_diag2(w):
    z = jnp.zeros_like(w)
    return jnp.concatenate([jnp.concatenate([w, z], axis=1), jnp.concatenate([z, w], axis=1)], axis=0)


def _dft_tables(seq):
    idx = jnp.arange(seq, dtype=jnp.int32)
    phase = (idx[:, None] * idx[None, :]) & (2 * seq - 1)
    ang = phase.astype(F32) * (math.pi / seq)
    alt = (1 - 2 * (idx & 1)).astype(F32)
    cm = jnp.cos(ang)
    sn = jnp.sin(ang)
    sf = jnp.where(idx[:, None] == 0, alt[None, :], sn)
    si = jnp.where(idx[None, :] == 0, alt[:, None], sn)
    cms = alt[:, None] * cm
    sfs = alt[:, None] * sf
    return tuple(t.astype(BF16) for t in (cm, sf, si, cms, sfs))


def kernel(x, c, ctx, c_ctx, w_ada, b_ada, norm_gains, w_in, conv_w, conv_b, filt_w1, filt_b1, filt_w2, filt_b2, filt_w3, filt_b3, filt_w4, filt_freq, filt_bias, qk_gains, w_branch_a, w_branch_b, w_out, w_ff1, w_ff2):
    B, L, D = x.shape
    T = B * L
    C = HYENA_WIDTH
    lyr = 0
    gains = norm_gains[lyr]
    w_in_b = w_in[lyr].astype(BF16)

    cin = jnp.zeros((MOD_ROWS, D), F32).at[:B].set(c).at[B].set(c_ctx)
    mod = _adaln(cin, w_ada[lyr], b_ada[lyr][None])

    xf = x.reshape(T, D)

    cos, sin = _rope_tables(L)
    qg = qk_gains[lyr, 0][None]
    kg = qk_gains[lyr, 1][None]
    q_scale = HEAD_DIM ** -0.5 * math.log2(math.e)
    k_ctx, v_ctx = _ctx_kv(ctx, mod, gains[0:1], w_in_b, kg, seq=L, mod_row=B)
    h, u, q, k_all, v_all = _inproj(xf, mod, gains[0:1], w_in_b, qg, kg, cos, sin, k_ctx, v_ctx, seq=L,
                                    q_scale=q_scale)

    emb = _filter_embedding(L)
    w1p = jnp.pad(filt_w1[lyr], ((0, EMB_PAD - FILTER_EMB), (0, 0)))
    deltas = jnp.abs(jnp.linspace(MIN_DECAY, MAX_DECAY, C, dtype=F32))[None]
    emb2 = jnp.concatenate([emb[:, :L // 2], emb[:, L // 2:]], axis=-1)
    pair = lambda v: jnp.tile(v, 2)[None]
    w4 = filt_w4[lyr]
    w4_halves = jnp.stack([jnp.concatenate([w4, jnp.zeros_like(w4)]), jnp.concatenate([jnp.zeros_like(w4), w4])])
    hid = _filter_hidden(emb2, _block_diag2(w1p), pair(filt_b1[lyr]), _block_diag2(filt_w2[lyr]), pair(filt_b2[lyr]),
                         _block_diag2(filt_w3[lyr]), pair(filt_b3[lyr]), pair(filt_freq[lyr]))
    filt, norm = _hyena_filter(hid, emb, w4_halves, deltas)
    P = FFT_BLOCK
    cm, sf, si, cms, sfs = _dft_tables(P)
    gr, gi = _filter_spectrum(cm, sf, cms, sfs, filt.reshape(2 * L // P, P, C), norm)
    zr, zi, zb, x0 = _hyena_pre(u, conv_w[lyr], conv_b[lyr][None], filt_bias[lyr][None], cm, sf, seq=L)
    ya = _conv(cm, si, zr, zi, gr, gi, x0, zb, seq=L)

    yb = _attention(q.reshape(B, L, Q_W), k_all, v_all).reshape(T, Q_W)

    merged = _merge(h, ya, yb, w_in_b, w_branch_a[lyr].astype(BF16), w_branch_b[lyr].astype(BF16))
    x1, h2 = _outproj(merged, w_out[lyr].astype(BF16), xf, mod, gains, seq=L)
    out = _mlp(h2, w_ff1[lyr].astype(BF16), w_ff2[lyr].astype(BF16), x1, mod, gains, seq=L)
    return out.reshape(B, L, D)
```

```python
import functools
import math

import jax
import jax.numpy as jnp
from jax import lax
from jax.experimental import pallas as pl
from jax.experimental.pallas import tpu as pltpu

F32 = jnp.float32
BF16 = jnp.bfloat16

D_MODEL = 2048
CTX_LEN = 256
GRID_W = 64
N_HEADS = 16
HEAD_DIM = 128
N_KV_HEADS = 4
GROUP = N_HEADS // N_KV_HEADS
AXIS_DIM = HEAD_DIM // 2
ROPE_THETA = 10000.0
HYENA_WIDTH = D_MODEL // 2
FILTER_EMB = 17
FILTER_BANDS = (FILTER_EMB - 1) // 2
DECAY_TARGET = 1e-2
MIN_DECAY = math.log(DECAY_TARGET) / 1.5
MAX_DECAY = math.log(DECAY_TARGET) / 0.3
D_FF = 4 * D_MODEL
EPS = 1e-6
Q_W = N_HEADS * HEAD_DIM
KV_W = N_KV_HEADS * HEAD_DIM
Q_OFF = 3 * HYENA_WIDTH
K_OFF = Q_OFF + Q_W
V_OFF = K_OFF + KV_W
GA_OFF = V_OFF + KV_W
GB_OFF = GA_OFF + D_MODEL

SUBLANES = 8
VMEM_LIMIT = 60 * 1024 * 1024
EMB_PAD = 128
MOD_ROWS = 8
HIGHEST = lax.Precision.HIGHEST
ONES_ROWS = 16
IN_TN = KV_W
PRE_SUB_COLS = 256
MERGE_TN = 512
MLP_TF = 512
QK_SUB_ROWS = 256
OUT_SUB_ROWS = 128
FFT_BLOCK = 512


def _params(*sem):
    return pltpu.CompilerParams(dimension_semantics=sem, vmem_limit_bytes=VMEM_LIMIT)


def _dot(a, b):
    return jnp.dot(a, b, preferred_element_type=F32)


def _rms(x, g):
    return x * lax.rsqrt(jnp.mean(x * x, axis=-1, keepdims=True) + EPS) * g


def _adaln_kernel(c_ref, w_ref, b_ref, o_ref):
    c = c_ref[...]
    s = c * jax.nn.sigmoid(c)
    o_ref[...] = _dot(s.astype(BF16), w_ref[...].astype(BF16)) + b_ref[...]


def _adaln(cin, w, b, tn=1024):
    rows, d = cin.shape
    n = w.shape[1]
    return pl.pallas_call(
        _adaln_kernel,
        grid=(n // tn,),
        in_specs=[pl.BlockSpec((rows, d), lambda j: (0, 0)),
                  pl.BlockSpec((d, tn), lambda j: (0, j)),
                  pl.BlockSpec((1, tn), lambda j: (0, j))],
        out_specs=pl.BlockSpec((rows, tn), lambda j: (0, j)),
        out_shape=jax.ShapeDtypeStruct((rows, n), F32),
        compiler_params=_params("arbitrary"),
        name="adaln",
    )(cin, w, b)


def _mod_row(i, tiles_per_mod):
    return i // tiles_per_mod


def _modnorm(x, g, mod_ref, b):
    sh = mod_ref[pl.ds(b, 1), 0:D_MODEL]
    sc = mod_ref[pl.ds(b, 1), D_MODEL:2 * D_MODEL]
    return (_rms(x, g) * (1.0 + sc) + sh).astype(BF16)


def _qk_head_epilogue(acc, g, cos, sin, first, scale, store):
    for hh in range(acc.shape[1] // HEAD_DIM):
        y = _rms(acc[:, hh * HEAD_DIM:(hh + 1) * HEAD_DIM], g)
        if cos is not None:
            partner = jnp.where(first,
                                pltpu.roll(y, HEAD_DIM - AXIS_DIM // 2, 1),
                                pltpu.roll(y, AXIS_DIM // 2, 1))
            y = y * cos + partner * sin
        if scale != 1.0:
            y = y * scale
        store(hh, y.astype(BF16))


def _inproj_kernel(x_ref, mod_ref, g_ref, w_ref, qg_ref, kg_ref, cos_ref, sin_ref, kin_ref, vin_ref,
                   h_ref, u_ref, q_ref, k_ref, v_ref, *, tiles_per_mod, q_scale):
    del kin_ref, vin_ref
    tm = x_ref.shape[0]
    sub = min(tm, QK_SUB_ROWS)
    tn = IN_TN
    b = _mod_row(pl.program_id(0), tiles_per_mod)
    lane = lax.broadcasted_iota(jnp.int32, (sub, HEAD_DIM), 1)
    first = (lane & (AXIS_DIM // 2)) == 0

    def proj(hr, col):
        return _dot(hr, w_ref[:, col:col + tn])

    for r in range(0, tm, sub):
        rows = slice(r, r + sub)
        hr = _modnorm(x_ref[rows, :], g_ref[...], mod_ref, b)
        h_ref[rows, :] = hr
        for col in range(0, Q_OFF, tn):
            u_ref[rows, col:col + tn] = proj(hr, col)
        cos, sin = cos_ref[rows, :], sin_ref[rows, :]
        for col in range(0, Q_W, tn):
            def store_q(hh, y, col=col):
                q_ref[rows, col + hh * HEAD_DIM:col + (hh + 1) * HEAD_DIM] = y
            _qk_head_epilogue(proj(hr, Q_OFF + col), qg_ref[...], cos, sin, first, q_scale, store_q)

        def store_k(hh, y):
            k_ref[0, rows, hh * HEAD_DIM:(hh + 1) * HEAD_DIM] = y
        _qk_head_epilogue(proj(hr, K_OFF), kg_ref[...], cos, sin, first, 1.0, store_k)
        v_ref[0, rows, :] = proj(hr, V_OFF).astype(v_ref.dtype)


def _inproj(x, mod, g, w_in_b, qg, kg, cos, sin, k_ctx, v_ctx, *, seq, q_scale, tm=512):
    t, d = x.shape
    seq_tiles = seq // tm
    anyspace = pl.BlockSpec(memory_space=pl.ANY)
    row = lambda w: pl.BlockSpec((tm, w), lambda i: (i, 0))
    gain = pl.BlockSpec((1, HEAD_DIM), lambda i: (0, 0))
    table = pl.BlockSpec((tm, HEAD_DIM), lambda i: (i % seq_tiles, 0))
    kv = pl.BlockSpec((1, tm, KV_W), lambda i: (i // seq_tiles, i % seq_tiles, 0))
    return pl.pallas_call(
        functools.partial(_inproj_kernel, tiles_per_mod=seq_tiles, q_scale=q_scale),
        grid=(t // tm,),
        in_specs=[row(d), pl.BlockSpec(mod.shape, lambda i: (0, 0)), pl.BlockSpec((1, d), lambda i: (0, 0)),
                  pl.BlockSpec((d, GA_OFF), lambda i: (0, 0), pipeline_mode=pl.Buffered(1)),
                  gain, gain, table, table, anyspace, anyspace],
        out_specs=[row(d), row(Q_OFF), row(Q_W), kv, kv],
        out_shape=[jax.ShapeDtypeStruct((t, d), BF16),
                   jax.ShapeDtypeStruct((t, Q_OFF), F32),
                   jax.ShapeDtypeStruct((t, Q_W), BF16),
                   jax.ShapeDtypeStruct(k_ctx.shape, k_ctx.dtype),
                   jax.ShapeDtypeStruct(v_ctx.shape, v_ctx.dtype)],
        input_output_aliases={8: 3, 9: 4},
        compiler_params=_params("arbitrary"),
        name="inproj",
    )(x, mod, g, w_in_b, qg, kg, cos, sin, k_ctx, v_ctx)


def _ctx_kv_kernel(x_ref, mod_ref, g_ref, wk_ref, wv_ref, kg_ref, k_ref, v_ref, *, seq, mod_row):
    k_ref[0, 0:seq, :] = jnp.zeros((seq, KV_W), k_ref.dtype)
    v_ref[0, 0:seq, :] = jnp.zeros((seq, KV_W), v_ref.dtype)
    hc = _modnorm(x_ref[0], g_ref[...], mod_ref, mod_row)

    def store(hh, y):
        k_ref[0, seq:, hh * HEAD_DIM:(hh + 1) * HEAD_DIM] = y
    _qk_head_epilogue(_dot(hc, wk_ref[...]), kg_ref[...], None, None, None, 1.0, store)
    v_ref[0, seq:, :] = _dot(hc, wv_ref[...]).astype(v_ref.dtype)


def _ctx_kv(ctx, mod, g, w_in_b, kg, *, seq, mod_row):
    nb, lc, d = ctx.shape
    out = pl.BlockSpec((1, seq + lc, KV_W), lambda b: (b, 0, 0))
    shape = jax.ShapeDtypeStruct((nb, seq + lc, KV_W), BF16)
    return pl.pallas_call(
        functools.partial(_ctx_kv_kernel, seq=seq, mod_row=mod_row),
        grid=(nb,),
        in_specs=[pl.BlockSpec((1, lc, d), lambda b: (b, 0, 0)),
                  pl.BlockSpec(mod.shape, lambda b: (0, 0)), pl.BlockSpec((1, d), lambda b: (0, 0)),
                  pl.BlockSpec((d, KV_W), lambda b: (0, K_OFF // KV_W)),
                  pl.BlockSpec((d, KV_W), lambda b: (0, V_OFF // KV_W)),
                  pl.BlockSpec((1, HEAD_DIM), lambda b: (0, 0))],
        out_specs=[out, out],
        out_shape=[shape, shape],
        compiler_params=_params("arbitrary"),
        name="ctx_kv",
    )(ctx, mod, g, w_in_b, w_in_b, kg)


def _dot_hi(a, b):
    return jnp.dot(a, b, precision=HIGHEST, preferred_element_type=F32)


def _filter_hidden_kernel(emb_ref, w1_ref, b1_ref, w2_ref, b2_ref, w3_ref, b3_ref, fr_ref, o_ref):
    fr = fr_ref[...]
    h = jnp.sin(fr * (_dot_hi(emb_ref[0], w1_ref[...]) + b1_ref[...]))
    h = jnp.sin(fr * (_dot_hi(h, w2_ref[...]) + b2_ref[...]))
    o_ref[0] = jnp.sin(fr * (_dot_hi(h, w3_ref[...]) + b3_ref[...]))


def _filter_hidden(emb, w1, b1, w2, b2, w3, b3, fr, *, tr=256):
    sides, rows, e = emb.shape
    width = w3.shape[1]
    full = lambda a: pl.BlockSpec(a.shape, lambda s, r: (0,) * a.ndim)
    return pl.pallas_call(
        _filter_hidden_kernel,
        grid=(sides, rows // tr),
        in_specs=[pl.BlockSpec((1, tr, e), lambda s, r: (s, r, 0)),
                  full(w1), full(b1), full(w2), full(b2), full(w3), full(b3), full(fr)],
        out_specs=pl.BlockSpec((1, tr, width), lambda s, r: (s, r, 0)),
        out_shape=jax.ShapeDtypeStruct((sides, rows, width), F32),
        compiler_params=_params("arbitrary", "arbitrary"),
        name="filter_hidden",
    )(emb, w1, b1, w2, b2, w3, b3, fr)


def _filter_kernel(hid_ref, emb_ref, w4f_ref, w4b_ref, dl_ref, k_ref, norm_ref):
    r = pl.program_id(1)
    dl = dl_ref[...]
    kf = _dot_hi(hid_ref[0], w4f_ref[0]) * jnp.exp(-emb_ref[0, :, 0:1] * dl)
    kb = _dot_hi(hid_ref[1], w4b_ref[0]) * jnp.exp(-emb_ref[1, :, 0:1] * dl)
    tr = kb.shape[0]
    row = r * tr + lax.broadcasted_iota(jnp.int32, kb.shape, 0)
    kb = jnp.where(row == 0, 0.0, kb)
    k_ref[0] = kb.astype(k_ref.dtype)
    k_ref[1] = kf.astype(k_ref.dtype)

    @pl.when(r == 0)
    def _():
        norm_ref[...] = jnp.zeros(norm_ref.shape, F32)

    norm_ref[...] += jnp.sum(jnp.abs(kf), axis=0, keepdims=True) + jnp.sum(jnp.abs(kb), axis=0, keepdims=True)


def _hyena_filter(hid, emb, w4, deltas, *, tr=512, tc=256):
    _, L, e = emb.shape
    C = deltas.shape[1]
    half_tiles = L // tr // 2
    width = hid.shape[2]
    return pl.pallas_call(
        _filter_kernel,
        grid=(C // tc, L // tr),
        in_specs=[pl.BlockSpec((2, tr, width), lambda c, r: (0, r % half_tiles, 0)),
                  pl.BlockSpec((2, tr, e), lambda c, r: (0, r, 0)),
                  pl.BlockSpec((1, width, tc), lambda c, r: (r // half_tiles, 0, c)),
                  pl.BlockSpec((1, width, tc), lambda c, r: (r // half_tiles, 0, c + C // tc)),
                  pl.BlockSpec((1, tc), lambda c, r: (0, c))],
        out_specs=[pl.BlockSpec((2, tr, tc), lambda c, r: (0, r, c)),
                   pl.BlockSpec((1, tc), lambda c, r: (0, c))],
        out_shape=[jax.ShapeDtypeStruct((2, L, C), BF16), jax.ShapeDtypeStruct((1, C), F32)],
        compiler_params=_params("arbitrary", "arbitrary"),
        name="hyena_filter",
    )(hid, emb, w4, w4, deltas)


def _hyena_pre_kernel(x0m, x0p, x0n, x1m, x1p, x1n, vm, vp, vn, w0, w1, w2, b0, b1, b2, fb, cm_ref, sf_ref,
                      zr_ref, zi_ref, zb_ref, x0_ref, *, tiles_per_seq):
    pos = pl.program_id(0) % tiles_per_seq
    first = pos == 0
    last = pos == tiles_per_seq - 1
    tr, tc = x0m.shape
    sub = min(tc, PRE_SUB_COLS)
    row = lax.broadcasted_iota(jnp.int32, (tr, sub), 0)

    def conv(m_ref, p_ref, n_ref, w_ref, b_ref, cols):
        u = m_ref[:, cols]
        prev = jnp.where(first, 0.0, p_ref[SUBLANES - 1:SUBLANES, cols])
        nxt = jnp.where(last, 0.0, n_ref[0:1, cols])
        um = jnp.where(row == 0, prev, pltpu.roll(u, 1, 0))
        up = jnp.where(row == tr - 1, nxt, pltpu.roll(u, tr - 1, 0))
        w = w_ref[:, cols]
        return um * w[0:1] + u * w[1:2] + up * w[2:3] + b_ref[:, cols]

    for c in range(0, tc, sub):
        cols = slice(c, c + sub)
        z = conv(x1m, x1p, x1n, w1, b1, cols) * conv(vm, vp, vn, w2, b2, cols)
        zb16 = z.astype(BF16)
        zr_ref[0, :, cols] = _dot(cm_ref[...], zb16).astype(zr_ref.dtype)
        zi_ref[0, :, cols] = _dot(sf_ref[...], zb16).astype(zi_ref.dtype)
        zb_ref[:, cols] = (z * fb[:, cols]).astype(zb_ref.dtype)
        x0_ref[:, cols] = conv(x0m, x0p, x0n, w0, b0, cols).astype(x0_ref.dtype)


def _hyena_pre(u, conv_w, conv_b, fbias, cm, sf, *, seq, tc=512):
    t = u.shape[0]
    C = HYENA_WIDTH
    tr = cm.shape[0]
    cb = C // tc
    rb = tr // SUBLANES
    last_rb = t // SUBLANES - 1
    in_specs, args = [], []
    for part in range(3):
        off = part * cb
        in_specs += [pl.BlockSpec((tr, tc), lambda i, j, off=off: (i, j + off)),
                     pl.BlockSpec((SUBLANES, tc), lambda i, j, off=off: (jnp.maximum(i * rb - 1, 0), j + off)),
                     pl.BlockSpec((SUBLANES, tc), lambda i, j, off=off: (jnp.minimum((i + 1) * rb, last_rb), j + off))]
        args += [u, u, u]
    for part in range(3):
        in_specs.append(pl.BlockSpec((3, tc), lambda i, j, off=part * cb: (0, j + off)))
        args.append(conv_w)
    for part in range(3):
        in_specs.append(pl.BlockSpec((1, tc), lambda i, j, off=part * cb: (0, j + off)))
        args.append(conv_b)
    in_specs.append(pl.BlockSpec((1, tc), lambda i, j: (0, j)))
    args.append(fbias)
    in_specs += [pl.BlockSpec((tr, tr), lambda i, j: (0, 0))] * 2
    args += [cm, sf]
    out_spec = pl.BlockSpec((tr, tc), lambda i, j: (i, j))
    spec_spec = pl.BlockSpec((1, tr, tc), lambda i, j: (i, 0, j))
    return pl.pallas_call(
        functools.partial(_hyena_pre_kernel, tiles_per_seq=seq // tr),
        grid=(t // tr, cb),
        in_specs=in_specs,
        out_specs=[spec_spec, spec_spec, out_spec, out_spec],
        out_shape=[jax.ShapeDtypeStruct((t // tr, tr, C), BF16),
                   jax.ShapeDtypeStruct((t // tr, tr, C), BF16),
                   jax.ShapeDtypeStruct((t, C), BF16),
                   jax.ShapeDtypeStruct((t, C), BF16)],
        compiler_params=_params("arbitrary", "arbitrary"),
        name="hyena_pre",
    )(*args)


def _filter_spectrum_kernel(cm_ref, sf_ref, seg_ref, norm_ref, gr_ref, gi_ref, pr_sc, pi_sc):
    @pl.when(pl.program_id(1) == 0)
    def _():
        pr_sc[...] = jnp.zeros(pr_sc.shape, F32)
        pi_sc[...] = jnp.zeros(pi_sc.shape, F32)

    p, tc = pr_sc.shape
    sub = min(tc, PRE_SUB_COLS)
    freq = lax.broadcasted_iota(jnp.int32, (p, sub), 0)
    sign = (1 - 2 * (freq & 1)).astype(F32)
    weight = jnp.where(freq == 0, 0.5 / p, 1.0 / p)
    for c in range(0, tc, sub):
        cols = slice(c, c + sub)
        seg = seg_ref[0, :, cols]
        sr = _dot(cm_ref[...], seg)
        si = _dot(sf_ref[...], seg)
        scale = weight / norm_ref[:, cols]
        gr_ref[0, :, cols] = (sr + sign * pr_sc[:, cols]) * scale
        gi_ref[0, :, cols] = (si + sign * pi_sc[:, cols]) * scale
        pr_sc[:, cols] = sr
        pi_sc[:, cols] = si


def _filter_spectrum(cm, sf, taps, norm, *, tc=1024):
    nseg, P, C = taps.shape
    mat = pl.BlockSpec((P, P), lambda c, d: (0, 0))
    out = pl.BlockSpec((1, P, tc), lambda c, d: (jnp.maximum(d - 1, 0), 0, c))
    return pl.pallas_call(
        _filter_spectrum_kernel,
        grid=(C // tc, nseg),
        in_specs=[mat, mat,
                  pl.BlockSpec((1, P, tc), lambda c, d: (d, 0, c)),
                  pl.BlockSpec((1, tc), lambda c, d: (0, c))],
        out_specs=[out, out],
        out_shape=[jax.ShapeDtypeStruct((nseg - 1, P, C), F32)] * 2,
        scratch_shapes=[pltpu.VMEM((P, tc), F32), pltpu.VMEM((P, tc), F32)],
        compiler_params=_params("arbitrary", "arbitrary"),
        name="dft_filter",
    )(cm, sf, taps, norm)


def _conv_kernel(cm_ref, si_ref, zr_ref, zi_ref, gr_ref, gi_ref, x0_ref, zb_ref, o_ref, yr_sc, yi_sc, *, rows):
    nblk, P, tc = zr_ref.shape
    dc = lax.broadcasted_iota(jnp.int32, (rows, tc), 0) == 0

    def out_block(i, carry):
        for r in range(0, P, rows):
            rr = ii = ri = ir = None
            for j in range(nblk):
                lag = i - j + (nblk - 1)
                gr = gr_ref[lag, r:r + rows, :]
                gi = gi_ref[lag, r:r + rows, :]
                zr = zr_ref[j, r:r + rows, :].astype(F32)
                zi = zi_ref[j, r:r + rows, :].astype(F32)
                if j == 0:
                    rr, ii, ri, ir = gr * zr, gi * zi, gr * zi, gi * zr
                else:
                    rr, ii, ri, ir = rr + gr * zr, ii + gi * zi, ri + gr * zi, ir + gi * zr
            if r == 0:
                yr = jnp.where(dc, rr, rr - ii)
                yi = jnp.where(dc, ii, ri + ir)
            else:
                yr = rr - ii
                yi = ri + ir
            yr_sc[r:r + rows, :] = yr.astype(yr_sc.dtype)
            yi_sc[r:r + rows, :] = yi.astype(yi_sc.dtype)
        y = _dot(cm_ref[...], yr_sc[...]) + _dot(si_ref[...], yi_sc[...])
        t0 = pl.multiple_of(i * P, P)
        x0 = x0_ref[pl.ds(t0, P), :].astype(F32)
        o_ref[pl.ds(t0, P), :] = (x0 * (y + zb_ref[pl.ds(t0, P), :].astype(F32))).astype(o_ref.dtype)
        return carry

    lax.fori_loop(0, nblk, out_block, 0)


def _conv(cm, si, zr, zi, gr, gi, x0, zb, *, seq, tc=256, rows=32):
    P = cm.shape[0]
    nblk = seq // P
    nlag = gr.shape[0]
    t, C = x0.shape
    mat = pl.BlockSpec((P, P), lambda c, b: (0, 0))
    zspec = pl.BlockSpec((nblk, P, tc), lambda c, b: (b, 0, c))
    gspec = pl.BlockSpec((nlag, P, tc), lambda c, b: (0, 0, c))
    tile = pl.BlockSpec((seq, tc), lambda c, b: (b, c))
    return pl.pallas_call(
        functools.partial(_conv_kernel, rows=rows),
        grid=(C // tc, t // seq),
        in_specs=[mat, mat, zspec, zspec, gspec, gspec, tile, tile],
        out_specs=tile,
        out_shape=jax.ShapeDtypeStruct((t, C), BF16),
        scratch_shapes=[pltpu.VMEM((P, tc), BF16), pltpu.VMEM((P, tc), BF16)],
        compiler_params=_params("arbitrary", "arbitrary"),
        name="hyena_conv",
    )(cm, si, zr, zi, gr, gi, x0, zb)


def _attn_kernel(q_ref, k_ref, v_ref, o_ref, vt_sc, *, tq, chunks):
    @pl.when(pl.program_id(2) == 0)
    def _():
        vt_sc[0:HEAD_DIM, :] = v_ref[0].T
        vt_sc[HEAD_DIM:, :] = jnp.ones((vt_sc.shape[0] - HEAD_DIM, vt_sc.shape[1]), vt_sc.dtype)

    qs = jnp.concatenate([q_ref[0, :, g * HEAD_DIM:(g + 1) * HEAD_DIM] for g in range(GROUP)], axis=0)
    n = GROUP * tq
    m = jnp.full((1, n), -jnp.inf, F32)
    acc = jnp.zeros((vt_sc.shape[0], n), F32)

    def scores(c):
        start, size = chunks[c]
        return lax.dot_general(k_ref[0, start:start + size, :], qs, (((1,), (1,)), ((), ())),
                               preferred_element_type=F32)

    st_next = scores(0)
    for c, (start, size) in enumerate(chunks):
        st = st_next
        if c + 1 < len(chunks):
            st_next = scores(c + 1)
        m_new = jnp.maximum(m, jnp.max(st, axis=0, keepdims=True))
        alpha = jnp.exp2(m - m_new)
        p = jnp.exp2(st - m_new).astype(BF16)
        acc = alpha * acc + _dot(vt_sc[:, start:start + size], p)
        m = m_new
    out = acc[0:HEAD_DIM] * (1.0 / acc[HEAD_DIM:HEAD_DIM + 1])
    for g in range(GROUP):
        o_ref[0, :, g * HEAD_DIM:(g + 1) * HEAD_DIM] = out[:, g * tq:(g + 1) * tq].T.astype(o_ref.dtype)


def _attention(q, k, v, *, tq=512, tk=512):
    B, L, _ = q.shape
    S = k.shape[1]
    gw = GROUP * HEAD_DIM
    kv_spec = pl.BlockSpec((1, S, HEAD_DIM), lambda b, h, i: (b, 0, h))
    q_spec = pl.BlockSpec((1, tq, gw), lambda b, h, i: (b, i, h))
    chunks = tuple((c, min(tk, S - c)) for c in range(0, S, tk))
    return pl.pallas_call(
        functools.partial(_attn_kernel, tq=tq, chunks=chunks),
        grid=(B, N_KV_HEADS, L // tq),
        in_specs=[q_spec, kv_spec, kv_spec],
        out_specs=q_spec,
        out_shape=jax.ShapeDtypeStruct((B, L, Q_W), BF16),
        scratch_shapes=[pltpu.VMEM((HEAD_DIM + ONES_ROWS, S), BF16)],
        compiler_params=_params("arbitrary", "arbitrary", "arbitrary"),
        name="attention",
    )(q, k, v)


def _merge_kernel(h_ref, ya_ref, yb_ref, wga_ref, wgb_ref, wba_ref, wbb_ref, o_ref):
    h = h_ref[...]
    ya = ya_ref[...]
    yb = yb_ref[...]
    tn = MERGE_TN
    for col in range(0, o_ref.shape[1], tn):
        cols = slice(col, col + tn)
        ga = jax.nn.sigmoid(_dot(h, wga_ref[:, cols]))
        gb = jax.nn.sigmoid(_dot(h, wgb_ref[:, cols]))
        o_ref[:, cols] = (ga * _dot(ya, wba_ref[:, cols]) + gb * _dot(yb, wbb_ref[:, cols])).astype(o_ref.dtype)


def _merge(h, ya, yb, w_in_b, wba, wbb, *, tm=512):
    t, d = h.shape
    row = lambda w: pl.BlockSpec((tm, w), lambda i: (i, 0))
    resident = lambda kdim, blk=0: pl.BlockSpec((kdim, d), lambda i: (0, blk), pipeline_mode=pl.Buffered(1))
    return pl.pallas_call(
        _merge_kernel,
        grid=(t // tm,),
        in_specs=[row(d), row(ya.shape[1]), row(yb.shape[1]),
                  resident(d, GA_OFF // d), resident(d, GB_OFF // d), resident(ya.shape[1]), resident(yb.shape[1])],
        out_specs=row(d),
        out_shape=jax.ShapeDtypeStruct((t, d), BF16),
        compiler_params=_params("arbitrary"),
        name="merge",
    )(h, ya, yb, w_in_b, w_in_b, wba, wbb)


def _outproj_kernel(m_ref, wo_ref, x_ref, mod_ref, g_ref, x1_ref, h2_ref, *, tiles_per_mod):
    b = _mod_row(pl.program_id(0), tiles_per_mod)
    gt1 = mod_ref[pl.ds(b, 1), 2 * D_MODEL:3 * D_MODEL]
    sh2 = mod_ref[pl.ds(b, 1), 3 * D_MODEL:4 * D_MODEL]
    sc2 = mod_ref[pl.ds(b, 1), 4 * D_MODEL:5 * D_MODEL]
    tm = m_ref.shape[0]
    sub = min(tm, OUT_SUB_ROWS)
    for r in range(0, tm, sub):
        mix = _dot(m_ref[r:r + sub, :], wo_ref[...])
        x1 = x_ref[r:r + sub, :] + gt1 * _rms(mix, g_ref[1:2, :])
        x1_ref[r:r + sub, :] = x1
        h2_ref[r:r + sub, :] = (_rms(x1, g_ref[2:3, :]) * (1.0 + sc2) + sh2).astype(h2_ref.dtype)


def _outproj(merged, wo, x, mod, gains, *, seq, tm=512):
    t, d = x.shape
    row = pl.BlockSpec((tm, d), lambda i: (i, 0))
    return pl.pallas_call(
        functools.partial(_outproj_kernel, tiles_per_mod=seq // tm),
        grid=(t // tm,),
        in_specs=[row, pl.BlockSpec((d, d), lambda i: (0, 0)), row,
                  pl.BlockSpec(mod.shape, lambda i: (0, 0)),
                  pl.BlockSpec(gains.shape, lambda i: (0, 0))],
        out_specs=[row, row],
        out_shape=[jax.ShapeDtypeStruct((t, d), F32), jax.ShapeDtypeStruct((t, d), BF16)],
        compiler_params=_params("arbitrary"),
        name="outproj",
    )(merged, wo, x, mod, gains)


def _mlp_kernel(h2_ref, w1_ref, w2_ref, x1_ref, mod_ref, g_ref, o_ref, *, tiles_per_mod):
    j = pl.program_id(1)

    @pl.when(j == 0)
    def _():
        o_ref[...] = jnp.zeros(o_ref.shape, F32)

    last = pl.num_programs(1) - 1

    def partial_sum(rows):
        hid = jnp.maximum(_dot(h2_ref[rows, :], w1_ref[...]), 0.0)
        return o_ref[rows, :] + _dot((hid * hid).astype(BF16), w2_ref[...])

    @pl.when(j < last)
    def _():
        o_ref[...] = partial_sum(slice(None))

    @pl.when(j == last)
    def _():
        b = _mod_row(pl.program_id(0), tiles_per_mod)
        gt2 = mod_ref[pl.ds(b, 1), 5 * D_MODEL:6 * D_MODEL]
        tm = o_ref.shape[0]
        sub = min(tm, OUT_SUB_ROWS)
        for r in range(0, tm, sub):
            rows = slice(r, r + sub)
            o_ref[rows, :] = x1_ref[rows, :] + gt2 * _rms(partial_sum(rows), g_ref[3:4, :])


def _mlp(h2, w1, w2, x1, mod, gains, *, seq, tm=1024, tf=MLP_TF):
    t, d = x1.shape
    f = w1.shape[1]
    row = pl.BlockSpec((tm, d), lambda i, j: (i, 0))
    return pl.pallas_call(
        functools.partial(_mlp_kernel, tiles_per_mod=seq // tm),
        grid=(t // tm, f // tf),
        in_specs=[row, pl.BlockSpec((d, tf), lambda i, j: (0, j)), pl.BlockSpec((tf, d), lambda i, j: (j, 0)),
                  row, pl.BlockSpec(mod.shape, lambda i, j: (0, 0)),
                  pl.BlockSpec(gains.shape, lambda i, j: (0, 0))],
        out_specs=row,
        out_shape=jax.ShapeDtypeStruct((t, d), F32),
        compiler_params=_params("arbitrary", "arbitrary"),
        name="mlp",
    )(h2, w1, w2, x1, mod, gains)


def _rope_tables(seq):
    rows = jnp.repeat(jnp.arange(seq // GRID_W), GRID_W)
    cols = jnp.tile(jnp.arange(GRID_W), seq // GRID_W)
    inv = ROPE_THETA ** (-jnp.arange(0, AXIS_DIM, 2, dtype=F32) / AXIS_DIM)
    ar = rows[:, None] * inv
    ac = cols[:, None] * inv
    cos = jnp.concatenate([jnp.cos(ar), jnp.cos(ar), jnp.cos(ac), jnp.cos(ac)], axis=-1)
    sin = jnp.concatenate([-jnp.sin(ar), jnp.sin(ar), -jnp.sin(ac), jnp.sin(ac)], axis=-1)
    return cos, sin


def _filter_embedding(seq):
    t = jnp.linspace(0.0, 1.0, seq, dtype=F32)[:, None]
    wpos = 2.0 * math.pi * jnp.arange(seq, dtype=F32)[:, None] / seq
    bands = jnp.linspace(1e-4, FILTER_BANDS - 1, FILTER_BANDS, dtype=F32)
    emb = jnp.concatenate([t, jnp.cos(bands * wpos), -jnp.sin(bands * wpos)], axis=-1)
    emb = jnp.pad(emb, ((0, 0), (0, EMB_PAD - FILTER_EMB)))
    emb_b = jnp.concatenate([emb[:1], emb[:0:-1]], axis=0)
    return jnp.stack([emb, emb_b])


def _block_diag2(w):
    z = jnp.zeros_like(w)
    return jnp.concatenate([jnp.concatenate([w, z], axis=1), jnp.concatenate([z, w], axis=1)], axis=0)


def _dft_tables(seq):
    idx = jnp.arange(seq, dtype=jnp.int32)
    phase = (idx[:, None] * idx[None, :]) & (2 * seq - 1)
    ang = phase.astype(F32) * (math.pi / seq)
    alt = (1 - 2 * (idx & 1)).astype(F32)
    cm = jnp.cos(ang)
    sn = jnp.sin(ang)
    sf = jnp.where(idx[:, None] == 0, alt[None, :], sn)
    si = jnp.where(idx[None, :] == 0, alt[:, None], sn)
    return cm.astype(BF16), sf.astype(BF16), si.astype(BF16)


def kernel(x, c, ctx, c_ctx, w_ada, b_ada, norm_gains, w_in, conv_w, conv_b, filt_w1, filt_b1, filt_w2, filt_b2, filt_w3, filt_b3, filt_w4, filt_freq, filt_bias, qk_gains, w_branch_a, w_branch_b, w_out, w_ff1, w_ff2):
    B, L, D = x.shape
    T = B * L
    C = HYENA_WIDTH
    lyr = 0
    gains = norm_gains[lyr]
    w_in_b = w_in[lyr].astype(BF16)

    cin = jnp.zeros((MOD_ROWS, D), F32).at[:B].set(c).at[B].set(c_ctx)
    mod = _adaln(cin, w_ada[lyr], b_ada[lyr][None])

    xf = x.reshape(T, D)

    cos, sin = _rope_tables(L)
    qg = qk_gains[lyr, 0][None]
    kg = qk_gains[lyr, 1][None]
    q_scale = HEAD_DIM ** -0.5 * math.log2(math.e)
    k_ctx, v_ctx = _ctx_kv(ctx, mod, gains[0:1], w_in_b, kg, seq=L, mod_row=B)
    h, u, q, k_all, v_all = _inproj(xf, mod, gains[0:1], w_in_b, qg, kg, cos, sin, k_ctx, v_ctx, seq=L,
                                    q_scale=q_scale)

    emb = _filter_embedding(L)
    w1p = jnp.pad(filt_w1[lyr], ((0, EMB_PAD - FILTER_EMB), (0, 0)))
    deltas = jnp.abs(jnp.linspace(MIN_DECAY, MAX_DECAY, C, dtype=F32))[None]
    emb2 = jnp.concatenate([emb[:, :L // 2], emb[:, L // 2:]], axis=-1)
    pair = lambda v: jnp.tile(v, 2)[None]
    w4 = filt_w4[lyr]
    w4_halves = jnp.stack([jnp.concatenate([w4, jnp.zeros_like(w4)]), jnp.concatenate([jnp.zeros_like(w4), w4])])
    hid = _filter_hidden(emb2, _block_diag2(w1p), pair(filt_b1[lyr]), _block_diag2(filt_w2[lyr]), pair(filt_b2[lyr]),
                         _block_diag2(filt_w3[lyr]), pair(filt_b3[lyr]), pair(filt_freq[lyr]))
    filt, norm = _hyena_filter(hid, emb, w4_halves, deltas)
    P = FFT_BLOCK
    cm, sf, si = _dft_tables(P)
    gr, gi = _filter_spectrum(cm, sf, filt.reshape(2 * L // P, P, C), norm)
    zr, zi, zb, x0 = _hyena_pre(u, conv_w[lyr], conv_b[lyr][None], filt_bias[lyr][None], cm, sf, seq=L)
    ya = _conv(cm, si, zr, zi, gr, gi, x0, zb, seq=L)

    yb = _attention(q.reshape(B, L, Q_W), k_all, v_all).reshape(T, Q_W)

    merged = _merge(h, ya, yb, w_in_b, w_branch_a[lyr].astype(BF16), w_branch_b[lyr].astype(BF16))
    x1, h2 = _outproj(merged, w_out[lyr].astype(BF16), xf, mod, gains, seq=L)
    out = _mlp(h2, w_ff1[lyr].astype(BF16), w_ff2[lyr].astype(BF16), x1, mod, gains, seq=L)
    return out.reshape(B, L, D)
```

```python
import functools
import math

import jax
import jax.numpy as jnp
from jax import lax
from jax.experimental import pallas as pl
from jax.experimental.pallas import tpu as pltpu

F32 = jnp.float32
BF16 = jnp.bfloat16

D_MODEL = 2048
CTX_LEN = 256
GRID_W = 64
N_HEADS = 16
HEAD_DIM = 128
N_KV_HEADS = 4
GROUP = N_HEADS // N_KV_HEADS
AXIS_DIM = HEAD_DIM // 2
ROPE_THETA = 10000.0
HYENA_WIDTH = D_MODEL // 2
FILTER_EMB = 17
FILTER_BANDS = (FILTER_EMB - 1) // 2
DECAY_TARGET = 1e-2
MIN_DECAY = math.log(DECAY_TARGET) / 1.5
MAX_DECAY = math.log(DECAY_TARGET) / 0.3
D_FF = 4 * D_MODEL
EPS = 1e-6
Q_W = N_HEADS * HEAD_DIM
KV_W = N_KV_HEADS * HEAD_DIM
Q_OFF = 3 * HYENA_WIDTH
K_OFF = Q_OFF + Q_W
V_OFF = K_OFF + KV_W
GA_OFF = V_OFF + KV_W
GB_OFF = GA_OFF + D_MODEL

SUBLANES = 8
VMEM_LIMIT = 60 * 1024 * 1024
EMB_PAD = 128
MOD_ROWS = 8
HIGHEST = lax.Precision.HIGHEST
ONES_ROWS = 16
IN_TN = KV_W
PRE_SUB_COLS = 256
MERGE_TN = 512
MLP_TF = 512
QK_SUB_ROWS = 256
OUT_SUB_ROWS = 128
FFT_BLOCK = 512


def _params(*sem):
    return pltpu.CompilerParams(dimension_semantics=sem, vmem_limit_bytes=VMEM_LIMIT)


def _dot(a, b):
    return jnp.dot(a, b, preferred_element_type=F32)


def _rms(x, g):
    return x * lax.rsqrt(jnp.mean(x * x, axis=-1, keepdims=True) + EPS) * g


def _adaln_kernel(c_ref, w_ref, b_ref, o_ref):
    c = c_ref[...]
    s = c * jax.nn.sigmoid(c)
    o_ref[...] = _dot(s.astype(BF16), w_ref[...].astype(BF16)) + b_ref[...]


def _adaln(cin, w, b, tn=1024):
    rows, d = cin.shape
    n = w.shape[1]
    return pl.pallas_call(
        _adaln_kernel,
        grid=(n // tn,),
        in_specs=[pl.BlockSpec((rows, d), lambda j: (0, 0)),
                  pl.BlockSpec((d, tn), lambda j: (0, j)),
                  pl.BlockSpec((1, tn), lambda j: (0, j))],
        out_specs=pl.BlockSpec((rows, tn), lambda j: (0, j)),
        out_shape=jax.ShapeDtypeStruct((rows, n), F32),
        compiler_params=_params("arbitrary"),
        name="adaln",
    )(cin, w, b)


def _mod_row(i, tiles_per_mod):
    return i // tiles_per_mod


def _modnorm(x, g, mod_ref, b):
    sh = mod_ref[pl.ds(b, 1), 0:D_MODEL]
    sc = mod_ref[pl.ds(b, 1), D_MODEL:2 * D_MODEL]
    return (_rms(x, g * (1.0 + sc)) + sh).astype(BF16)


def _qk_head_epilogue(acc, g, cos, sin, first, scale, store):
    for hh in range(acc.shape[1] // HEAD_DIM):
        y = _rms(acc[:, hh * HEAD_DIM:(hh + 1) * HEAD_DIM], g)
        if cos is not None:
            partner = jnp.where(first,
                                pltpu.roll(y, HEAD_DIM - AXIS_DIM // 2, 1),
                                pltpu.roll(y, AXIS_DIM // 2, 1))
            y = y * cos + partner * sin
        if scale != 1.0:
            y = y * scale
        store(hh, y.astype(BF16))


def _inproj_kernel(x_ref, mod_ref, g_ref, w_ref, qg_ref, kg_ref, cos_ref, sin_ref, kin_ref, vin_ref,
                   h_ref, u_ref, q_ref, k_ref, v_ref, *, tiles_per_mod, q_scale):
    del kin_ref, vin_ref
    tm = x_ref.shape[0]
    sub = min(tm, QK_SUB_ROWS)
    tn = IN_TN
    b = _mod_row(pl.program_id(0), tiles_per_mod)
    lane = lax.broadcasted_iota(jnp.int32, (sub, HEAD_DIM), 1)
    first = (lane & (AXIS_DIM // 2)) == 0

    def proj(hr, col):
        return _dot(hr, w_ref[:, col:col + tn])

    for r in range(0, tm, sub):
        rows = slice(r, r + sub)
        hr = _modnorm(x_ref[rows, :], g_ref[...], mod_ref, b)
        h_ref[rows, :] = hr
        for col in range(0, Q_OFF, tn):
            u_ref[rows, col:col + tn] = proj(hr, col)
        cos, sin = cos_ref[rows, :], sin_ref[rows, :]
        for col in range(0, Q_W, tn):
            def store_q(hh, y, col=col):
                q_ref[rows, col + hh * HEAD_DIM:col + (hh + 1) * HEAD_DIM] = y
            _qk_head_epilogue(proj(hr, Q_OFF + col), qg_ref[...], cos, sin, first, q_scale, store_q)

        def store_k(hh, y):
            k_ref[0, rows, hh * HEAD_DIM:(hh + 1) * HEAD_DIM] = y
        _qk_head_epilogue(proj(hr, K_OFF), kg_ref[...], cos, sin, first, 1.0, store_k)
        v_ref[0, rows, :] = proj(hr, V_OFF).astype(v_ref.dtype)


def _inproj(x, mod, g, w_in_b, qg, kg, cos, sin, k_ctx, v_ctx, *, seq, q_scale, tm=512):
    t, d = x.shape
    seq_tiles = seq // tm
    anyspace = pl.BlockSpec(memory_space=pl.ANY)
    row = lambda w: pl.BlockSpec((tm, w), lambda i: (i, 0))
    gain = pl.BlockSpec((1, HEAD_DIM), lambda i: (0, 0))
    table = pl.BlockSpec((tm, HEAD_DIM), lambda i: (i % seq_tiles, 0))
    kv = pl.BlockSpec((1, tm, KV_W), lambda i: (i // seq_tiles, i % seq_tiles, 0))
    return pl.pallas_call(
        functools.partial(_inproj_kernel, tiles_per_mod=seq_tiles, q_scale=q_scale),
        grid=(t // tm,),
        in_specs=[row(d), pl.BlockSpec(mod.shape, lambda i: (0, 0)), pl.BlockSpec((1, d), lambda i: (0, 0)),
                  pl.BlockSpec((d, GA_OFF), lambda i: (0, 0), pipeline_mode=pl.Buffered(1)),
                  gain, gain, table, table, anyspace, anyspace],
        out_specs=[row(d), row(Q_OFF), row(Q_W), kv, kv],
        out_shape=[jax.ShapeDtypeStruct((t, d), BF16),
                   jax.ShapeDtypeStruct((t, Q_OFF), F32),
                   jax.ShapeDtypeStruct((t, Q_W), BF16),
                   jax.ShapeDtypeStruct(k_ctx.shape, k_ctx.dtype),
                   jax.ShapeDtypeStruct(v_ctx.shape, v_ctx.dtype)],
        input_output_aliases={8: 3, 9: 4},
        compiler_params=_params("arbitrary"),
        name="inproj",
    )(x, mod, g, w_in_b, qg, kg, cos, sin, k_ctx, v_ctx)


def _ctx_kv_kernel(x_ref, mod_ref, g_ref, wk_ref, wv_ref, kg_ref, k_ref, v_ref, *, seq, mod_row):
    k_ref[0, 0:seq, :] = jnp.zeros((seq, KV_W), k_ref.dtype)
    v_ref[0, 0:seq, :] = jnp.zeros((seq, KV_W), v_ref.dtype)
    hc = _modnorm(x_ref[0], g_ref[...], mod_ref, mod_row)

    def store(hh, y):
        k_ref[0, seq:, hh * HEAD_DIM:(hh + 1) * HEAD_DIM] = y
    _qk_head_epilogue(_dot(hc, wk_ref[...]), kg_ref[...], None, None, None, 1.0, store)
    v_ref[0, seq:, :] = _dot(hc, wv_ref[...]).astype(v_ref.dtype)


def _ctx_kv(ctx, mod, g, w_in_b, kg, *, seq, mod_row):
    nb, lc, d = ctx.shape
    out = pl.BlockSpec((1, seq + lc, KV_W), lambda b: (b, 0, 0))
    shape = jax.ShapeDtypeStruct((nb, seq + lc, KV_W), BF16)
    return pl.pallas_call(
        functools.partial(_ctx_kv_kernel, seq=seq, mod_row=mod_row),
        grid=(nb,),
        in_specs=[pl.BlockSpec((1, lc, d), lambda b: (b, 0, 0)),
                  pl.BlockSpec(mod.shape, lambda b: (0, 0)), pl.BlockSpec((1, d), lambda b: (0, 0)),
                  pl.BlockSpec((d, KV_W), lambda b: (0, K_OFF // KV_W)),
                  pl.BlockSpec((d, KV_W), lambda b: (0, V_OFF // KV_W)),
                  pl.BlockSpec((1, HEAD_DIM), lambda b: (0, 0))],
        out_specs=[out, out],
        out_shape=[shape, shape],
        compiler_params=_params("arbitrary"),
        name="ctx_kv",
    )(ctx, mod, g, w_in_b, w_in_b, kg)


def _dot_hi(a, b):
    return jnp.dot(a, b, precision=HIGHEST, preferred_element_type=F32)


def _filter_hidden_kernel(emb_ref, w1_ref, b1_ref, w2_ref, b2_ref, w3_ref, b3_ref, fr_ref, o_ref):
    fr = fr_ref[...]
    h = jnp.sin(fr * (_dot_hi(emb_ref[0], w1_ref[...]) + b1_ref[...]))
    h = jnp.sin(fr * (_dot_hi(h, w2_ref[...]) + b2_ref[...]))
    o_ref[0] = jnp.sin(fr * (_dot_hi(h, w3_ref[...]) + b3_ref[...]))


def _filter_hidden(emb, w1, b1, w2, b2, w3, b3, fr, *, tr=256):
    sides, rows, e = emb.shape
    width = w3.shape[1]
    full = lambda a: pl.BlockSpec(a.shape, lambda s, r: (0,) * a.ndim)
    return pl.pallas_call(
        _filter_hidden_kernel,
        grid=(sides, rows // tr),
        in_specs=[pl.BlockSpec((1, tr, e), lambda s, r: (s, r, 0)),
                  full(w1), full(b1), full(w2), full(b2), full(w3), full(b3), full(fr)],
        out_specs=pl.BlockSpec((1, tr, width), lambda s, r: (s, r, 0)),
        out_shape=jax.ShapeDtypeStruct((sides, rows, width), F32),
        compiler_params=_params("arbitrary", "arbitrary"),
        name="filter_hidden",
    )(emb, w1, b1, w2, b2, w3, b3, fr)


def _filter_kernel(hid_ref, emb_ref, w4f_ref, w4b_ref, dl_ref, k_ref, norm_ref):
    r = pl.program_id(1)
    dl = dl_ref[...]
    kf = _dot_hi(hid_ref[0], w4f_ref[0]) * jnp.exp(-emb_ref[0, :, 0:1] * dl)
    kb = _dot_hi(hid_ref[1], w4b_ref[0]) * jnp.exp(-emb_ref[1, :, 0:1] * dl)
    tr = kb.shape[0]
    row = r * tr + lax.broadcasted_iota(jnp.int32, kb.shape, 0)
    kb = jnp.where(row == 0, 0.0, kb)
    k_ref[0] = kb.astype(k_ref.dtype)
    k_ref[1] = kf.astype(k_ref.dtype)

    @pl.when(r == 0)
    def _():
        norm_ref[...] = jnp.zeros(norm_ref.shape, F32)

    norm_ref[...] += jnp.sum(jnp.abs(kf), axis=0, keepdims=True) + jnp.sum(jnp.abs(kb), axis=0, keepdims=True)


def _hyena_filter(hid, emb, w4, deltas, *, tr=512, tc=256):
    _, L, e = emb.shape
    C = deltas.shape[1]
    half_tiles = L // tr // 2
    width = hid.shape[2]
    return pl.pallas_call(
        _filter_kernel,
        grid=(C // tc, L // tr),
        in_specs=[pl.BlockSpec((2, tr, width), lambda c, r: (0, r % half_tiles, 0)),
                  pl.BlockSpec((2, tr, e), lambda c, r: (0, r, 0)),
                  pl.BlockSpec((1, width, tc), lambda c, r: (r // half_tiles, 0, c)),
                  pl.BlockSpec((1, width, tc), lambda c, r: (r // half_tiles, 0, c + C // tc)),
                  pl.BlockSpec((1, tc), lambda c, r: (0, c))],
        out_specs=[pl.BlockSpec((2, tr, tc), lambda c, r: (0, r, c)),
                   pl.BlockSpec((1, tc), lambda c, r: (0, c))],
        out_shape=[jax.ShapeDtypeStruct((2, L, C), BF16), jax.ShapeDtypeStruct((1, C), F32)],
        compiler_params=_params("arbitrary", "arbitrary"),
        name="hyena_filter",
    )(hid, emb, w4, w4, deltas)


def _hyena_pre_kernel(x0m, x0p, x0n, x1m, x1p, x1n, vm, vp, vn, w0, w1, w2, b0, b1, b2, fb, cm_ref, sf_ref,
                      zr_ref, zi_ref, zb_ref, x0_ref, *, tiles_per_seq):
    pos = pl.program_id(0) % tiles_per_seq
    first = pos == 0
    last = pos == tiles_per_seq - 1
    tr, tc = x0m.shape
    sub = min(tc, PRE_SUB_COLS)
    row = lax.broadcasted_iota(jnp.int32, (tr, sub), 0)

    def conv(m_ref, p_ref, n_ref, w_ref, b_ref, cols):
        u = m_ref[:, cols]
        prev = jnp.where(first, 0.0, p_ref[SUBLANES - 1:SUBLANES, cols])
        nxt = jnp.where(last, 0.0, n_ref[0:1, cols])
        um = jnp.where(row == 0, prev, pltpu.roll(u, 1, 0))
        up = jnp.where(row == tr - 1, nxt, pltpu.roll(u, tr - 1, 0))
        w = w_ref[:, cols]
        return um * w[0:1] + u * w[1:2] + up * w[2:3] + b_ref[:, cols]

    for c in range(0, tc, sub):
        cols = slice(c, c + sub)
        z = conv(x1m, x1p, x1n, w1, b1, cols) * conv(vm, vp, vn, w2, b2, cols)
        zb16 = z.astype(BF16)
        zr_ref[0, :, cols] = _dot(cm_ref[...], zb16).astype(zr_ref.dtype)
        zi_ref[0, :, cols] = _dot(sf_ref[...], zb16).astype(zi_ref.dtype)
        zb_ref[:, cols] = (z * fb[:, cols]).astype(zb_ref.dtype)
        x0_ref[:, cols] = conv(x0m, x0p, x0n, w0, b0, cols).astype(x0_ref.dtype)


def _hyena_pre(u, conv_w, conv_b, fbias, cm, sf, *, seq, tc=512):
    t = u.shape[0]
    C = HYENA_WIDTH
    tr = cm.shape[0]
    cb = C // tc
    rb = tr // SUBLANES
    last_rb = t // SUBLANES - 1
    in_specs, args = [], []
    for part in range(3):
        off = part * cb
        in_specs += [pl.BlockSpec((tr, tc), lambda i, j, off=off: (i, j + off)),
                     pl.BlockSpec((SUBLANES, tc), lambda i, j, off=off: (jnp.maximum(i * rb - 1, 0), j + off)),
                     pl.BlockSpec((SUBLANES, tc), lambda i, j, off=off: (jnp.minimum((i + 1) * rb, last_rb), j + off))]
        args += [u, u, u]
    for part in range(3):
        in_specs.append(pl.BlockSpec((3, tc), lambda i, j, off=part * cb: (0, j + off)))
        args.append(conv_w)
    for part in range(3):
        in_specs.append(pl.BlockSpec((1, tc), lambda i, j, off=part * cb: (0, j + off)))
        args.append(conv_b)
    in_specs.append(pl.BlockSpec((1, tc), lambda i, j: (0, j)))
    args.append(fbias)
    in_specs += [pl.BlockSpec((tr, tr), lambda i, j: (0, 0))] * 2
    args += [cm, sf]
    out_spec = pl.BlockSpec((tr, tc), lambda i, j: (i, j))
    spec_spec = pl.BlockSpec((1, tr, tc), lambda i, j: (i, 0, j))
    return pl.pallas_call(
        functools.partial(_hyena_pre_kernel, tiles_per_seq=seq // tr),
        grid=(t // tr, cb),
        in_specs=in_specs,
        out_specs=[spec_spec, spec_spec, out_spec, out_spec],
        out_shape=[jax.ShapeDtypeStruct((t // tr, tr, C), BF16),
                   jax.ShapeDtypeStruct((t // tr, tr, C), BF16),
                   jax.ShapeDtypeStruct((t, C), BF16),
                   jax.ShapeDtypeStruct((t, C), BF16)],
        compiler_params=_params("arbitrary", "arbitrary"),
        name="hyena_pre",
    )(*args)


def _filter_spectrum_kernel(cm_ref, sf_ref, seg_ref, norm_ref, gr_ref, gi_ref, pr_sc, pi_sc):
    @pl.when(pl.program_id(1) == 0)
    def _():
        pr_sc[...] = jnp.zeros(pr_sc.shape, F32)
        pi_sc[...] = jnp.zeros(pi_sc.shape, F32)

    p, tc = pr_sc.shape
    sub = min(tc, PRE_SUB_COLS)
    freq = lax.broadcasted_iota(jnp.int32, (p, sub), 0)
    sign = (1 - 2 * (freq & 1)).astype(F32)
    weight = jnp.where(freq == 0, 0.5 / p, 1.0 / p)
    for c in range(0, tc, sub):
        cols = slice(c, c + sub)
        seg = seg_ref[0, :, cols]
        sr = _dot(cm_ref[...], seg)
        si = _dot(sf_ref[...], seg)
        scale = weight / norm_ref[:, cols]
        gr_ref[0, :, cols] = (sr + sign * pr_sc[:, cols]) * scale
        gi_ref[0, :, cols] = (si + sign * pi_sc[:, cols]) * scale
        pr_sc[:, cols] = sr
        pi_sc[:, cols] = si


def _filter_spectrum(cm, sf, taps, norm, *, tc=1024):
    nseg, P, C = taps.shape
    mat = pl.BlockSpec((P, P), lambda c, d: (0, 0))
    out = pl.BlockSpec((1, P, tc), lambda c, d: (jnp.maximum(d - 1, 0), 0, c))
    return pl.pallas_call(
        _filter_spectrum_kernel,
        grid=(C // tc, nseg),
        in_specs=[mat, mat,
                  pl.BlockSpec((1, P, tc), lambda c, d: (d, 0, c)),
                  pl.BlockSpec((1, tc), lambda c, d: (0, c))],
        out_specs=[out, out],
        out_shape=[jax.ShapeDtypeStruct((nseg - 1, P, C), F32)] * 2,
        scratch_shapes=[pltpu.VMEM((P, tc), F32), pltpu.VMEM((P, tc), F32)],
        compiler_params=_params("arbitrary", "arbitrary"),
        name="dft_filter",
    )(cm, sf, taps, norm)


def _conv_kernel(cm_ref, si_ref, zr_ref, zi_ref, gr_ref, gi_ref, x0_ref, zb_ref, o_ref, yr_sc, yi_sc, *, rows):
    nblk, P, tc = zr_ref.shape
    dc = lax.broadcasted_iota(jnp.int32, (rows, tc), 0) == 0

    def out_block(i, carry):
        for r in range(0, P, rows):
            rr = ii = ri = ir = None
            for j in range(nblk):
                lag = i - j + (nblk - 1)
                gr = gr_ref[lag, r:r + rows, :]
                gi = gi_ref[lag, r:r + rows, :]
                zr = zr_ref[j, r:r + rows, :].astype(F32)
                zi = zi_ref[j, r:r + rows, :].astype(F32)
                if j == 0:
                    rr, ii, ri, ir = gr * zr, gi * zi, gr * zi, gi * zr
                else:
                    rr, ii, ri, ir = rr + gr * zr, ii + gi * zi, ri + gr * zi, ir + gi * zr
            if r == 0:
                yr = jnp.where(dc, rr, rr - ii)
                yi = jnp.where(dc, ii, ri + ir)
            else:
                yr = rr - ii
                yi = ri + ir
            yr_sc[r:r + rows, :] = yr.astype(yr_sc.dtype)
            yi_sc[r:r + rows, :] = yi.astype(yi_sc.dtype)
        y = _dot(cm_ref[...], yr_sc[...]) + _dot(si_ref[...], yi_sc[...])
        t0 = pl.multiple_of(i * P, P)
        x0 = x0_ref[pl.ds(t0, P), :].astype(F32)
        o_ref[pl.ds(t0, P), :] = (x0 * (y + zb_ref[pl.ds(t0, P), :].astype(F32))).astype(o_ref.dtype)
        return carry

    lax.fori_loop(0, nblk, out_block, 0)


def _conv(cm, si, zr, zi, gr, gi, x0, zb, *, seq, tc=256, rows=32):
    P = cm.shape[0]
    nblk = seq // P
    nlag = gr.shape[0]
    t, C = x0.shape
    mat = pl.BlockSpec((P, P), lambda c, b: (0, 0))
    zspec = pl.BlockSpec((nblk, P, tc), lambda c, b: (b, 0, c))
    gspec = pl.BlockSpec((nlag, P, tc), lambda c, b: (0, 0, c))
    tile = pl.BlockSpec((seq, tc), lambda c, b: (b, c))
    return pl.pallas_call(
        functools.partial(_conv_kernel, rows=rows),
        grid=(C // tc, t // seq),
        in_specs=[mat, mat, zspec, zspec, gspec, gspec, tile, tile],
        out_specs=tile,
        out_shape=jax.ShapeDtypeStruct((t, C), BF16),
        scratch_shapes=[pltpu.VMEM((P, tc), BF16), pltpu.VMEM((P, tc), BF16)],
        compiler_params=_params("arbitrary", "arbitrary"),
        name="hyena_conv",
    )(cm, si, zr, zi, gr, gi, x0, zb)


def _attn_kernel(q_ref, k_ref, v_ref, o_ref, vt_sc, *, tq, chunks):
    @pl.when(pl.program_id(2) == 0)
    def _():
        vt_sc[0:HEAD_DIM, :] = v_ref[0].T
        vt_sc[HEAD_DIM:, :] = jnp.ones((vt_sc.shape[0] - HEAD_DIM, vt_sc.shape[1]), vt_sc.dtype)

    qs = jnp.concatenate([q_ref[0, :, g * HEAD_DIM:(g + 1) * HEAD_DIM] for g in range(GROUP)], axis=0)
    n = GROUP * tq
    m = jnp.full((1, n), -jnp.inf, F32)
    acc = jnp.zeros((vt_sc.shape[0], n), F32)

    def scores(c):
        start, size = chunks[c]
        return lax.dot_general(k_ref[0, start:start + size, :], qs, (((1,), (1,)), ((), ())),
                               preferred_element_type=F32)

    st_next = scores(0)
    for c, (start, size) in enumerate(chunks):
        st = st_next
        if c + 1 < len(chunks):
            st_next = scores(c + 1)
        m_new = jnp.maximum(m, jnp.max(st, axis=0, keepdims=True))
        alpha = jnp.exp2(m - m_new)
        p = jnp.exp2(st - m_new).astype(BF16)
        acc = alpha * acc + _dot(vt_sc[:, start:start + size], p)
        m = m_new
    out = acc[0:HEAD_DIM] * (1.0 / acc[HEAD_DIM:HEAD_DIM + 1])
    for g in range(GROUP):
        o_ref[0, :, g * HEAD_DIM:(g + 1) * HEAD_DIM] = out[:, g * tq:(g + 1) * tq].T.astype(o_ref.dtype)


def _attention(q, k, v, *, tq=512, tk=512):
    B, L, _ = q.shape
    S = k.shape[1]
    gw = GROUP * HEAD_DIM
    kv_spec = pl.BlockSpec((1, S, HEAD_DIM), lambda b, h, i: (b, 0, h))
    q_spec = pl.BlockSpec((1, tq, gw), lambda b, h, i: (b, i, h))
    chunks = tuple((c, min(tk, S - c)) for c in range(0, S, tk))
    return pl.pallas_call(
        functools.partial(_attn_kernel, tq=tq, chunks=chunks),
        grid=(B, N_KV_HEADS, L // tq),
        in_specs=[q_spec, kv_spec, kv_spec],
        out_specs=q_spec,
        out_shape=jax.ShapeDtypeStruct((B, L, Q_W), BF16),
        scratch_shapes=[pltpu.VMEM((HEAD_DIM + ONES_ROWS, S), BF16)],
        compiler_params=_params("arbitrary", "arbitrary", "arbitrary"),
        name="attention",
    )(q, k, v)


def _merge_kernel(h_ref, ya_ref, yb_ref, wga_ref, wgb_ref, wba_ref, wbb_ref, o_ref):
    h = h_ref[...]
    ya = ya_ref[...]
    yb = yb_ref[...]
    tn = MERGE_TN
    for col in range(0, o_ref.shape[1], tn):
        cols = slice(col, col + tn)
        ga = jax.nn.sigmoid(_dot(h, wga_ref[:, cols]))
        gb = jax.nn.sigmoid(_dot(h, wgb_ref[:, cols]))
        o_ref[:, cols] = (ga * _dot(ya, wba_ref[:, cols]) + gb * _dot(yb, wbb_ref[:, cols])).astype(o_ref.dtype)


def _merge(h, ya, yb, w_in_b, wba, wbb, *, tm=512):
    t, d = h.shape
    row = lambda w: pl.BlockSpec((tm, w), lambda i: (i, 0))
    resident = lambda kdim, blk=0: pl.BlockSpec((kdim, d), lambda i: (0, blk), pipeline_mode=pl.Buffered(1))
    return pl.pallas_call(
        _merge_kernel,
        grid=(t // tm,),
        in_specs=[row(d), row(ya.shape[1]), row(yb.shape[1]),
                  resident(d, GA_OFF // d), resident(d, GB_OFF // d), resident(ya.shape[1]), resident(yb.shape[1])],
        out_specs=row(d),
        out_shape=jax.ShapeDtypeStruct((t, d), BF16),
        compiler_params=_params("arbitrary"),
        name="merge",
    )(h, ya, yb, w_in_b, w_in_b, wba, wbb)


def _outproj_kernel(m_ref, wo_ref, x_ref, mod_ref, g_ref, x1_ref, h2_ref, *, tiles_per_mod):
    b = _mod_row(pl.program_id(0), tiles_per_mod)
    gt1 = mod_ref[pl.ds(b, 1), 2 * D_MODEL:3 * D_MODEL]
    sh2 = mod_ref[pl.ds(b, 1), 3 * D_MODEL:4 * D_MODEL]
    sc2 = mod_ref[pl.ds(b, 1), 4 * D_MODEL:5 * D_MODEL]
    tm = m_ref.shape[0]
    sub = min(tm, OUT_SUB_ROWS)
    g1 = gt1 * g_ref[1:2, :]
    g2 = g_ref[2:3, :] * (1.0 + sc2)
    for r in range(0, tm, sub):
        mix = _dot(m_ref[r:r + sub, :], wo_ref[...])
        x1 = x_ref[r:r + sub, :] + _rms(mix, g1)
        x1_ref[r:r + sub, :] = x1
        h2_ref[r:r + sub, :] = (_rms(x1, g2) + sh2).astype(h2_ref.dtype)


def _outproj(merged, wo, x, mod, gains, *, seq, tm=512):
    t, d = x.shape
    row = pl.BlockSpec((tm, d), lambda i: (i, 0))
    return pl.pallas_call(
        functools.partial(_outproj_kernel, tiles_per_mod=seq // tm),
        grid=(t // tm,),
        in_specs=[row, pl.BlockSpec((d, d), lambda i: (0, 0)), row,
                  pl.BlockSpec(mod.shape, lambda i: (0, 0)),
                  pl.BlockSpec(gains.shape, lambda i: (0, 0))],
        out_specs=[row, row],
        out_shape=[jax.ShapeDtypeStruct((t, d), F32), jax.ShapeDtypeStruct((t, d), BF16)],
        compiler_params=_params("arbitrary"),
        name="outproj",
    )(merged, wo, x, mod, gains)


def _mlp_kernel(h2_ref, w1_ref, w2_ref, x1_ref, mod_ref, g_ref, o_ref, *, tiles_per_mod):
    j = pl.program_id(1)

    @pl.when(j == 0)
    def _():
        o_ref[...] = jnp.zeros(o_ref.shape, F32)

    last = pl.num_programs(1) - 1

    def partial_sum(rows):
        hid = jnp.maximum(_dot(h2_ref[rows, :], w1_ref[...]), 0.0)
        return o_ref[rows, :] + _dot((hid * hid).astype(BF16), w2_ref[...])

    @pl.when(j < last)
    def _():
        o_ref[...] = partial_sum(slice(None))

    @pl.when(j == last)
    def _():
        b = _mod_row(pl.program_id(0), tiles_per_mod)
        gated_gain = mod_ref[pl.ds(b, 1), 5 * D_MODEL:6 * D_MODEL] * g_ref[3:4, :]
        tm = o_ref.shape[0]
        sub = min(tm, OUT_SUB_ROWS)
        for r in range(0, tm, sub):
            rows = slice(r, r + sub)
            o_ref[rows, :] = x1_ref[rows, :] + _rms(partial_sum(rows), gated_gain)


def _mlp(h2, w1, w2, x1, mod, gains, *, seq, tm=1024, tf=MLP_TF):
    t, d = x1.shape
    f = w1.shape[1]
    row = pl.BlockSpec((tm, d), lambda i, j: (i, 0))
    return pl.pallas_call(
        functools.partial(_mlp_kernel, tiles_per_mod=seq // tm),
        grid=(t // tm, f // tf),
        in_specs=[row, pl.BlockSpec((d, tf), lambda i, j: (0, j)), pl.BlockSpec((tf, d), lambda i, j: (j, 0)),
                  row, pl.BlockSpec(mod.shape, lambda i, j: (0, 0)),
                  pl.BlockSpec(gains.shape, lambda i, j: (0, 0))],
        out_specs=row,
        out_shape=jax.ShapeDtypeStruct((t, d), F32),
        compiler_params=_params("arbitrary", "arbitrary"),
        name="mlp",
    )(h2, w1, w2, x1, mod, gains)


def _rope_tables(seq):
    rows = jnp.repeat(jnp.arange(seq // GRID_W), GRID_W)
    cols = jnp.tile(jnp.arange(GRID_W), seq // GRID_W)
    inv = ROPE_THETA ** (-jnp.arange(0, AXIS_DIM, 2, dtype=F32) / AXIS_DIM)
    ar = rows[:, None] * inv
    ac = cols[:, None] * inv
    cos = jnp.concatenate([jnp.cos(ar), jnp.cos(ar), jnp.cos(ac), jnp.cos(ac)], axis=-1)
    sin = jnp.concatenate([-jnp.sin(ar), jnp.sin(ar), -jnp.sin(ac), jnp.sin(ac)], axis=-1)
    return cos, sin


def _filter_embedding(seq):
    t = jnp.linspace(0.0, 1.0, seq, dtype=F32)[:, None]
    wpos = 2.0 * math.pi * jnp.arange(seq, dtype=F32)[:, None] / seq
    bands = jnp.linspace(1e-4, FILTER_BANDS - 1, FILTER_BANDS, dtype=F32)
    emb = jnp.concatenate([t, jnp.cos(bands * wpos), -jnp.sin(bands * wpos)], axis=-1)
    emb = jnp.pad(emb, ((0, 0), (0, EMB_PAD - FILTER_EMB)))
    emb_b = jnp.concatenate([emb[:1], emb[:0:-1]], axis=0)
    return jnp.stack([emb, emb_b])


def _block_diag2(w):
    z = jnp.zeros_like(w)
    return jnp.concatenate([jnp.concatenate([w, z], axis=1), jnp.concatenate([z, w], axis=1)], axis=0)


def _dft_tables(seq):
    idx = jnp.arange(seq, dtype=jnp.int32)
    phase = (idx[:, None] * idx[None, :]) & (2 * seq - 1)
    ang = phase.astype(F32) * (math.pi / seq)
    alt = (1 - 2 * (idx & 1)).astype(F32)
    cm = jnp.cos(ang)
    sn = jnp.sin(ang)
    sf = jnp.where(idx[:, None] == 0, alt[None, :], sn)
    si = jnp.where(idx[None, :] == 0, alt[:, None], sn)
    return cm.astype(BF16), sf.astype(BF16), si.astype(BF16)


def kernel(x, c, ctx, c_ctx, w_ada, b_ada, norm_gains, w_in, conv_w, conv_b, filt_w1, filt_b1, filt_w2, filt_b2, filt_w3, filt_b3, filt_w4, filt_freq, filt_bias, qk_gains, w_branch_a, w_branch_b, w_out, w_ff1, w_ff2):
    B, L, D = x.shape
    T = B * L
    C = HYENA_WIDTH
    lyr = 0
    gains = norm_gains[lyr]
    w_in_b = w_in[lyr].astype(BF16)

    cin = jnp.zeros((MOD_ROWS, D), F32).at[:B].set(c).at[B].set(c_ctx)
    mod = _adaln(cin, w_ada[lyr], b_ada[lyr][None])

    xf = x.reshape(T, D)

    cos, sin = _rope_tables(L)
    qg = qk_gains[lyr, 0][None]
    kg = qk_gains[lyr, 1][None]
    q_scale = HEAD_DIM ** -0.5 * math.log2(math.e)
    k_ctx, v_ctx = _ctx_kv(ctx, mod, gains[0:1], w_in_b, kg, seq=L, mod_row=B)
    h, u, q, k_all, v_all = _inproj(xf, mod, gains[0:1], w_in_b, qg, kg, cos, sin, k_ctx, v_ctx, seq=L,
                                    q_scale=q_scale)

    emb = _filter_embedding(L)
    w1p = jnp.pad(filt_w1[lyr], ((0, EMB_PAD - FILTER_EMB), (0, 0)))
    deltas = jnp.abs(jnp.linspace(MIN_DECAY, MAX_DECAY, C, dtype=F32))[None]
    emb2 = jnp.concatenate([emb[:, :L // 2], emb[:, L // 2:]], axis=-1)
    pair = lambda v: jnp.tile(v, 2)[None]
    w4 = filt_w4[lyr]
    w4_halves = jnp.stack([jnp.concatenate([w4, jnp.zeros_like(w4)]), jnp.concatenate([jnp.zeros_like(w4), w4])])
    hid = _filter_hidden(emb2, _block_diag2(w1p), pair(filt_b1[lyr]), _block_diag2(filt_w2[lyr]), pair(filt_b2[lyr]),
                         _block_diag2(filt_w3[lyr]), pair(filt_b3[lyr]), pair(filt_freq[lyr]))
    filt, norm = _hyena_filter(hid, emb, w4_halves, deltas)
    P = FFT_BLOCK
    cm, sf, si = _dft_tables(P)
    gr, gi = _filter_spectrum(cm, sf, filt.reshape(2 * L // P, P, C), norm)
    zr, zi, zb, x0 = _hyena_pre(u, conv_w[lyr], conv_b[lyr][None], filt_bias[lyr][None], cm, sf, seq=L)
    ya = _conv(cm, si, zr, zi, gr, gi, x0, zb, seq=L)

    yb = _attention(q.reshape(B, L, Q_W), k_all, v_all).reshape(T, Q_W)

    merged = _merge(h, ya, yb, w_in_b, w_branch_a[lyr].astype(BF16), w_branch_b[lyr].astype(BF16))
    x1, h2 = _outproj(merged, w_out[lyr].astype(BF16), xf, mod, gains, seq=L)
    out = _mlp(h2, w_ff1[lyr].astype(BF16), w_ff2[lyr].astype(BF16), x1, mod, gains, seq=L)
    return out.reshape(B, L, D)
```

```python
import functools
import math

import jax
import jax.numpy as jnp
from jax import lax
from jax.experimental import pallas as pl
from jax.experimental.pallas import tpu as pltpu

F32 = jnp.float32
BF16 = jnp.bfloat16

D_MODEL = 2048
CTX_LEN = 256
GRID_W = 64
N_HEADS = 16
HEAD_DIM = 128
N_KV_HEADS = 4
GROUP = N_HEADS // N_KV_HEADS
AXIS_DIM = HEAD_DIM // 2
ROPE_THETA = 10000.0
HYENA_WIDTH = D_MODEL // 2
FILTER_EMB = 17
FILTER_BANDS = (FILTER_EMB - 1) // 2
DECAY_TARGET = 1e-2
MIN_DECAY = math.log(DECAY_TARGET) / 1.5
MAX_DECAY = math.log(DECAY_TARGET) / 0.3
D_FF = 4 * D_MODEL
EPS = 1e-6
Q_W = N_HEADS * HEAD_DIM
KV_W = N_KV_HEADS * HEAD_DIM
Q_OFF = 3 * HYENA_WIDTH
K_OFF = Q_OFF + Q_W
V_OFF = K_OFF + KV_W
GA_OFF = V_OFF + KV_W
GB_OFF = GA_OFF + D_MODEL

SUBLANES = 8
VMEM_LIMIT = 60 * 1024 * 1024
EMB_PAD = 128
MOD_ROWS = 8
HIGHEST = lax.Precision.HIGHEST
ONES_ROWS = 16
IN_TN = KV_W
PRE_SUB_COLS = 256
MERGE_TN = 512
MLP_TF = 512
QK_SUB_ROWS = 256
OUT_SUB_ROWS = 128
FFT_BLOCK = 512


def _params(*sem):
    return pltpu.CompilerParams(dimension_semantics=sem, vmem_limit_bytes=VMEM_LIMIT)


def _dot(a, b):
    return jnp.dot(a, b, preferred_element_type=F32)


def _rms(x, g):
    return x * lax.rsqrt(jnp.mean(x * x, axis=-1, keepdims=True) + EPS) * g


def _adaln_kernel(c_ref, w_ref, b_ref, o_ref):
    c = c_ref[...]
    s = c * jax.nn.sigmoid(c)
    o_ref[...] = _dot(s.astype(BF16), w_ref[...].astype(BF16)) + b_ref[...]


def _adaln(cin, w, b, tn=1024):
    rows, d = cin.shape
    n = w.shape[1]
    return pl.pallas_call(
        _adaln_kernel,
        grid=(n // tn,),
        in_specs=[pl.BlockSpec((rows, d), lambda j: (0, 0)),
                  pl.BlockSpec((d, tn), lambda j: (0, j)),
                  pl.BlockSpec((1, tn), lambda j: (0, j))],
        out_specs=pl.BlockSpec((rows, tn), lambda j: (0, j)),
        out_shape=jax.ShapeDtypeStruct((rows, n), F32),
        compiler_params=_params("arbitrary"),
        name="adaln",
    )(cin, w, b)


def _mod_row(i, tiles_per_mod):
    return i // tiles_per_mod


def _modnorm(x, g, mod_ref, b):
    sh = mod_ref[pl.ds(b, 1), 0:D_MODEL]
    sc = mod_ref[pl.ds(b, 1), D_MODEL:2 * D_MODEL]
    return (_rms(x, g * (1.0 + sc)) + sh).astype(BF16)


def _qk_head_epilogue(acc, g, cos, sin, first, scale, store):
    for hh in range(acc.shape[1] // HEAD_DIM):
        y = _rms(acc[:, hh * HEAD_DIM:(hh + 1) * HEAD_DIM], g)
        if cos is not None:
            partner = jnp.where(first,
                                pltpu.roll(y, HEAD_DIM - AXIS_DIM // 2, 1),
                                pltpu.roll(y, AXIS_DIM // 2, 1))
            y = y * cos + partner * sin
        if scale != 1.0:
            y = y * scale
        store(hh, y.astype(BF16))


def _inproj_kernel(x_ref, mod_ref, g_ref, w_ref, qg_ref, kg_ref, cos_ref, sin_ref, kin_ref, vin_ref,
                   h_ref, u_ref, q_ref, k_ref, v_ref, *, tiles_per_mod, q_scale):
    del kin_ref, vin_ref
    tm = x_ref.shape[0]
    sub = min(tm, QK_SUB_ROWS)
    tn = IN_TN
    b = _mod_row(pl.program_id(0), tiles_per_mod)
    lane = lax.broadcasted_iota(jnp.int32, (sub, HEAD_DIM), 1)
    first = (lane & (AXIS_DIM // 2)) == 0

    def proj(hr, col):
        return _dot(hr, w_ref[:, col:col + tn])

    for r in range(0, tm, sub):
        rows = slice(r, r + sub)
        hr = _modnorm(x_ref[rows, :], g_ref[...], mod_ref, b)
        h_ref[rows, :] = hr
        for col in range(0, Q_OFF, tn):
            u_ref[rows, col:col + tn] = proj(hr, col)
        cos, sin = cos_ref[rows, :], sin_ref[rows, :]
        for col in range(0, Q_W, tn):
            def store_q(hh, y, col=col):
                q_ref[rows, col + hh * HEAD_DIM:col + (hh + 1) * HEAD_DIM] = y
            _qk_head_epilogue(proj(hr, Q_OFF + col), qg_ref[...], cos, sin, first, q_scale, store_q)

        def store_k(hh, y):
            k_ref[0, rows, hh * HEAD_DIM:(hh + 1) * HEAD_DIM] = y
        _qk_head_epilogue(proj(hr, K_OFF), kg_ref[...], cos, sin, first, 1.0, store_k)
        v_ref[0, rows, :] = proj(hr, V_OFF).astype(v_ref.dtype)


def _inproj(x, mod, g, w_in_b, qg, kg, cos, sin, k_ctx, v_ctx, *, seq, q_scale, tm=512):
    t, d = x.shape
    seq_tiles = seq // tm
    anyspace = pl.BlockSpec(memory_space=pl.ANY)
    row = lambda w: pl.BlockSpec((tm, w), lambda i: (i, 0))
    gain = pl.BlockSpec((1, HEAD_DIM), lambda i: (0, 0))
    table = pl.BlockSpec((tm, HEAD_DIM), lambda i: (i % seq_tiles, 0))
    kv = pl.BlockSpec((1, tm, KV_W), lambda i: (i // seq_tiles, i % seq_tiles, 0))
    return pl.pallas_call(
        functools.partial(_inproj_kernel, tiles_per_mod=seq_tiles, q_scale=q_scale),
        grid=(t // tm,),
        in_specs=[row(d), pl.BlockSpec(mod.shape, lambda i: (0, 0)), pl.BlockSpec((1, d), lambda i: (0, 0)),
                  pl.BlockSpec((d, GA_OFF), lambda i: (0, 0), pipeline_mode=pl.Buffered(1)),
                  gain, gain, table, table, anyspace, anyspace],
        out_specs=[row(d), row(Q_OFF), row(Q_W), kv, kv],
        out_shape=[jax.ShapeDtypeStruct((t, d), BF16),
                   jax.ShapeDtypeStruct((t, Q_OFF), F32),
                   jax.ShapeDtypeStruct((t, Q_W), BF16),
                   jax.ShapeDtypeStruct(k_ctx.shape, k_ctx.dtype),
                   jax.ShapeDtypeStruct(v_ctx.shape, v_ctx.dtype)],
        input_output_aliases={8: 3, 9: 4},
        compiler_params=_params("arbitrary"),
        name="inproj",
    )(x, mod, g, w_in_b, qg, kg, cos, sin, k_ctx, v_ctx)


def _ctx_kv_kernel(x_ref, mod_ref, g_ref, wk_ref, wv_ref, kg_ref, k_ref, v_ref, *, seq, mod_row):
    k_ref[0, 0:seq, :] = jnp.zeros((seq, KV_W), k_ref.dtype)
    v_ref[0, 0:seq, :] = jnp.zeros((seq, KV_W), v_ref.dtype)
    hc = _modnorm(x_ref[0], g_ref[...], mod_ref, mod_row)

    def store(hh, y):
        k_ref[0, seq:, hh * HEAD_DIM:(hh + 1) * HEAD_DIM] = y
    _qk_head_epilogue(_dot(hc, wk_ref[...]), kg_ref[...], None, None, None, 1.0, store)
    v_ref[0, seq:, :] = _dot(hc, wv_ref[...]).astype(v_ref.dtype)


def _ctx_kv(ctx, mod, g, w_in_b, kg, *, seq, mod_row):
    nb, lc, d = ctx.shape
    out = pl.BlockSpec((1, seq + lc, KV_W), lambda b: (b, 0, 0))
    shape = jax.ShapeDtypeStruct((nb, seq + lc, KV_W), BF16)
    return pl.pallas_call(
        functools.partial(_ctx_kv_kernel, seq=seq, mod_row=mod_row),
        grid=(nb,),
        in_specs=[pl.BlockSpec((1, lc, d), lambda b: (b, 0, 0)),
                  pl.BlockSpec(mod.shape, lambda b: (0, 0)), pl.BlockSpec((1, d), lambda b: (0, 0)),
                  pl.BlockSpec((d, KV_W), lambda b: (0, K_OFF // KV_W)),
                  pl.BlockSpec((d, KV_W), lambda b: (0, V_OFF // KV_W)),
                  pl.BlockSpec((1, HEAD_DIM), lambda b: (0, 0))],
        out_specs=[out, out],
        out_shape=[shape, shape],
        compiler_params=_params("arbitrary"),
        name="ctx_kv",
    )(ctx, mod, g, w_in_b, w_in_b, kg)


def _dot_hi(a, b):
    return jnp.dot(a, b, precision=HIGHEST, preferred_element_type=F32)


def _filter_hidden_kernel(emb_ref, w1_ref, b1_ref, w2_ref, b2_ref, w3_ref, b3_ref, fr_ref, o_ref):
    fr = fr_ref[...]
    h = jnp.sin(fr * (_dot_hi(emb_ref[0], w1_ref[...]) + b1_ref[...]))
    h = jnp.sin(fr * (_dot_hi(h, w2_ref[...]) + b2_ref[...]))
    o_ref[0] = jnp.sin(fr * (_dot_hi(h, w3_ref[...]) + b3_ref[...]))


def _filter_hidden(emb, w1, b1, w2, b2, w3, b3, fr, *, tr=512):
    sides, rows, e = emb.shape
    width = w3.shape[1]
    full = lambda a: pl.BlockSpec(a.shape, lambda s, r: (0,) * a.ndim)
    return pl.pallas_call(
        _filter_hidden_kernel,
        grid=(sides, rows // tr),
        in_specs=[pl.BlockSpec((1, tr, e), lambda s, r: (s, r, 0)),
                  full(w1), full(b1), full(w2), full(b2), full(w3), full(b3), full(fr)],
        out_specs=pl.BlockSpec((1, tr, width), lambda s, r: (s, r, 0)),
        out_shape=jax.ShapeDtypeStruct((sides, rows, width), F32),
        compiler_params=_params("arbitrary", "arbitrary"),
        name="filter_hidden",
    )(emb, w1, b1, w2, b2, w3, b3, fr)


def _filter_kernel(hid_ref, emb_ref, w4f_ref, w4b_ref, dl_ref, k_ref, norm_ref):
    r = pl.program_id(1)
    dl = dl_ref[...]
    kf = _dot_hi(hid_ref[0], w4f_ref[0]) * jnp.exp(-emb_ref[0, :, 0:1] * dl)
    kb = _dot_hi(hid_ref[1], w4b_ref[0]) * jnp.exp(-emb_ref[1, :, 0:1] * dl)
    tr = kb.shape[0]
    row = r * tr + lax.broadcasted_iota(jnp.int32, kb.shape, 0)
    kb = jnp.where(row == 0, 0.0, kb)
    k_ref[0] = kb.astype(k_ref.dtype)
    k_ref[1] = kf.astype(k_ref.dtype)

    @pl.when(r == 0)
    def _():
        norm_ref[...] = jnp.zeros(norm_ref.shape, F32)

    norm_ref[...] += jnp.sum(jnp.abs(kf), axis=0, keepdims=True) + jnp.sum(jnp.abs(kb), axis=0, keepdims=True)


def _hyena_filter(hid, emb, w4, deltas, *, tr=512, tc=256):
    _, L, e = emb.shape
    C = deltas.shape[1]
    half_tiles = L // tr // 2
    width = hid.shape[2]
    return pl.pallas_call(
        _filter_kernel,
        grid=(C // tc, L // tr),
        in_specs=[pl.BlockSpec((2, tr, width), lambda c, r: (0, r % half_tiles, 0)),
                  pl.BlockSpec((2, tr, e), lambda c, r: (0, r, 0)),
                  pl.BlockSpec((1, width, tc), lambda c, r: (r // half_tiles, 0, c)),
                  pl.BlockSpec((1, width, tc), lambda c, r: (r // half_tiles, 0, c + C // tc)),
                  pl.BlockSpec((1, tc), lambda c, r: (0, c))],
        out_specs=[pl.BlockSpec((2, tr, tc), lambda c, r: (0, r, c)),
                   pl.BlockSpec((1, tc), lambda c, r: (0, c))],
        out_shape=[jax.ShapeDtypeStruct((2, L, C), BF16), jax.ShapeDtypeStruct((1, C), F32)],
        compiler_params=_params("arbitrary", "arbitrary"),
        name="hyena_filter",
    )(hid, emb, w4, w4, deltas)


def _hyena_pre_kernel(x0m, x0p, x0n, x1m, x1p, x1n, vm, vp, vn, w0, w1, w2, b0, b1, b2, fb, cm_ref, sf_ref,
                      zr_ref, zi_ref, zb_ref, x0_ref, *, tiles_per_seq):
    pos = pl.program_id(0) % tiles_per_seq
    first = pos == 0
    last = pos == tiles_per_seq - 1
    tr, tc = x0m.shape
    sub = min(tc, PRE_SUB_COLS)
    row = lax.broadcasted_iota(jnp.int32, (tr, sub), 0)

    def conv(m_ref, p_ref, n_ref, w_ref, b_ref, cols):
        u = m_ref[:, cols]
        prev = jnp.where(first, 0.0, p_ref[SUBLANES - 1:SUBLANES, cols])
        nxt = jnp.where(last, 0.0, n_ref[0:1, cols])
        um = jnp.where(row == 0, prev, pltpu.roll(u, 1, 0))
        up = jnp.where(row == tr - 1, nxt, pltpu.roll(u, tr - 1, 0))
        w = w_ref[:, cols]
        return um * w[0:1] + u * w[1:2] + up * w[2:3] + b_ref[:, cols]

    for c in range(0, tc, sub):
        cols = slice(c, c + sub)
        z = conv(x1m, x1p, x1n, w1, b1, cols) * conv(vm, vp, vn, w2, b2, cols)
        zb16 = z.astype(BF16)
        zr_ref[0, :, cols] = _dot(cm_ref[...], zb16).astype(zr_ref.dtype)
        zi_ref[0, :, cols] = _dot(sf_ref[...], zb16).astype(zi_ref.dtype)
        zb_ref[:, cols] = (z * fb[:, cols]).astype(zb_ref.dtype)
        x0_ref[:, cols] = conv(x0m, x0p, x0n, w0, b0, cols).astype(x0_ref.dtype)


def _hyena_pre(u, conv_w, conv_b, fbias, cm, sf, *, seq, tc=1024):
    t = u.shape[0]
    C = HYENA_WIDTH
    tr = cm.shape[0]
    cb = C // tc
    rb = tr // SUBLANES
    last_rb = t // SUBLANES - 1
    in_specs, args = [], []
    for part in range(3):
        off = part * cb
        in_specs += [pl.BlockSpec((tr, tc), lambda i, j, off=off: (i, j + off)),
                     pl.BlockSpec((SUBLANES, tc), lambda i, j, off=off: (jnp.maximum(i * rb - 1, 0), j + off)),
                     pl.BlockSpec((SUBLANES, tc), lambda i, j, off=off: (jnp.minimum((i + 1) * rb, last_rb), j + off))]
        args += [u, u, u]
    for part in range(3):
        in_specs.append(pl.BlockSpec((3, tc), lambda i, j, off=part * cb: (0, j + off)))
        args.append(conv_w)
    for part in range(3):
        in_specs.append(pl.BlockSpec((1, tc), lambda i, j, off=part * cb: (0, j + off)))
        args.append(conv_b)
    in_specs.append(pl.BlockSpec((1, tc), lambda i, j: (0, j)))
    args.append(fbias)
    in_specs += [pl.BlockSpec((tr, tr), lambda i, j: (0, 0))] * 2
    args += [cm, sf]
    out_spec = pl.BlockSpec((tr, tc), lambda i, j: (i, j))
    spec_spec = pl.BlockSpec((1, tr, tc), lambda i, j: (i, 0, j))
    return pl.pallas_call(
        functools.partial(_hyena_pre_kernel, tiles_per_seq=seq // tr),
        grid=(t // tr, cb),
        in_specs=in_specs,
        out_specs=[spec_spec, spec_spec, out_spec, out_spec],
        out_shape=[jax.ShapeDtypeStruct((t // tr, tr, C), BF16),
                   jax.ShapeDtypeStruct((t // tr, tr, C), BF16),
                   jax.ShapeDtypeStruct((t, C), BF16),
                   jax.ShapeDtypeStruct((t, C), BF16)],
        compiler_params=_params("arbitrary", "arbitrary"),
        name="hyena_pre",
    )(*args)


def _filter_spectrum_kernel(cm_ref, sf_ref, seg_ref, norm_ref, gr_ref, gi_ref, pr_sc, pi_sc):
    @pl.when(pl.program_id(1) == 0)
    def _():
        pr_sc[...] = jnp.zeros(pr_sc.shape, F32)
        pi_sc[...] = jnp.zeros(pi_sc.shape, F32)

    p, tc = pr_sc.shape
    sub = min(tc, PRE_SUB_COLS)
    freq = lax.broadcasted_iota(jnp.int32, (p, sub), 0)
    sign = (1 - 2 * (freq & 1)).astype(F32)
    weight = jnp.where(freq == 0, 0.5 / p, 1.0 / p)
    for c in range(0, tc, sub):
        cols = slice(c, c + sub)
        seg = seg_ref[0, :, cols]
        sr = _dot(cm_ref[...], seg)
        si = _dot(sf_ref[...], seg)
        scale = weight / norm_ref[:, cols]
        gr_ref[0, :, cols] = (sr + sign * pr_sc[:, cols]) * scale
        gi_ref[0, :, cols] = (si + sign * pi_sc[:, cols]) * scale
        pr_sc[:, cols] = sr
        pi_sc[:, cols] = si


def _filter_spectrum(cm, sf, taps, norm, *, tc=1024):
    nseg, P, C = taps.shape
    mat = pl.BlockSpec((P, P), lambda c, d: (0, 0))
    out = pl.BlockSpec((1, P, tc), lambda c, d: (jnp.maximum(d - 1, 0), 0, c))
    return pl.pallas_call(
        _filter_spectrum_kernel,
        grid=(C // tc, nseg),
        in_specs=[mat, mat,
                  pl.BlockSpec((1, P, tc), lambda c, d: (d, 0, c)),
                  pl.BlockSpec((1, tc), lambda c, d: (0, c))],
        out_specs=[out, out],
        out_shape=[jax.ShapeDtypeStruct((nseg - 1, P, C), F32)] * 2,
        scratch_shapes=[pltpu.VMEM((P, tc), F32), pltpu.VMEM((P, tc), F32)],
        compiler_params=_params("arbitrary", "arbitrary"),
        name="dft_filter",
    )(cm, sf, taps, norm)


def _conv_kernel(cm_ref, si_ref, zr_ref, zi_ref, gr_ref, gi_ref, x0_ref, zb_ref, o_ref, yr_sc, yi_sc, *, rows):
    nblk, P, tc = zr_ref.shape
    dc = lax.broadcasted_iota(jnp.int32, (rows, tc), 0) == 0

    def out_block(i, carry):
        for r in range(0, P, rows):
            rr = ii = ri = ir = None
            for j in range(nblk):
                lag = i - j + (nblk - 1)
                gr = gr_ref[lag, r:r + rows, :]
                gi = gi_ref[lag, r:r + rows, :]
                zr = zr_ref[j, r:r + rows, :].astype(F32)
                zi = zi_ref[j, r:r + rows, :].astype(F32)
                if j == 0:
                    rr, ii, ri, ir = gr * zr, gi * zi, gr * zi, gi * zr
                else:
                    rr, ii, ri, ir = rr + gr * zr, ii + gi * zi, ri + gr * zi, ir + gi * zr
            if r == 0:
                yr = jnp.where(dc, rr, rr - ii)
                yi = jnp.where(dc, ii, ri + ir)
            else:
                yr = rr - ii
                yi = ri + ir
            yr_sc[r:r + rows, :] = yr.astype(yr_sc.dtype)
            yi_sc[r:r + rows, :] = yi.astype(yi_sc.dtype)
        y = _dot(cm_ref[...], yr_sc[...]) + _dot(si_ref[...], yi_sc[...])
        t0 = pl.multiple_of(i * P, P)
        x0 = x0_ref[pl.ds(t0, P), :].astype(F32)
        o_ref[pl.ds(t0, P), :] = (x0 * (y + zb_ref[pl.ds(t0, P), :].astype(F32))).astype(o_ref.dtype)
        return carry

    lax.fori_loop(0, nblk, out_block, 0)


def _conv(cm, si, zr, zi, gr, gi, x0, zb, *, seq, tc=256, rows=32):
    P = cm.shape[0]
    nblk = seq // P
    nlag = gr.shape[0]
    t, C = x0.shape
    mat = pl.BlockSpec((P, P), lambda c, b: (0, 0))
    zspec = pl.BlockSpec((nblk, P, tc), lambda c, b: (b, 0, c))
    gspec = pl.BlockSpec((nlag, P, tc), lambda c, b: (0, 0, c))
    tile = pl.BlockSpec((seq, tc), lambda c, b: (b, c))
    return pl.pallas_call(
        functools.partial(_conv_kernel, rows=rows),
        grid=(C // tc, t // seq),
        in_specs=[mat, mat, zspec, zspec, gspec, gspec, tile, tile],
        out_specs=tile,
        out_shape=jax.ShapeDtypeStruct((t, C), BF16),
        scratch_shapes=[pltpu.VMEM((P, tc), BF16), pltpu.VMEM((P, tc), BF16)],
        compiler_params=_params("arbitrary", "arbitrary"),
        name="hyena_conv",
    )(cm, si, zr, zi, gr, gi, x0, zb)


def _attn_kernel(q_ref, k_ref, v_ref, o_ref, vt_sc, *, tq, chunks):
    @pl.when(pl.program_id(2) == 0)
    def _():
        vt_sc[0:HEAD_DIM, :] = v_ref[0].T
        vt_sc[HEAD_DIM:, :] = jnp.ones((vt_sc.shape[0] - HEAD_DIM, vt_sc.shape[1]), vt_sc.dtype)

    qs = jnp.concatenate([q_ref[0, :, g * HEAD_DIM:(g + 1) * HEAD_DIM] for g in range(GROUP)], axis=0)
    n = GROUP * tq
    m = jnp.full((1, n), -jnp.inf, F32)
    acc = jnp.zeros((vt_sc.shape[0], n), F32)

    def scores(c):
        start, size = chunks[c]
        return lax.dot_general(k_ref[0, start:start + size, :], qs, (((1,), (1,)), ((), ())),
                               preferred_element_type=F32)

    st_next = scores(0)
    for c, (start, size) in enumerate(chunks):
        st = st_next
        if c + 1 < len(chunks):
            st_next = scores(c + 1)
        m_new = jnp.maximum(m, jnp.max(st, axis=0, keepdims=True))
        alpha = jnp.exp2(m - m_new)
        p = jnp.exp2(st - m_new).astype(BF16)
        acc = alpha * acc + _dot(vt_sc[:, start:start + size], p)
        m = m_new
    out = acc[0:HEAD_DIM] * (1.0 / acc[HEAD_DIM:HEAD_DIM + 1])
    for g in range(GROUP):
        o_ref[0, :, g * HEAD_DIM:(g + 1) * HEAD_DIM] = out[:, g * tq:(g + 1) * tq].T.astype(o_ref.dtype)


def _attention(q, k, v, *, tq=512, tk=512):
    B, L, _ = q.shape
    S = k.shape[1]
    gw = GROUP * HEAD_DIM
    kv_spec = pl.BlockSpec((1, S, HEAD_DIM), lambda b, h, i: (b, 0, h))
    q_spec = pl.BlockSpec((1, tq, gw), lambda b, h, i: (b, i, h))
    chunks = tuple((c, min(tk, S - c)) for c in range(0, S, tk))
    return pl.pallas_call(
        functools.partial(_attn_kernel, tq=tq, chunks=chunks),
        grid=(B, N_KV_HEADS, L // tq),
        in_specs=[q_spec, kv_spec, kv_spec],
        out_specs=q_spec,
        out_shape=jax.ShapeDtypeStruct((B, L, Q_W), BF16),
        scratch_shapes=[pltpu.VMEM((HEAD_DIM + ONES_ROWS, S), BF16)],
        compiler_params=_params("arbitrary", "arbitrary", "arbitrary"),
        name="attention",
    )(q, k, v)


def _merge_kernel(h_ref, ya_ref, yb_ref, wga_ref, wgb_ref, wba_ref, wbb_ref, o_ref):
    h = h_ref[...]
    ya = ya_ref[...]
    yb = yb_ref[...]
    tn = MERGE_TN
    for col in range(0, o_ref.shape[1], tn):
        cols = slice(col, col + tn)
        ga = jax.nn.sigmoid(_dot(h, wga_ref[:, cols]))
        gb = jax.nn.sigmoid(_dot(h, wgb_ref[:, cols]))
        o_ref[:, cols] = (ga * _dot(ya, wba_ref[:, cols]) + gb * _dot(yb, wbb_ref[:, cols])).astype(o_ref.dtype)


def _merge(h, ya, yb, w_in_b, wba, wbb, *, tm=512):
    t, d = h.shape
    row = lambda w: pl.BlockSpec((tm, w), lambda i: (i, 0))
    resident = lambda kdim, blk=0: pl.BlockSpec((kdim, d), lambda i: (0, blk), pipeline_mode=pl.Buffered(1))
    return pl.pallas_call(
        _merge_kernel,
        grid=(t // tm,),
        in_specs=[row(d), row(ya.shape[1]), row(yb.shape[1]),
                  resident(d, GA_OFF // d), resident(d, GB_OFF // d), resident(ya.shape[1]), resident(yb.shape[1])],
        out_specs=row(d),
        out_shape=jax.ShapeDtypeStruct((t, d), BF16),
        compiler_params=_params("arbitrary"),
        name="merge",
    )(h, ya, yb, w_in_b, w_in_b, wba, wbb)


def _outproj_kernel(m_ref, wo_ref, x_ref, mod_ref, g_ref, x1_ref, h2_ref, *, tiles_per_mod):
    b = _mod_row(pl.program_id(0), tiles_per_mod)
    gt1 = mod_ref[pl.ds(b, 1), 2 * D_MODEL:3 * D_MODEL]
    sh2 = mod_ref[pl.ds(b, 1), 3 * D_MODEL:4 * D_MODEL]
    sc2 = mod_ref[pl.ds(b, 1), 4 * D_MODEL:5 * D_MODEL]
    tm = m_ref.shape[0]
    sub = min(tm, OUT_SUB_ROWS)
    g1 = gt1 * g_ref[1:2, :]
    g2 = g_ref[2:3, :] * (1.0 + sc2)
    for r in range(0, tm, sub):
        mix = _dot(m_ref[r:r + sub, :], wo_ref[...])
        x1 = x_ref[r:r + sub, :] + _rms(mix, g1)
        x1_ref[r:r + sub, :] = x1
        h2_ref[r:r + sub, :] = (_rms(x1, g2) + sh2).astype(h2_ref.dtype)


def _outproj(merged, wo, x, mod, gains, *, seq, tm=512):
    t, d = x.shape
    row = pl.BlockSpec((tm, d), lambda i: (i, 0))
    return pl.pallas_call(
        functools.partial(_outproj_kernel, tiles_per_mod=seq // tm),
        grid=(t // tm,),
        in_specs=[row, pl.BlockSpec((d, d), lambda i: (0, 0)), row,
                  pl.BlockSpec(mod.shape, lambda i: (0, 0)),
                  pl.BlockSpec(gains.shape, lambda i: (0, 0))],
        out_specs=[row, row],
        out_shape=[jax.ShapeDtypeStruct((t, d), F32), jax.ShapeDtypeStruct((t, d), BF16)],
        compiler_params=_params("arbitrary"),
        name="outproj",
    )(merged, wo, x, mod, gains)


def _mlp_kernel(h2_ref, w1_ref, w2_ref, x1_ref, mod_ref, g_ref, o_ref, *, tiles_per_mod):
    j = pl.program_id(1)

    @pl.when(j == 0)
    def _():
        o_ref[...] = jnp.zeros(o_ref.shape, F32)

    last = pl.num_programs(1) - 1

    def partial_sum(rows):
        hid = jnp.maximum(_dot(h2_ref[rows, :], w1_ref[...]), 0.0)
        return o_ref[rows, :] + _dot((hid * hid).astype(BF16), w2_ref[...])

    @pl.when(j < last)
    def _():
        o_ref[...] = partial_sum(slice(None))

    @pl.when(j == last)
    def _():
        b = _mod_row(pl.program_id(0), tiles_per_mod)
        gated_gain = mod_ref[pl.ds(b, 1), 5 * D_MODEL:6 * D_MODEL] * g_ref[3:4, :]
        tm = o_ref.shape[0]
        sub = min(tm, OUT_SUB_ROWS)
        for r in range(0, tm, sub):
            rows = slice(r, r + sub)
            o_ref[rows, :] = x1_ref[rows, :] + _rms(partial_sum(rows), gated_gain)


def _mlp(h2, w1, w2, x1, mod, gains, *, seq, tm=1024, tf=MLP_TF):
    t, d = x1.shape
    f = w1.shape[1]
    row = pl.BlockSpec((tm, d), lambda i, j: (i, 0))
    return pl.pallas_call(
        functools.partial(_mlp_kernel, tiles_per_mod=seq // tm),
        grid=(t // tm, f // tf),
        in_specs=[row, pl.BlockSpec((d, tf), lambda i, j: (0, j)), pl.BlockSpec((tf, d), lambda i, j: (j, 0)),
                  row, pl.BlockSpec(mod.shape, lambda i, j: (0, 0)),
                  pl.BlockSpec(gains.shape, lambda i, j: (0, 0))],
        out_specs=row,
        out_shape=jax.ShapeDtypeStruct((t, d), F32),
        compiler_params=_params("arbitrary", "arbitrary"),
        name="mlp",
    )(h2, w1, w2, x1, mod, gains)


def _rope_tables(seq):
    rows = jnp.repeat(jnp.arange(seq // GRID_W), GRID_W)
    cols = jnp.tile(jnp.arange(GRID_W), seq // GRID_W)
    inv = ROPE_THETA ** (-jnp.arange(0, AXIS_DIM, 2, dtype=F32) / AXIS_DIM)
    ar = rows[:, None] * inv
    ac = cols[:, None] * inv
    cos = jnp.concatenate([jnp.cos(ar), jnp.cos(ar), jnp.cos(ac), jnp.cos(ac)], axis=-1)
    sin = jnp.concatenate([-jnp.sin(ar), jnp.sin(ar), -jnp.sin(ac), jnp.sin(ac)], axis=-1)
    return cos, sin


def _filter_embedding(seq):
    t = jnp.linspace(0.0, 1.0, seq, dtype=F32)[:, None]
    wpos = 2.0 * math.pi * jnp.arange(seq, dtype=F32)[:, None] / seq
    bands = jnp.linspace(1e-4, FILTER_BANDS - 1, FILTER_BANDS, dtype=F32)
    emb = jnp.concatenate([t, jnp.cos(bands * wpos), -jnp.sin(bands * wpos)], axis=-1)
    emb = jnp.pad(emb, ((0, 0), (0, EMB_PAD - FILTER_EMB)))
    emb_b = jnp.concatenate([emb[:1], emb[:0:-1]], axis=0)
    return jnp.stack([emb, emb_b])


def _block_diag2(w):
    z = jnp.zeros_like(w)
    return jnp.concatenate([jnp.concatenate([w, z], axis=1), jnp.concatenate([z, w], axis=1)], axis=0)


def _dft_tables(seq):
    idx = jnp.arange(seq, dtype=jnp.int32)
    phase = (idx[:, None] * idx[None, :]) & (2 * seq - 1)
    ang = phase.astype(F32) * (math.pi / seq)
    alt = (1 - 2 * (idx & 1)).astype(F32)
    cm = jnp.cos(ang)
    sn = jnp.sin(ang)
    sf = jnp.where(idx[:, None] == 0, alt[None, :], sn)
    si = jnp.where(idx[None, :] == 0, alt[:, None], sn)
    return cm.astype(BF16), sf.astype(BF16), si.astype(BF16)


def kernel(x, c, ctx, c_ctx, w_ada, b_ada, norm_gains, w_in, conv_w, conv_b, filt_w1, filt_b1, filt_w2, filt_b2, filt_w3, filt_b3, filt_w4, filt_freq, filt_bias, qk_gains, w_branch_a, w_branch_b, w_out, w_ff1, w_ff2):
    B, L, D = x.shape
    T = B * L
    C = HYENA_WIDTH
    lyr = 0
    gains = norm_gains[lyr]
    w_in_b = w_in[lyr].astype(BF16)

    cin = jnp.zeros((MOD_ROWS, D), F32).at[:B].set(c).at[B].set(c_ctx)
    mod = _adaln(cin, w_ada[lyr], b_ada[lyr][None])

    xf = x.reshape(T, D)

    cos, sin = _rope_tables(L)
    qg = qk_gains[lyr, 0][None]
    kg = qk_gains[lyr, 1][None]
    q_scale = HEAD_DIM ** -0.5 * math.log2(math.e)
    k_ctx, v_ctx = _ctx_kv(ctx, mod, gains[0:1], w_in_b, kg, seq=L, mod_row=B)
    h, u, q, k_all, v_all = _inproj(xf, mod, gains[0:1], w_in_b, qg, kg, cos, sin, k_ctx, v_ctx, seq=L,
                                    q_scale=q_scale)

    emb = _filter_embedding(L)
    w1p = jnp.pad(filt_w1[lyr], ((0, EMB_PAD - FILTER_EMB), (0, 0)))
    deltas = jnp.abs(jnp.linspace(MIN_DECAY, MAX_DECAY, C, dtype=F32))[None]
    emb2 = jnp.concatenate([emb[:, :L // 2], emb[:, L // 2:]], axis=-1)
    pair = lambda v: jnp.tile(v, 2)[None]
    w4 = filt_w4[lyr]
    w4_halves = jnp.stack([jnp.concatenate([w4, jnp.zeros_like(w4)]), jnp.concatenate([jnp.zeros_like(w4), w4])])
    hid = _filter_hidden(emb2, _block_diag2(w1p), pair(filt_b1[lyr]), _block_diag2(filt_w2[lyr]), pair(filt_b2[lyr]),
                         _block_diag2(filt_w3[lyr]), pair(filt_b3[lyr]), pair(filt_freq[lyr]))
    filt, norm = _hyena_filter(hid, emb, w4_halves, deltas)
    P = FFT_BLOCK
    cm, sf, si = _dft_tables(P)
    gr, gi = _filter_spectrum(cm, sf, filt.reshape(2 * L // P, P, C), norm)
    zr, zi, zb, x0 = _hyena_pre(u, conv_w[lyr], conv_b[lyr][None], filt_bias[lyr][None], cm, sf, seq=L)
    ya = _conv(cm, si, zr, zi, gr, gi, x0, zb, seq=L)

    yb = _attention(q.reshape(B, L, Q_W), k_all, v_all).reshape(T, Q_W)

    merged = _merge(h, ya, yb, w_in_b, w_branch_a[lyr].astype(BF16), w_branch_b[lyr].astype(BF16))
    x1, h2 = _outproj(merged, w_out[lyr].astype(BF16), xf, mod, gains, seq=L)
    out = _mlp(h2, w_ff1[lyr].astype(BF16), w_ff2[lyr].astype(BF16), x1, mod, gains, seq=L)
    return out.reshape(B, L, D)
```
